```python
import math
import jax
import jax.numpy as jnp
from jax import lax
import numpy as np

D_MODEL = 1024
BATCH = 8
SEQ = 2048
DEPTH = 2
DEC_BATCH = 128
DEC_SEQ = 8
PAST_LEN = 2048
PAGE_SIZE = 128

N_MIXERS = 2
N_FOX_LAYERS = (DEPTH + 1) // 2
N_MLSTM_LAYERS = DEPTH // 2

FOX_HEADS = 16
FOX_HEAD_DIM = D_MODEL // FOX_HEADS
FOX_Q_BLOCK = 128
FOX_IN = 3 * D_MODEL + FOX_HEADS

MLSTM_HEADS = 8
MLSTM_DV = D_MODEL // MLSTM_HEADS
MLSTM_DK = MLSTM_DV // 2
MLSTM_CHUNK = 64
MLSTM_IN = 2 * MLSTM_HEADS * MLSTM_DK + 2 * D_MODEL + 2 * MLSTM_HEADS

MOE_GROUPS = 4
MOE_EXPERTS_PER_GROUP = 8
MOE_EXPERTS = MOE_GROUPS * MOE_EXPERTS_PER_GROUP
MOE_TOP_K = 2
MOE_HIDDEN = 256

RMS_EPS = 1e-6
F32 = jnp.float32

kernel_name = 'hybrid_fox_mlstm_hmoe_step'


def rmsnorm(x, g):
    xf = x.astype(F32)
    y = xf * lax.rsqrt(jnp.mean(xf * xf, axis=-1, keepdims=True) + RMS_EPS)
    return (y * g.astype(F32)).astype(x.dtype)


def gather_pages(pool, page_table):
    pages = pool[page_table]
    return pages.reshape(page_table.shape[0], page_table.shape[1] * pool.shape[1], *pool.shape[2:])


def fox_project(h, w_in, b_f):
    bsz, t, _ = h.shape
    proj = h @ w_in
    q = proj[..., :D_MODEL].reshape(bsz, t, FOX_HEADS, FOX_HEAD_DIM)
    k = proj[..., D_MODEL:2 * D_MODEL].reshape(bsz, t, FOX_HEADS, FOX_HEAD_DIM)
    v = proj[..., 2 * D_MODEL:3 * D_MODEL].reshape(bsz, t, FOX_HEADS, FOX_HEAD_DIM)
    log_f = jax.nn.log_sigmoid(proj[..., 3 * D_MODEL:].astype(F32) + b_f.astype(F32))
    return q, k, v, log_f


def fox_attention(q, k, v, c_q, c_k, q_offset):
    bsz, t_q, n_h, d_h = q.shape
    blk = math.gcd(t_q, FOX_Q_BLOCK)
    n_blk = t_q // blk
    k_pos = jnp.arange(k.shape[1])
    kf = k.astype(F32)
    vf = v.astype(F32)
    ck_h = c_k.astype(F32).transpose(0, 2, 1)
    scale = d_h ** -0.5

    def one_block(bi):
        start = bi * blk
        qb = lax.dynamic_slice_in_dim(q, start, blk, axis=1).astype(F32)
        cb = lax.dynamic_slice_in_dim(c_q, start, blk, axis=1).astype(F32).transpose(0, 2, 1)
        q_pos = q_offset + start + jnp.arange(blk)
        s = jnp.einsum('bqhd,bkhd->bhqk', qb, kf) * scale + (cb[..., :, None] - ck_h[..., None, :])
        s = jnp.where(k_pos[None, :] <= q_pos[:, None], s, -jnp.inf)
        p = jax.nn.softmax(s, axis=-1)
        return jnp.einsum('bhqk,bkhd->bqhd', p, vf)

    out = lax.map(one_block, jnp.arange(n_blk))
    return out.transpose(1, 0, 2, 3, 4).reshape(bsz, t_q, n_h, d_h).astype(q.dtype)


def fox_output(o, w_out):
    bsz, t = o.shape[:2]
    return o.reshape(bsz, t, D_MODEL) @ w_out


def mlstm_chunkwise(q, k, v, i_pre, log_f, c0, n0, m0):
    bsz, t = q.shape[:2]
    chunk = math.gcd(t, MLSTM_CHUNK)
    n_chunks = t // chunk

    def to_chunks(a):
        return a.reshape(bsz, n_chunks, chunk, *a.shape[2:]).swapaxes(0, 1)

    qs = to_chunks(q.astype(F32) * (MLSTM_DK ** -0.5))
    ks = to_chunks(k.astype(F32))
    vs = to_chunks(v.astype(F32))
    i_s = to_chunks(i_pre.astype(F32))
    f_s = to_chunks(log_f.astype(F32))
    causal = jnp.tril(jnp.ones((chunk, chunk), dtype=bool))

    def step(carry, inp):
        c_prev, n_prev, m_prev = carry
        qc, kc, vc, ic, fc = inp
        b = jnp.cumsum(fc, axis=1).transpose(0, 2, 1)
        ih = ic.transpose(0, 2, 1)
        d_mat = jnp.where(causal, b[..., :, None] - b[..., None, :] + ih[..., None, :], -jnp.inf)
        inter = b + m_prev[..., None]
        m_t = jnp.maximum(inter, jnp.max(d_mat, axis=-1))
        w_inter = jnp.exp(inter - m_t)
        s = jnp.exp(d_mat - m_t[..., None]) * jnp.einsum('blhd,bshd->bhls', qc, kc)
        num = jnp.einsum('bhls,bshv->bhlv', s, vc) + w_inter[..., None] * jnp.einsum('blhd,bhdv->bhlv', qc, c_prev)
        den = jnp.sum(s, axis=-1) + w_inter * jnp.einsum('blhd,bhd->bhl', qc, n_prev)
        h = num / jnp.maximum(jnp.abs(den), jnp.exp(-m_t))[..., None]
        b_last = b[..., -1]
        decay = b_last[..., None] - b + ih
        m_new = jnp.maximum(b_last + m_prev, jnp.max(decay, axis=-1))
        w_k = jnp.exp(decay - m_new[..., None])
        carry_scale = jnp.exp(b_last + m_prev - m_new)
        c_new = carry_scale[..., None, None] * c_prev + jnp.einsum('bhs,bshd,bshv->bhdv', w_k, kc, vc)
        n_new = carry_scale[..., None] * n_prev + jnp.einsum('bhs,bshd->bhd', w_k, kc)
        return (c_new, n_new, m_new), h.transpose(0, 2, 1, 3)

    (c_f, n_f, m_f), hs = lax.scan(step, (c0.astype(F32), n0.astype(F32), m0.astype(F32)),
                                   (qs, ks, vs, i_s, f_s))
    h = hs.swapaxes(0, 1).reshape(bsz, t, MLSTM_HEADS, MLSTM_DV)
    return h, c_f, n_f, m_f


def mlstm_mixer(h, w_in, b_i, b_f, head_norm, w_out, c0, n0, m0):
    bsz, t, _ = h.shape
    proj = h @ w_in
    qk = MLSTM_HEADS * MLSTM_DK
    q = proj[..., :qk].reshape(bsz, t, MLSTM_HEADS, MLSTM_DK)
    k = proj[..., qk:2 * qk].reshape(bsz, t, MLSTM_HEADS, MLSTM_DK)
    v = proj[..., 2 * qk:2 * qk + D_MODEL].reshape(bsz, t, MLSTM_HEADS, MLSTM_DV)
    o = proj[..., 2 * qk + D_MODEL:2 * qk + 2 * D_MODEL]
    gates = proj[..., 2 * qk + 2 * D_MODEL:].astype(F32)
    i_pre = gates[..., :MLSTM_HEADS] + b_i.astype(F32)
    log_f = jax.nn.log_sigmoid(gates[..., MLSTM_HEADS:] + b_f.astype(F32))
    hh, c_f, n_f, m_f = mlstm_chunkwise(q, k, v, i_pre, log_f, c0, n0, m0)
    hh = hh * lax.rsqrt(jnp.mean(hh * hh, axis=-1, keepdims=True) + RMS_EPS) * head_norm.astype(F32)
    out = (jax.nn.sigmoid(o.astype(F32)) * hh.reshape(bsz, t, D_MODEL)).astype(h.dtype) @ w_out
    return out, c_f, n_f, m_f


def hier_moe(h, w_group, b_group, w_router, b_router, w_gate, w_up, w_down):
    bsz, t, _ = h.shape
    x = h.reshape(bsz * t, D_MODEL)
    n_tok = x.shape[0]
    g_logits = (x @ w_group).astype(F32) + b_group.astype(F32)
    g_sel = jnp.argmax(g_logits, axis=-1)
    p_g = jnp.max(jax.nn.softmax(g_logits, axis=-1), axis=-1)
    e_logits = ((x @ w_router).astype(F32) + b_router.astype(F32)).reshape(n_tok, MOE_GROUPS, MOE_EXPERTS_PER_GROUP)
    e_in = e_logits[jnp.arange(n_tok), g_sel]
    top_v, top_i = lax.top_k(e_in, MOE_TOP_K)
    top_w = jax.nn.softmax(top_v, axis=-1) * p_g[:, None]
    expert_id = g_sel[:, None] * MOE_EXPERTS_PER_GROUP + top_i
    gates = jnp.sum(jax.nn.one_hot(expert_id, MOE_EXPERTS, dtype=F32) * top_w[..., None], axis=1)
    hg = jnp.einsum('nd,edf->nef', x, w_gate)
    hu = jnp.einsum('nd,edf->nef', x, w_up)
    act = jax.nn.silu(hg) * hu * gates[:, :, None].astype(hg.dtype)
    y = jnp.einsum('nef,efd->nd', act, w_down)
    return y.reshape(bsz, t, D_MODEL)


def setup_inputs(seed: int = 0) -> dict:
    key = jax.random.key(seed)
    ks = jax.random.split(key, 32)
    n_pages = PAST_LEN // PAGE_SIZE
    n_used = DEC_BATCH * n_pages
    n_pool = n_used + max(1, n_used // 4)
    d_in_scale = D_MODEL ** -0.5

    def nrm(k, shape, scale=1.0):
        return jax.random.normal(k, shape, F32) * scale

    x_prompt = nrm(ks[0], (BATCH, SEQ, D_MODEL))
    x_sample = nrm(ks[1], (DEC_BATCH, DEC_SEQ, D_MODEL))
    cache_k = nrm(ks[2], (N_FOX_LAYERS, n_pool, PAGE_SIZE, FOX_HEADS, FOX_HEAD_DIM))
    cache_v = nrm(ks[3], (N_FOX_LAYERS, n_pool, PAGE_SIZE, FOX_HEADS, FOX_HEAD_DIM))
    cache_logf = jax.nn.log_sigmoid(
        jax.random.uniform(ks[4], (N_FOX_LAYERS, n_pool, PAGE_SIZE, FOX_HEADS), F32, 1.0, 5.0)
        + nrm(ks[5], (N_FOX_LAYERS, n_pool, PAGE_SIZE, FOX_HEADS), 0.5))
    page_table = jax.random.permutation(ks[6], n_pool)[:n_used].reshape(DEC_BATCH, n_pages).astype(jnp.int32)
    state_C = nrm(ks[7], (N_MLSTM_LAYERS, DEC_BATCH, MLSTM_HEADS, MLSTM_DK, MLSTM_DV), 0.5)
    state_n = nrm(ks[8], (N_MLSTM_LAYERS, DEC_BATCH, MLSTM_HEADS, MLSTM_DK), 0.5)
    state_m = nrm(ks[9], (N_MLSTM_LAYERS, DEC_BATCH, MLSTM_HEADS))

    norm_mix = 1.0 + nrm(ks[10], (DEPTH, D_MODEL), 0.05)
    norm_ffn = 1.0 + nrm(ks[11], (DEPTH, D_MODEL), 0.05)
    norm_final = 1.0 + nrm(ks[12], (D_MODEL,), 0.05)

    fox_w_in = nrm(ks[13], (N_FOX_LAYERS, D_MODEL, FOX_IN), d_in_scale)
    fox_b_f = jax.random.uniform(ks[14], (N_FOX_LAYERS, FOX_HEADS), F32, 1.0, 5.0)
    fox_w_out = nrm(ks[15], (N_FOX_LAYERS, D_MODEL, D_MODEL), d_in_scale)

    mlstm_w_in = nrm(ks[16], (N_MLSTM_LAYERS, D_MODEL, MLSTM_IN), d_in_scale)
    mlstm_b_i = nrm(ks[17], (N_MLSTM_LAYERS, MLSTM_HEADS), 0.1)
    mlstm_b_f = jax.random.uniform(ks[18], (N_MLSTM_LAYERS, MLSTM_HEADS), F32, 3.0, 6.0)
    mlstm_head_norm = 1.0 + nrm(ks[19], (N_MLSTM_LAYERS, MLSTM_HEADS, MLSTM_DV), 0.05)
    mlstm_w_out = nrm(ks[20], (N_MLSTM_LAYERS, D_MODEL, D_MODEL), d_in_scale)

    moe_w_group = nrm(ks[21], (DEPTH, D_MODEL, MOE_GROUPS), d_in_scale)
    moe_b_group = nrm(ks[22], (DEPTH, MOE_GROUPS), 0.01)
    moe_w_router = nrm(ks[23], (DEPTH, D_MODEL, MOE_EXPERTS), d_in_scale)
    moe_b_router = nrm(ks[24], (DEPTH, MOE_EXPERTS), 0.01)
    moe_w_gate = nrm(ks[25], (DEPTH, MOE_EXPERTS, D_MODEL, MOE_HIDDEN), d_in_scale)
    moe_w_up = nrm(ks[26], (DEPTH, MOE_EXPERTS, D_MODEL, MOE_HIDDEN), d_in_scale)
    moe_w_down = nrm(ks[27], (DEPTH, MOE_EXPERTS, MOE_HIDDEN, D_MODEL), MOE_HIDDEN ** -0.5)

    return {
        'x_prompt': x_prompt, 'x_sample': x_sample,
        'cache_k': cache_k, 'cache_v': cache_v, 'cache_logf': cache_logf, 'page_table': page_table,
        'state_C': state_C, 'state_n': state_n, 'state_m': state_m,
        'norm_mix': norm_mix, 'norm_ffn': norm_ffn, 'norm_final': norm_final,
        'fox_w_in': fox_w_in, 'fox_b_f': fox_b_f, 'fox_w_out': fox_w_out,
        'mlstm_w_in': mlstm_w_in, 'mlstm_b_i': mlstm_b_i, 'mlstm_b_f': mlstm_b_f,
        'mlstm_head_norm': mlstm_head_norm, 'mlstm_w_out': mlstm_w_out,
        'moe_w_group': moe_w_group, 'moe_b_group': moe_b_group,
        'moe_w_router': moe_w_router, 'moe_b_router': moe_b_router,
        'moe_w_gate': moe_w_gate, 'moe_w_up': moe_w_up, 'moe_w_down': moe_w_down,
    }


def reference(x_prompt, x_sample, cache_k, cache_v, cache_logf, page_table, state_C, state_n, state_m,
              norm_mix, norm_ffn, norm_final, fox_w_in, fox_b_f, fox_w_out,
              mlstm_w_in, mlstm_b_i, mlstm_b_f, mlstm_head_norm, mlstm_w_out,
              moe_w_group, moe_b_group, moe_w_router, moe_b_router, moe_w_gate, moe_w_up, moe_w_down):
    xp, xs = x_prompt, x_sample
    k_p, v_p, lf_p, k_s, v_s, lf_s = [], [], [], [], [], []
    c_p, n_p, m_p, c_s, n_s, m_s = [], [], [], [], [], []
    for layer in range(DEPTH):
        hp = rmsnorm(xp, norm_mix[layer])
        hs = rmsnorm(xs, norm_mix[layer])
        j = layer // N_MIXERS
        if layer % N_MIXERS == 0:
            qp, kp, vp, lfp = fox_project(hp, fox_w_in[j], fox_b_f[j])
            cp = jnp.cumsum(lfp, axis=1)
            op = fox_attention(qp, kp, vp, cp, cp, 0)
            qs, kss, vss, lfs = fox_project(hs, fox_w_in[j], fox_b_f[j])
            k_past = gather_pages(cache_k[j], page_table)
            v_past = gather_pages(cache_v[j], page_table)
            lf_past = gather_pages(cache_logf[j], page_table)
            past_len = k_past.shape[1]
            k_all = jnp.concatenate([k_past.astype(kss.dtype), kss], axis=1)
            v_all = jnp.concatenate([v_past.astype(vss.dtype), vss], axis=1)
            c_all = jnp.cumsum(jnp.concatenate([lf_past.astype(F32), lfs], axis=1), axis=1)
            os_ = fox_attention(qs, k_all, v_all, c_all[:, past_len:], c_all, past_len)
            xp = xp + fox_output(op, fox_w_out[j])
            xs = xs + fox_output(os_, fox_w_out[j])
            k_p.append(kp); v_p.append(vp); lf_p.append(lfp)
            k_s.append(kss); v_s.append(vss); lf_s.append(lfs)
        else:
            bp = xp.shape[0]
            zc = jnp.zeros((bp, MLSTM_HEADS, MLSTM_DK, MLSTM_DV), F32)
            zn = jnp.zeros((bp, MLSTM_HEADS, MLSTM_DK), F32)
            zm = jnp.zeros((bp, MLSTM_HEADS), F32)
            op, cpf, npf, mpf = mlstm_mixer(hp, mlstm_w_in[j], mlstm_b_i[j], mlstm_b_f[j],
                                            mlstm_head_norm[j], mlstm_w_out[j], zc, zn, zm)
            os_, csf, nsf, msf = mlstm_mixer(hs, mlstm_w_in[j], mlstm_b_i[j], mlstm_b_f[j],
                                             mlstm_head_norm[j], mlstm_w_out[j],
                                             state_C[j], state_n[j], state_m[j])
            xp = xp + op
            xs = xs + os_
            c_p.append(cpf); n_p.append(npf); m_p.append(mpf)
            c_s.append(csf); n_s.append(nsf); m_s.append(msf)
        xp = xp + hier_moe(rmsnorm(xp, norm_ffn[layer]), moe_w_group[layer], moe_b_group[layer],
                           moe_w_router[layer], moe_b_router[layer],
                           moe_w_gate[layer], moe_w_up[layer], moe_w_down[layer])
        xs = xs + hier_moe(rmsnorm(xs, norm_ffn[layer]), moe_w_group[layer], moe_b_group[layer],
                           moe_w_router[layer], moe_b_router[layer],
                           moe_w_gate[layer], moe_w_up[layer], moe_w_down[layer])
    y_prompt = rmsnorm(xp, norm_final)
    y_sample = rmsnorm(xs, norm_final)
    return (y_prompt, y_sample,
            jnp.stack(k_p), jnp.stack(v_p), jnp.stack(lf_p),
            jnp.stack(k_s), jnp.stack(v_s), jnp.stack(lf_s),
            jnp.stack(c_p), jnp.stack(n_p), jnp.stack(m_p),
            jnp.stack(c_s), jnp.stack(n_s), jnp.stack(m_s))
```

```python
import functools

import jax
import jax.numpy as jnp
from jax import lax
from jax.experimental import pallas as pl
from jax.experimental.pallas import tpu as pltpu

D_MODEL = 1024
FOX_HEADS = 16
FOX_HEAD_DIM = 64
MLSTM_HEADS = 8
MLSTM_DK = 64
MLSTM_DV = 128
MOE_GROUPS = 4
MOE_EXPERTS_PER_GROUP = 8
MOE_EXPERTS = 32
MOE_HIDDEN = 256
RMS_EPS = 1e-6

LANES = 128
VMEM_LIMIT_BYTES = 56 * 1024 * 1024

F32 = jnp.float32
BF16 = jnp.bfloat16
HIGHEST = lax.Precision.HIGHEST
NEG_INF = float("-inf")


def _params(*sem):
    return pltpu.CompilerParams(dimension_semantics=sem, vmem_limit_bytes=VMEM_LIMIT_BYTES)


def _dot(a, b, precision=None):
    return jnp.dot(a, b, preferred_element_type=F32, precision=precision)


def _dot_nt(a, b, precision=None):
    return lax.dot_general(a, b, (((1,), (1,)), ((), ())), preferred_element_type=F32, precision=precision)


def _rmsnorm(x, g):
    return x * lax.rsqrt(jnp.mean(x * x, axis=-1, keepdims=True) + RMS_EPS) * g


def _log_sigmoid(z):
    return jnp.minimum(z, 0.0) - jnp.log1p(jnp.exp(-jnp.abs(z)))


def _sigmoid(z):
    return 1.0 / (1.0 + jnp.exp(-z))


def _upper_tri(n):
    r = lax.broadcasted_iota(jnp.int32, (n, n), 0)
    c = lax.broadcasted_iota(jnp.int32, (n, n), 1)
    return (r <= c).astype(F32)


def _fox_proj_kernel(x_ref, g_ref, w_ref, wf_ref, bf_ref, q_ref, k_ref, v_ref, kb_ref, vb_ref, lf_ref):
    hb = _rmsnorm(x_ref[...], g_ref[...]).astype(BF16)
    q = _dot(hb, w_ref[:, 0:D_MODEL])
    q_ref[...] = (q * (FOX_HEAD_DIM ** -0.5)).astype(q_ref.dtype)
    k = _dot(hb, w_ref[:, D_MODEL:2 * D_MODEL])
    k_ref[...] = k
    kb_ref[...] = k.astype(BF16)
    v = _dot(hb, w_ref[:, 2 * D_MODEL:3 * D_MODEL])
    v_ref[...] = v
    vb_ref[...] = v.astype(BF16)
    z = _dot(hb, wf_ref[...]) + bf_ref[...]
    lf_ref[...] = _log_sigmoid(z)[:, :FOX_HEADS]


def _fox_proj(x, g, w_qkv, w_f, b_f, tm, q_dtype):
    n = x.shape[0]
    row = lambda i: (i, 0)
    const = lambda i: (0, 0)
    return pl.pallas_call(
        _fox_proj_kernel,
        grid=(n // tm,),
        in_specs=[
            pl.BlockSpec((tm, D_MODEL), row),
            pl.BlockSpec((1, D_MODEL), const),
            pl.BlockSpec((D_MODEL, 3 * D_MODEL), const),
            pl.BlockSpec((D_MODEL, LANES), const),
            pl.BlockSpec((1, LANES), const),
        ],
        out_specs=[
            pl.BlockSpec((tm, D_MODEL), row),
            pl.BlockSpec((tm, D_MODEL), row),
            pl.BlockSpec((tm, D_MODEL), row),
            pl.BlockSpec((tm, D_MODEL), row),
            pl.BlockSpec((tm, D_MODEL), row),
            pl.BlockSpec((tm, FOX_HEADS), row),
        ],
        out_shape=[
            jax.ShapeDtypeStruct((n, D_MODEL), q_dtype),
            jax.ShapeDtypeStruct((n, D_MODEL), F32),
            jax.ShapeDtypeStruct((n, D_MODEL), F32),
            jax.ShapeDtypeStruct((n, D_MODEL), BF16),
            jax.ShapeDtypeStruct((n, D_MODEL), BF16),
            jax.ShapeDtypeStruct((n, FOX_HEADS), F32),
        ],
        compiler_params=_params("parallel"),
        name="fox_proj",
    )(x, g, w_qkv, w_f, b_f)


def _mlstm_proj_kernel(x_ref, g_ref, w_ref, wg_ref, bg_ref, q_ref, k_ref, v_ref, o_ref, gate_ref):
    hb = _rmsnorm(x_ref[...], g_ref[...]).astype(BF16)
    qk = MLSTM_HEADS * MLSTM_DK
    q = _dot(hb, w_ref[:, 0:qk])
    q_ref[...] = (q * (MLSTM_DK ** -0.5)).astype(BF16)
    k_ref[...] = _dot(hb, w_ref[:, qk:2 * qk]).astype(BF16)
    v_ref[...] = _dot(hb, w_ref[:, 2 * qk:2 * qk + D_MODEL]).astype(BF16)
    o_ref[...] = _dot(hb, w_ref[:, 2 * qk + D_MODEL:2 * qk + 2 * D_MODEL]).astype(BF16)
    z = _dot(hb, wg_ref[...]) + bg_ref[...]
    lane = lax.broadcasted_iota(jnp.int32, z.shape, 1)
    gates = jnp.where(lane < MLSTM_HEADS, z, _log_sigmoid(z))
    gate_ref[...] = gates[:, :2 * MLSTM_HEADS]


def _mlstm_proj(x, g, w_main, w_g, b_g, tm):
    n = x.shape[0]
    qk = MLSTM_HEADS * MLSTM_DK
    row = lambda i: (i, 0)
    const = lambda i: (0, 0)
    return pl.pallas_call(
        _mlstm_proj_kernel,
        grid=(n // tm,),
        in_specs=[
            pl.BlockSpec((tm, D_MODEL), row),
            pl.BlockSpec((1, D_MODEL), const),
            pl.BlockSpec((D_MODEL, 2 * qk + 2 * D_MODEL), const),
            pl.BlockSpec((D_MODEL, LANES), const),
            pl.BlockSpec((1, LANES), const),
        ],
        out_specs=[
            pl.BlockSpec((tm, qk), row),
            pl.BlockSpec((tm, qk), row),
            pl.BlockSpec((tm, D_MODEL), row),
            pl.BlockSpec((tm, D_MODEL), row),
            pl.BlockSpec((tm, 2 * MLSTM_HEADS), row),
        ],
        out_shape=[
            jax.ShapeDtypeStruct((n, qk), BF16),
            jax.ShapeDtypeStruct((n, qk), BF16),
            jax.ShapeDtypeStruct((n, D_MODEL), BF16),
            jax.ShapeDtypeStruct((n, D_MODEL), BF16),
            jax.ShapeDtypeStruct((n, 2 * MLSTM_HEADS), F32),
        ],
        compiler_params=_params("parallel"),
        name="mlstm_proj",
    )(x, g, w_main, w_g, b_g)


def _cumsum_lanes_kernel(x_ref, o_ref):
    rows, t = x_ref.shape
    tri = _upper_tri(LANES)
    carry = jnp.zeros((rows, 1), F32)
    for j in range(t // LANES):
        blk = _dot(x_ref[:, j * LANES:(j + 1) * LANES], tri, HIGHEST) + carry
        o_ref[:, j * LANES:(j + 1) * LANES] = blk
        carry = blk[:, LANES - 1:LANES]


def _cumsum_lanes(x):
    return pl.pallas_call(
        _cumsum_lanes_kernel,
        out_shape=jax.ShapeDtypeStruct(x.shape, F32),
        compiler_params=_params(),
        name="cumsum_lanes",
    )(x)


def _fox_prompt_kernel(q_ref, k_ref, v_ref, c_ref, o_ref, *, blk):
    qi = pl.program_id(2)
    q2 = q_ref[0]
    lane = lax.broadcasted_iota(jnp.int32, (1, LANES), 1)
    row = lax.broadcasted_iota(jnp.int32, (blk, blk), 0)
    col = lax.broadcasted_iota(jnp.int32, (blk, blk), 1)
    causal = col <= row
    outs = []
    for h in range(2):
        qh = jnp.where(lane // FOX_HEAD_DIM == h, q2, jnp.zeros_like(q2))

        def block(j, carry, masked, qh=qh, h=h):
            m, l, acc = carry
            start = pl.multiple_of(j * blk, blk)
            ks = k_ref[0, pl.ds(start, blk), :]
            vs = v_ref[0, pl.ds(start, blk), :]
            cs = c_ref[0, 0, h:h + 1, pl.ds(start, blk)]
            s = _dot_nt(qh, ks) - cs
            if masked:
                s = jnp.where(causal, s, NEG_INF)
            m_new = jnp.maximum(m, jnp.max(s, axis=1, keepdims=True))
            alpha = jnp.exp(m - m_new)
            p = jnp.exp(s - m_new)
            l = alpha * l + jnp.sum(p, axis=1, keepdims=True)
            acc = alpha * acc + _dot(p.astype(BF16), vs)
            return m_new, l, acc

        init = (jnp.full((blk, 1), NEG_INF, F32), jnp.zeros((blk, 1), F32), jnp.zeros((blk, LANES), F32))
        carry = lax.fori_loop(0, qi, functools.partial(block, masked=False), init)
        m, l, acc = block(qi, carry, True)
        outs.append(acc / l)
    o_ref[0] = jnp.where(lane // FOX_HEAD_DIM == 0, outs[0], outs[1]).astype(o_ref.dtype)


def _fox_prompt_attention(q, k, v, c_t, blk):
    b, t, _ = q.shape
    pairs = FOX_HEADS // 2
    return pl.pallas_call(
        functools.partial(_fox_prompt_kernel, blk=blk),
        grid=(b, pairs, t // blk),
        in_specs=[
            pl.BlockSpec((1, blk, LANES), lambda bi, p, qi: (bi, qi, p)),
            pl.BlockSpec((1, t, LANES), lambda bi, p, qi: (bi, 0, p)),
            pl.BlockSpec((1, t, LANES), lambda bi, p, qi: (bi, 0, p)),
            pl.BlockSpec((1, 1, 2, t), lambda bi, p, qi: (bi, p, 0, 0)),
        ],
        out_specs=pl.BlockSpec((1, blk, LANES), lambda bi, p, qi: (bi, qi, p)),
        out_shape=jax.ShapeDtypeStruct((b, t, D_MODEL), BF16),
        compiler_params=_params("parallel", "parallel", "arbitrary"),
        name="fox_prompt_attention",
    )(q, k, v, c_t)


def _fox_sample_kernel(pt_ref, q_ref, kc_ref, vc_ref, lfc_ref, kn_ref, vn_ref, lfn_ref, o_ref,
                       qbd_ref, m_ref, l_ref, acc_ref, carry_ref, kpad_ref, vpad_ref, lfpad_ref, *, n_pages, page):
    p = pl.program_id(1)
    n_q = q_ref.shape[1]
    chunk = 4 * FOX_HEAD_DIM
    n_chunks = D_MODEL // chunk
    rows = FOX_HEADS * n_q
    crow = 4 * n_q
    lane_c = lax.broadcasted_iota(jnp.int32, (1, chunk), 1)

    @pl.when(p == 0)
    def _init():
        q = q_ref[0]
        for c in range(n_chunks):
            qc = q[:, c * chunk:(c + 1) * chunk]
            pieces = [jnp.where(lane_c // FOX_HEAD_DIM == hl, qc, 0.0) for hl in range(4)]
            qbd_ref[c] = jnp.concatenate(pieces, axis=0).astype(BF16)
        m_ref[...] = jnp.full(m_ref.shape, NEG_INF, F32)
        l_ref[...] = jnp.zeros(l_ref.shape, F32)
        acc_ref[...] = jnp.zeros(acc_ref.shape, F32)
        carry_ref[...] = jnp.zeros(carry_ref.shape, F32)

    eye = (lax.broadcasted_iota(jnp.int32, (FOX_HEADS, FOX_HEADS), 0)
           == lax.broadcasted_iota(jnp.int32, (FOX_HEADS, FOX_HEADS), 1)).astype(F32)

    def process(get_k, get_v, lf, mask):
        lf_t = _dot_nt(eye, lf, HIGHEST)
        cum = _dot(lf_t, _upper_tri(page), HIGHEST) + carry_ref[...]
        carry_ref[...] = cum[:, page - 1:page]
        bias = jnp.concatenate([jnp.broadcast_to(cum[h:h + 1, :], (n_q, page)) for h in range(FOX_HEADS)], axis=0)
        s = jnp.concatenate([_dot_nt(qbd_ref[c], get_k(c)) for c in range(n_chunks)], axis=0) - bias
        if mask is not None:
            s = jnp.where(mask, s, NEG_INF)
        m_old = m_ref[...]
        m_new = jnp.maximum(m_old, jnp.max(s, axis=1, keepdims=True))
        alpha = jnp.exp(m_old - m_new)
        pr = jnp.exp(s - m_new)
        l_ref[...] = alpha * l_ref[...] + jnp.sum(pr, axis=1, keepdims=True)
        m_ref[...] = m_new
        pb = pr.astype(BF16)
        for c in range(n_chunks):
            sl = slice(c * crow, (c + 1) * crow)
            acc_ref[sl, :] = alpha[sl] * acc_ref[sl, :] + _dot(pb[sl], get_v(c))

    @pl.when(p < n_pages)
    def _past():
        process(lambda c: kc_ref[0, :, c * chunk:(c + 1) * chunk].astype(BF16),
                lambda c: vc_ref[0, :, c * chunk:(c + 1) * chunk].astype(BF16),
                lfc_ref[0], None)

    @pl.when(p == n_pages)
    def _new():
        kpad_ref[...] = jnp.zeros(kpad_ref.shape, BF16)
        vpad_ref[...] = jnp.zeros(vpad_ref.shape, BF16)
        lfpad_ref[...] = jnp.zeros(lfpad_ref.shape, F32)
        kpad_ref[0:n_q, :] = kn_ref[0].astype(BF16)
        vpad_ref[0:n_q, :] = vn_ref[0].astype(BF16)
        lfpad_ref[0:n_q, :] = lfn_ref[0]
        key = lax.broadcasted_iota(jnp.int32, (rows, page), 1)
        qry = lax.broadcasted_iota(jnp.int32, (rows, page), 0) % n_q
        process(lambda c: kpad_ref[:, c * chunk:(c + 1) * chunk],
                lambda c: vpad_ref[:, c * chunk:(c + 1) * chunk],
                lfpad_ref[...], key <= qry)
        inv_l = 1.0 / l_ref[...]
        for c in range(n_chunks):
            blk = acc_ref[c * crow:(c + 1) * crow, :] * inv_l[c * crow:(c + 1) * crow]
            out = jnp.zeros((n_q, chunk), F32)
            for hl in range(4):
                out = out + jnp.where(lane_c // FOX_HEAD_DIM == hl, blk[hl * n_q:(hl + 1) * n_q, :], 0.0)
            o_ref[0, :, c * chunk:(c + 1) * chunk] = out


def _fox_sample_attention(page_table, q, cache_k, cache_v, cache_lf, k_new, v_new, lf_new):
    b, n_q, _ = q.shape
    n_pages = page_table.shape[1]
    page = cache_k.shape[1]
    rows = FOX_HEADS * n_q
    chunk = 4 * FOX_HEAD_DIM

    def cache_map(bi, p, pt):
        return (pt[bi, jnp.minimum(p, n_pages - 1)], 0, 0)

    per_b = lambda bi, p, pt: (bi, 0, 0)
    grid_spec = pltpu.PrefetchScalarGridSpec(
        num_scalar_prefetch=1,
        grid=(b, n_pages + 1),
        in_specs=[
            pl.BlockSpec((1, n_q, D_MODEL), per_b),
            pl.BlockSpec((1, page, D_MODEL), cache_map),
            pl.BlockSpec((1, page, D_MODEL), cache_map),
            pl.BlockSpec((1, page, FOX_HEADS), cache_map),
            pl.BlockSpec((1, n_q, D_MODEL), per_b),
            pl.BlockSpec((1, n_q, D_MODEL), per_b),
            pl.BlockSpec((1, n_q, FOX_HEADS), per_b),
        ],
        out_specs=pl.BlockSpec((1, n_q, D_MODEL), per_b),
        scratch_shapes=[
            pltpu.VMEM((D_MODEL // chunk, 4 * n_q, chunk), BF16),
            pltpu.VMEM((rows, 1), F32),
            pltpu.VMEM((rows, 1), F32),
            pltpu.VMEM((rows, chunk), F32),
            pltpu.VMEM((FOX_HEADS, 1), F32),
            pltpu.VMEM((page, D_MODEL), BF16),
            pltpu.VMEM((page, D_MODEL), BF16),
            pltpu.VMEM((page, FOX_HEADS), F32),
        ],
    )
    return pl.pallas_call(
        functools.partial(_fox_sample_kernel, n_pages=n_pages, page=page),
        grid_spec=grid_spec,
        out_shape=jax.ShapeDtypeStruct((b, n_q, D_MODEL), F32),
        compiler_params=_params("parallel", "arbitrary"),
        name="fox_sample_attention",
    )(page_table, q, cache_k, cache_v, cache_lf, k_new, v_new, lf_new)


def _outproj_router_kernel(x_ref, o_ref, w_ref, g_ref, wr_ref, br_ref, xn_ref, h_ref, gate_ref):
    x_new = x_ref[...] + _dot(o_ref[...].astype(BF16), w_ref[...])
    xn_ref[...] = x_new
    h = _rmsnorm(x_new, g_ref[...])
    h_ref[...] = h.astype(BF16)
    logits = _dot(h, wr_ref[...], HIGHEST) + br_ref[...]
    lane = lax.broadcasted_iota(jnp.int32, logits.shape, 1)
    big = jnp.int32(LANES)
    is_group = (lane >= MOE_EXPERTS) & (lane < MOE_EXPERTS + MOE_GROUPS)
    gl = jnp.where(is_group, logits, NEG_INF)
    g_max = jnp.max(gl, axis=1, keepdims=True)
    g_sel = jnp.min(jnp.where(gl == g_max, lane - MOE_EXPERTS, big), axis=1, keepdims=True)
    p_g = 1.0 / jnp.sum(jnp.where(is_group, jnp.exp(gl - g_max), 0.0), axis=1, keepdims=True)
    in_group = (lane < MOE_EXPERTS) & (lane // MOE_EXPERTS_PER_GROUP == g_sel)
    el = jnp.where(in_group, logits, NEG_INF)
    v1 = jnp.max(el, axis=1, keepdims=True)
    i1 = jnp.min(jnp.where(el == v1, lane, big), axis=1, keepdims=True)
    el2 = jnp.where(lane == i1, NEG_INF, el)
    v2 = jnp.max(el2, axis=1, keepdims=True)
    i2 = jnp.min(jnp.where(el2 == v2, lane, big), axis=1, keepdims=True)
    e2 = jnp.exp(v2 - v1)
    w1 = p_g / (1.0 + e2)
    w2 = p_g * e2 / (1.0 + e2)
    gate_ref[...] = jnp.where(lane == i1, w1, 0.0) + jnp.where(lane == i2, w2, 0.0)


def _outproj_router(x, o, w_out, g_ffn, w_r, b_r, tm):
    n = x.shape[0]
    row = lambda i: (i, 0)
    const = lambda i: (0, 0)
    return pl.pallas_call(
        _outproj_router_kernel,
        grid=(n // tm,),
        in_specs=[
            pl.BlockSpec((tm, D_MODEL), row),
            pl.BlockSpec((tm, D_MODEL), row),
            pl.BlockSpec((D_MODEL, D_MODEL), const),
            pl.BlockSpec((1, D_MODEL), const),
            pl.BlockSpec((D_MODEL, LANES), const),
            pl.BlockSpec((1, LANES), const),
        ],
        out_specs=[
            pl.BlockSpec((tm, D_MODEL), row),
            pl.BlockSpec((tm, D_MODEL), row),
            pl.BlockSpec((tm, LANES), row),
        ],
        out_shape=[
            jax.ShapeDtypeStruct((n, D_MODEL), F32),
            jax.ShapeDtypeStruct((n, D_MODEL), BF16),
            jax.ShapeDtypeStruct((n, LANES), F32),
        ],
        compiler_params=_params("parallel"),
        name="outproj_router",
    )(x, o, w_out, g_ffn, w_r, b_r)


def _moe_kernel(x_ref, h_ref, gate_ref, wg_ref, wu_ref, wd_ref, gf_ref, o_ref, acc_ref, *, final_norm):
    e = pl.program_id(1)

    @pl.when(e == 0)
    def _():
        acc_ref[...] = jnp.zeros(acc_ref.shape, F32)

    gates = gate_ref[...]
    lane = lax.broadcasted_iota(jnp.int32, gates.shape, 1)
    gcol = jnp.sum(jnp.where(lane == e, gates, 0.0), axis=1, keepdims=True)
    h = h_ref[...]
    hg = _dot(h, wg_ref[0])
    hu = _dot(h, wu_ref[0])
    act = hg * _sigmoid(hg) * hu * gcol
    acc_ref[...] += _dot(act.astype(BF16), wd_ref[0])

    @pl.when(e == pl.num_programs(1) - 1)
    def _():
        y = x_ref[...] + acc_ref[...]
        if final_norm:
            y = _rmsnorm(y, gf_ref[...])
        o_ref[...] = y


def _moe(x, h, gates, w_gate, w_up, w_down, g_final, tm, final_norm):
    n = x.shape[0]
    row = lambda i, e: (i, 0)
    return pl.pallas_call(
        functools.partial(_moe_kernel, final_norm=final_norm),
        grid=(n // tm, MOE_EXPERTS),
        in_specs=[
            pl.BlockSpec((tm, D_MODEL), row),
            pl.BlockSpec((tm, D_MODEL), row),
            pl.BlockSpec((tm, LANES), row),
            pl.BlockSpec((1, D_MODEL, MOE_HIDDEN), lambda i, e: (e, 0, 0)),
            pl.BlockSpec((1, D_MODEL, MOE_HIDDEN), lambda i, e: (e, 0, 0)),
            pl.BlockSpec((1, MOE_HIDDEN, D_MODEL), lambda i, e: (e, 0, 0)),
            pl.BlockSpec((1, D_MODEL), lambda i, e: (0, 0)),
        ],
        out_specs=pl.BlockSpec((tm, D_MODEL), row),
        out_shape=jax.ShapeDtypeStruct((n, D_MODEL), F32),
        scratch_shapes=[pltpu.VMEM((tm, D_MODEL), F32)],
        compiler_params=_params("parallel", "arbitrary"),
        name="moe_experts",
    )(x, h, gates, w_gate, w_up, w_down, g_final)


def _mlstm_kernel(q_ref, k_ref, kt_ref, v_ref, o_ref, gc_ref, gr_ref, hn_ref, c0_ref, n0_ref, m0_ref,
                  h_ref, c_out_ref, n_out_ref, m_out_ref, c_s, n_s, m_s, *, chunk):
    ci = pl.program_id(2)

    @pl.when(ci == 0)
    def _():
        c_s[...] = c0_ref[0, 0]
        n_s[...] = n0_ref[0, 0]
        m_s[...] = m0_ref[0, 0]

    q2 = q_ref[0]
    k2 = k_ref[0]
    kt2 = kt_ref[0]
    gc = gc_ref[0, 0]
    gr = gr_ref[0, 0]
    c_prev = c_s[...]
    n_prev = n_s[...]
    m_prev_row = m_s[...]
    lane = lax.broadcasted_iota(jnp.int32, (1, LANES), 1)
    srow = lax.broadcasted_iota(jnp.int32, (LANES, 1), 0)
    row = lax.broadcasted_iota(jnp.int32, (chunk, chunk), 0)
    col = lax.broadcasted_iota(jnp.int32, (chunk, chunk), 1)
    causal = col <= row
    b_cols = _dot(causal.astype(F32), gc, HIGHEST)
    b_rows = _dot(gr, (row <= col).astype(F32), HIGHEST)
    c_prev_b = c_prev.astype(BF16)
    c_new = jnp.zeros((LANES, MLSTM_DV), F32)
    n_new = jnp.zeros((1, LANES), F32)
    scale_rows = jnp.zeros((LANES, 1), F32)
    scale_lanes = jnp.zeros((1, LANES), F32)
    m_lanes = jnp.zeros((1, LANES), F32)
    for hl in range(2):
        in_head = lane // MLSTM_DK == hl
        b_col = b_cols[:, 2 + hl:3 + hl]
        b_row = b_rows[2 + hl:3 + hl, :]
        i_row = gr[hl:hl + 1, :]
        m_prev = m_prev_row[:, hl * MLSTM_DK:hl * MLSTM_DK + 1]
        d = jnp.where(causal, b_col - b_row + i_row, NEG_INF)
        inter = b_col + m_prev
        m_t = jnp.maximum(inter, jnp.max(d, axis=1, keepdims=True))
        w_inter = jnp.exp(inter - m_t)
        qh = jnp.where(in_head, q2, jnp.zeros_like(q2))
        s = jnp.exp(d - m_t) * _dot(qh, kt2)
        vh = v_ref[0, :, hl * MLSTM_DV:(hl + 1) * MLSTM_DV]
        num = _dot(s.astype(BF16), vh) + w_inter * _dot(qh, c_prev_b)
        qn = jnp.sum(qh.astype(F32) * n_prev, axis=1, keepdims=True)
        den = jnp.sum(s, axis=1, keepdims=True) + w_inter * qn
        hval = num / jnp.maximum(jnp.abs(den), jnp.exp(-m_t))
        hnorm = hval * lax.rsqrt(jnp.mean(hval * hval, axis=1, keepdims=True) + RMS_EPS)
        hnorm = hnorm * hn_ref[:, hl * MLSTM_DV:(hl + 1) * MLSTM_DV]
        og = o_ref[0, :, hl * MLSTM_DV:(hl + 1) * MLSTM_DV].astype(F32)
        h_ref[0, :, hl * MLSTM_DV:(hl + 1) * MLSTM_DV] = (_sigmoid(og) * hnorm).astype(h_ref.dtype)
        b_last = b_row[:, chunk - 1:chunk]
        decay = b_last - b_row + i_row
        m_new = jnp.maximum(b_last + m_prev, jnp.max(decay, axis=1, keepdims=True))
        w_k = jnp.exp(decay - m_new)
        carry_scale = jnp.exp(b_last + m_prev - m_new)
        ktw = jnp.where(srow // MLSTM_DK == hl, kt2.astype(F32) * w_k, 0.0).astype(BF16)
        c_new = c_new + _dot(ktw, vh)
        kh = jnp.where(in_head, k2, jnp.zeros_like(k2)).astype(F32)
        n_new = n_new + _dot(w_k, kh, HIGHEST)
        scale_rows = jnp.where(srow // MLSTM_DK == hl, carry_scale, scale_rows)
        scale_lanes = jnp.where(in_head, carry_scale, scale_lanes)
        m_lanes = jnp.where(in_head, m_new, m_lanes)
    c_s[...] = scale_rows * c_prev + c_new
    n_s[...] = scale_lanes * n_prev + n_new
    m_s[...] = m_lanes

    @pl.when(ci == pl.num_programs(2) - 1)
    def _():
        c_out_ref[0, 0] = c_s[...]
        n_out_ref[0, 0] = n_s[...]
        m_out_ref[0, 0] = m_s[...]


def _mlstm(q, k, kt, v, o, gcol, grow, head_norm, c0, n0, m0, chunk):
    b, t, _ = q.shape
    pairs = MLSTM_HEADS // 2
    state = lambda bi, p, ci: (bi, p, 0, 0)
    seq = lambda bi, p, ci: (bi, ci, p)
    return pl.pallas_call(
        functools.partial(_mlstm_kernel, chunk=chunk),
        grid=(b, pairs, t // chunk),
        in_specs=[
            pl.BlockSpec((1, chunk, LANES), seq),
            pl.BlockSpec((1, chunk, LANES), seq),
            pl.BlockSpec((1, LANES, chunk), lambda bi, p, ci: (bi, p, ci)),
            pl.BlockSpec((1, chunk, 2 * MLSTM_DV), seq),
            pl.BlockSpec((1, chunk, 2 * MLSTM_DV), seq),
            pl.BlockSpec((1, 1, chunk, 4), lambda bi, p, ci: (bi, p, ci, 0)),
            pl.BlockSpec((1, 1, 4, chunk), lambda bi, p, ci: (bi, p, 0, ci)),
            pl.BlockSpec((1, 2 * MLSTM_DV), lambda bi, p, ci: (0, p)),
            pl.BlockSpec((1, 1, 2 * MLSTM_DK, MLSTM_DV), state),
            pl.BlockSpec((1, 1, 1, 2 * MLSTM_DK), state),
            pl.BlockSpec((1, 1, 1, 2 * MLSTM_DK), state),
        ],
        out_specs=[
            pl.BlockSpec((1, chunk, 2 * MLSTM_DV), seq),
            pl.BlockSpec((1, 1, 2 * MLSTM_DK, MLSTM_DV), state),
            pl.BlockSpec((1, 1, 1, 2 * MLSTM_DK), state),
            pl.BlockSpec((1, 1, 1, 2 * MLSTM_DK), state),
        ],
        out_shape=[
            jax.ShapeDtypeStruct((b, t, D_MODEL), BF16),
            jax.ShapeDtypeStruct((b, pairs, 2 * MLSTM_DK, MLSTM_DV), F32),
            jax.ShapeDtypeStruct((b, pairs, 1, 2 * MLSTM_DK), F32),
            jax.ShapeDtypeStruct((b, pairs, 1, 2 * MLSTM_DK), F32),
        ],
        scratch_shapes=[
            pltpu.VMEM((2 * MLSTM_DK, MLSTM_DV), F32),
            pltpu.VMEM((1, 2 * MLSTM_DK), F32),
            pltpu.VMEM((1, 2 * MLSTM_DK), F32),
        ],
        compiler_params=_params("parallel", "parallel", "arbitrary"),
        name="mlstm_chunkwise",
    )(q, k, kt, v, o, gcol, grow, head_norm, c0, n0, m0)


def _pad_lanes(w, width=LANES):
    return jnp.pad(w, ((0, 0), (0, width - w.shape[1])))


def _router_weights(w_group, b_group, w_router, b_router):
    w = _pad_lanes(jnp.concatenate([w_router, w_group], axis=1).astype(F32))
    b = _pad_lanes(jnp.concatenate([b_router, b_group])[None, :].astype(F32))
    return w, b


def _mlstm_mixer(x, g, w_in, b_i, b_f, head_norm, c0, n0, m0, tm, chunk):
    b, t, _ = x.shape
    qk = MLSTM_HEADS * MLSTM_DK
    pairs = MLSTM_HEADS // 2
    w_main = w_in[:, :2 * qk + 2 * D_MODEL].astype(BF16)
    w_g = _pad_lanes(w_in[:, 2 * qk + 2 * D_MODEL:]).astype(BF16)
    b_g = _pad_lanes(jnp.concatenate([b_i, b_f])[None, :].astype(F32))
    q, k, v, o, gates = _mlstm_proj(x.reshape(b * t, D_MODEL), g, w_main, w_g, b_g, tm)
    q = q.reshape(b, t, qk)
    k = k.reshape(b, t, qk)
    kt = jnp.swapaxes(k, 1, 2)
    v = v.reshape(b, t, D_MODEL)
    o = o.reshape(b, t, D_MODEL)
    gates = gates.reshape(b, t, 2, pairs, 2)
    gcol = gates.transpose(0, 3, 1, 2, 4).reshape(b, pairs, t, 4)
    grow = jnp.swapaxes(gcol, 2, 3)
    c0 = c0.reshape(b, pairs, 2 * MLSTM_DK, MLSTM_DV)
    n0 = n0.reshape(b, pairs, 1, 2 * MLSTM_DK)
    m0 = jnp.repeat(m0.reshape(b, pairs, 1, 2), MLSTM_DK, axis=3)
    h, c_f, n_f, m_f = _mlstm(q, k, kt, v, o, gcol, grow, head_norm.reshape(1, D_MODEL), c0, n0, m0, chunk)
    c_f = c_f.reshape(b, MLSTM_HEADS, MLSTM_DK, MLSTM_DV)
    n_f = n_f.reshape(b, MLSTM_HEADS, MLSTM_DK)
    m_f = m_f.reshape(b, MLSTM_HEADS, MLSTM_DK)[:, :, 0]
    return h.reshape(b * t, D_MODEL), c_f, n_f, m_f


def kernel(x_prompt, x_sample, cache_k, cache_v, cache_logf, page_table, state_C, state_n, state_m, norm_mix, norm_ffn, norm_final, fox_w_in, fox_b_f, fox_w_out, mlstm_w_in, mlstm_b_i, mlstm_b_f, mlstm_head_norm, mlstm_w_out, moe_w_group, moe_b_group, moe_w_router, moe_b_router, moe_w_gate, moe_w_up, moe_w_down):
    bp, tp, _ = x_prompt.shape
    bs, ts, _ = x_sample.shape
    n_p, n_s = bp * tp, bs * ts
    tm_p, tm_s = min(512, n_p), min(512, n_s)
    xp = x_prompt.reshape(n_p, D_MODEL)
    xs = x_sample.reshape(n_s, D_MODEL)
    g_final = norm_final[None, :]

    def ffn(x, o, w_out, layer, tm, final_norm):
        w_r, b_r = _router_weights(moe_w_group[layer], moe_b_group[layer], moe_w_router[layer], moe_b_router[layer])
        x_new, h, gates = _outproj_router(x, o, w_out.astype(BF16), norm_ffn[layer][None, :], w_r, b_r, tm)
        return _moe(x_new, h, gates, moe_w_gate[layer].astype(BF16), moe_w_up[layer].astype(BF16),
                    moe_w_down[layer].astype(BF16), g_final, tm, final_norm)

    g0 = norm_mix[0][None, :]
    w_qkv = fox_w_in[0][:, :3 * D_MODEL].astype(BF16)
    w_f = _pad_lanes(fox_w_in[0][:, 3 * D_MODEL:]).astype(BF16)
    b_f = _pad_lanes(fox_b_f[0][None, :].astype(F32))
    qp, kp, vp, kpb, vpb, lfp = _fox_proj(xp, g0, w_qkv, w_f, b_f, tm_p, BF16)
    qs, ks, vs, _, _, lfs = _fox_proj(xs, g0, w_qkv, w_f, b_f, tm_s, F32)

    lf_t = jnp.swapaxes(lfp.reshape(bp, tp, FOX_HEADS), 1, 2).reshape(bp * FOX_HEADS, tp)
    c_t = _cumsum_lanes(lf_t).reshape(bp, FOX_HEADS // 2, 2, tp)
    op = _fox_prompt_attention(qp.reshape(bp, tp, D_MODEL), kpb.reshape(bp, tp, D_MODEL),
                               vpb.reshape(bp, tp, D_MODEL), c_t, 256)
    n_pool, page = cache_k.shape[1], cache_k.shape[2]
    os_ = _fox_sample_attention(page_table, qs.reshape(bs, ts, D_MODEL),
                                cache_k[0].reshape(n_pool, page, D_MODEL), cache_v[0].reshape(n_pool, page, D_MODEL),
                                cache_logf[0], ks.reshape(bs, ts, D_MODEL), vs.reshape(bs, ts, D_MODEL),
                                lfs.reshape(bs, ts, FOX_HEADS))
    xp = ffn(xp, op.reshape(n_p, D_MODEL), fox_w_out[0], 0, tm_p, False)
    xs = ffn(xs, os_.reshape(n_s, D_MODEL), fox_w_out[0], 0, tm_s, False)

    g1 = norm_mix[1][None, :]
    zc = jnp.zeros((bp, MLSTM_HEADS, MLSTM_DK, MLSTM_DV), F32)
    zn = jnp.zeros((bp, MLSTM_HEADS, MLSTM_DK), F32)
    zm = jnp.zeros((bp, MLSTM_HEADS), F32)
    hp, cpf, npf, mpf = _mlstm_mixer(xp.reshape(bp, tp, D_MODEL), g1, mlstm_w_in[0], mlstm_b_i[0], mlstm_b_f[0],
                                     mlstm_head_norm[0], zc, zn, zm, tm_p, 256)
    hs, csf, nsf, msf = _mlstm_mixer(xs.reshape(bs, ts, D_MODEL), g1, mlstm_w_in[0], mlstm_b_i[0], mlstm_b_f[0],
                                     mlstm_head_norm[0], state_C[0], state_n[0], state_m[0], tm_s, ts)
    yp = ffn(xp, hp, mlstm_w_out[0], 1, tm_p, True)
    ys = ffn(xs, hs, mlstm_w_out[0], 1, tm_s, True)

    hd = (FOX_HEADS, FOX_HEAD_DIM)
    return (yp.reshape(bp, tp, D_MODEL), ys.reshape(bs, ts, D_MODEL),
            kp.reshape(1, bp, tp, *hd), vp.reshape(1, bp, tp, *hd), lfp.reshape(1, bp, tp, FOX_HEADS),
            ks.reshape(1, bs, ts, *hd), vs.reshape(1, bs, ts, *hd), lfs.reshape(1, bs, ts, FOX_HEADS),
            cpf[None], npf[None], mpf[None], csf[None], nsf[None], msf[None])
```

```python
import functools

import jax
import jax.numpy as jnp
from jax import lax
from jax.experimental import pallas as pl
from jax.experimental.pallas import tpu as pltpu

D_MODEL = 1024
FOX_HEADS = 16
FOX_HEAD_DIM = 64
MLSTM_HEADS = 8
MLSTM_DK = 64
MLSTM_DV = 128
MOE_GROUPS = 4
MOE_EXPERTS_PER_GROUP = 8
MOE_EXPERTS = 32
MOE_HIDDEN = 256
RMS_EPS = 1e-6

LANES = 128
VMEM_LIMIT_BYTES = 56 * 1024 * 1024

F32 = jnp.float32
BF16 = jnp.bfloat16
HIGHEST = lax.Precision.HIGHEST
NEG_INF = float("-inf")


def _params(*sem):
    return pltpu.CompilerParams(dimension_semantics=sem, vmem_limit_bytes=VMEM_LIMIT_BYTES)


def _dot(a, b, precision=None):
    return jnp.dot(a, b, preferred_element_type=F32, precision=precision)


def _dot_nt(a, b, precision=None):
    return lax.dot_general(a, b, (((1,), (1,)), ((), ())), preferred_element_type=F32, precision=precision)


def _rmsnorm(x, g):
    return x * lax.rsqrt(jnp.mean(x * x, axis=-1, keepdims=True) + RMS_EPS) * g


def _log_sigmoid(z):
    return jnp.minimum(z, 0.0) - jnp.log1p(jnp.exp(-jnp.abs(z)))


def _sigmoid(z):
    return 1.0 / (1.0 + jnp.exp(-z))


def _upper_tri(n):
    r = lax.broadcasted_iota(jnp.int32, (n, n), 0)
    c = lax.broadcasted_iota(jnp.int32, (n, n), 1)
    return (r <= c).astype(F32)


def _fox_proj_kernel(x_ref, g_ref, w_ref, wf_ref, bf_ref, q_ref, k_ref, v_ref, kb_ref, vb_ref, lf_ref):
    hb = _rmsnorm(x_ref[...], g_ref[...]).astype(BF16)
    q = _dot(hb, w_ref[:, 0:D_MODEL])
    q_ref[...] = (q * (FOX_HEAD_DIM ** -0.5)).astype(q_ref.dtype)
    k = _dot(hb, w_ref[:, D_MODEL:2 * D_MODEL])
    k_ref[...] = k
    kb_ref[...] = k.astype(BF16)
    v = _dot(hb, w_ref[:, 2 * D_MODEL:3 * D_MODEL])
    v_ref[...] = v
    vb_ref[...] = v.astype(BF16)
    z = _dot(hb, wf_ref[...]) + bf_ref[...]
    lf_ref[...] = _log_sigmoid(z)[:, :FOX_HEADS]


def _fox_proj(x, g, w_qkv, w_f, b_f, tm, q_dtype):
    n = x.shape[0]
    row = lambda i: (i, 0)
    const = lambda i: (0, 0)
    return pl.pallas_call(
        _fox_proj_kernel,
        grid=(n // tm,),
        in_specs=[
            pl.BlockSpec((tm, D_MODEL), row),
            pl.BlockSpec((1, D_MODEL), const),
            pl.BlockSpec((D_MODEL, 3 * D_MODEL), const),
            pl.BlockSpec((D_MODEL, LANES), const),
            pl.BlockSpec((1, LANES), const),
        ],
        out_specs=[
            pl.BlockSpec((tm, D_MODEL), row),
            pl.BlockSpec((tm, D_MODEL), row),
            pl.BlockSpec((tm, D_MODEL), row),
            pl.BlockSpec((tm, D_MODEL), row),
            pl.BlockSpec((tm, D_MODEL), row),
            pl.BlockSpec((tm, FOX_HEADS), row),
        ],
        out_shape=[
            jax.ShapeDtypeStruct((n, D_MODEL), q_dtype),
            jax.ShapeDtypeStruct((n, D_MODEL), F32),
            jax.ShapeDtypeStruct((n, D_MODEL), F32),
            jax.ShapeDtypeStruct((n, D_MODEL), BF16),
            jax.ShapeDtypeStruct((n, D_MODEL), BF16),
            jax.ShapeDtypeStruct((n, FOX_HEADS), F32),
        ],
        compiler_params=_params("parallel"),
        name="fox_proj",
    )(x, g, w_qkv, w_f, b_f)


def _mlstm_proj_kernel(x_ref, g_ref, w_ref, wg_ref, bg_ref, q_ref, k_ref, v_ref, o_ref, gate_ref):
    hb = _rmsnorm(x_ref[...], g_ref[...]).astype(BF16)
    qk = MLSTM_HEADS * MLSTM_DK
    q = _dot(hb, w_ref[:, 0:qk])
    q_ref[...] = (q * (MLSTM_DK ** -0.5)).astype(BF16)
    k_ref[...] = _dot(hb, w_ref[:, qk:2 * qk]).astype(BF16)
    v_ref[...] = _dot(hb, w_ref[:, 2 * qk:2 * qk + D_MODEL]).astype(BF16)
    o_ref[...] = _dot(hb, w_ref[:, 2 * qk + D_MODEL:2 * qk + 2 * D_MODEL]).astype(BF16)
    z = _dot(hb, wg_ref[...]) + bg_ref[...]
    lane = lax.broadcasted_iota(jnp.int32, z.shape, 1)
    gates = jnp.where(lane < MLSTM_HEADS, z, _log_sigmoid(z))
    gate_ref[...] = gates[:, :2 * MLSTM_HEADS]


def _mlstm_proj(x, g, w_main, w_g, b_g, tm):
    n = x.shape[0]
    qk = MLSTM_HEADS * MLSTM_DK
    row = lambda i: (i, 0)
    const = lambda i: (0, 0)
    return pl.pallas_call(
        _mlstm_proj_kernel,
        grid=(n // tm,),
        in_specs=[
            pl.BlockSpec((tm, D_MODEL), row),
            pl.BlockSpec((1, D_MODEL), const),
            pl.BlockSpec((D_MODEL, 2 * qk + 2 * D_MODEL), const),
            pl.BlockSpec((D_MODEL, LANES), const),
            pl.BlockSpec((1, LANES), const),
        ],
        out_specs=[
            pl.BlockSpec((tm, qk), row),
            pl.BlockSpec((tm, qk), row),
            pl.BlockSpec((tm, D_MODEL), row),
            pl.BlockSpec((tm, D_MODEL), row),
            pl.BlockSpec((tm, 2 * MLSTM_HEADS), row),
        ],
        out_shape=[
            jax.ShapeDtypeStruct((n, qk), BF16),
            jax.ShapeDtypeStruct((n, qk), BF16),
            jax.ShapeDtypeStruct((n, D_MODEL), BF16),
            jax.ShapeDtypeStruct((n, D_MODEL), BF16),
            jax.ShapeDtypeStruct((n, 2 * MLSTM_HEADS), F32),
        ],
        compiler_params=_params("parallel"),
        name="mlstm_proj",
    )(x, g, w_main, w_g, b_g)


def _cumsum_lanes_kernel(x_ref, o_ref):
    rows, t = x_ref.shape
    tri = _upper_tri(LANES)
    carry = jnp.zeros((rows, 1), F32)
    for j in range(t // LANES):
        blk = _dot(x_ref[:, j * LANES:(j + 1) * LANES], tri, HIGHEST) + carry
        o_ref[:, j * LANES:(j + 1) * LANES] = blk
        carry = blk[:, LANES - 1:LANES]


def _cumsum_lanes(x):
    return pl.pallas_call(
        _cumsum_lanes_kernel,
        out_shape=jax.ShapeDtypeStruct(x.shape, F32),
        compiler_params=_params(),
        name="cumsum_lanes",
    )(x)


def _fox_prompt_kernel(q_ref, k_ref, v_ref, c_ref, o_ref, *, blk):
    qi = pl.program_id(2)
    q2 = q_ref[0]
    lane = lax.broadcasted_iota(jnp.int32, (1, LANES), 1)
    head0 = lane < FOX_HEAD_DIM
    zero = jnp.zeros_like(q2)
    qh = (jnp.where(head0, q2, zero), jnp.where(head0, zero, q2))
    row = lax.broadcasted_iota(jnp.int32, (blk, blk), 0)
    col = lax.broadcasted_iota(jnp.int32, (blk, blk), 1)
    causal = col <= row

    def block(j, carry, masked):
        start = pl.multiple_of(j * blk, blk)
        ks = k_ref[0, pl.ds(start, blk), :]
        vs = v_ref[0, pl.ds(start, blk), :]
        out = []
        for h in range(2):
            m, l, acc = carry[h]
            cs = c_ref[0, 0, h:h + 1, pl.ds(start, blk)]
            s = _dot_nt(qh[h], ks) - cs
            if masked:
                s = jnp.where(causal, s, NEG_INF)
            m_new = jnp.maximum(m, jnp.max(s, axis=1, keepdims=True))
            alpha = jnp.exp(m - m_new)
            p = jnp.exp(s - m_new)
            l = alpha * l + jnp.sum(p, axis=1, keepdims=True)
            acc = alpha * acc + _dot(p.astype(BF16), vs)
            out.append((m_new, l, acc))
        return tuple(out)

    one = (jnp.full((blk, 1), NEG_INF, F32), jnp.zeros((blk, 1), F32), jnp.zeros((blk, LANES), F32))
    carry = lax.fori_loop(0, qi, functools.partial(block, masked=False), (one, one))
    (_, l0, acc0), (_, l1, acc1) = block(qi, carry, True)
    o_ref[0] = jnp.where(head0, acc0 / l0, acc1 / l1).astype(o_ref.dtype)


def _fox_prompt_attention(q, k, v, c_t, blk):
    b, t, _ = q.shape
    pairs = FOX_HEADS // 2
    return pl.pallas_call(
        functools.partial(_fox_prompt_kernel, blk=blk),
        grid=(b, pairs, t // blk),
        in_specs=[
            pl.BlockSpec((1, blk, LANES), lambda bi, p, qi: (bi, qi, p)),
            pl.BlockSpec((1, t, LANES), lambda bi, p, qi: (bi, 0, p)),
            pl.BlockSpec((1, t, LANES), lambda bi, p, qi: (bi, 0, p)),
            pl.BlockSpec((1, 1, 2, t), lambda bi, p, qi: (bi, p, 0, 0)),
        ],
        out_specs=pl.BlockSpec((1, blk, LANES), lambda bi, p, qi: (bi, qi, p)),
        out_shape=jax.ShapeDtypeStruct((b, t, D_MODEL), BF16),
        compiler_params=_params("parallel", "parallel", "arbitrary"),
        name="fox_prompt_attention",
    )(q, k, v, c_t)


def _fox_sample_kernel(pt_ref, q_ref, kc_ref, vc_ref, lfc_ref, kn_ref, vn_ref, lfn_ref, o_ref,
                       m_ref, l_ref, acc_ref, carry_ref, kpad_ref, vpad_ref, lfpad_ref, *, n_pages, page):
    p = pl.program_id(1)
    n_q = q_ref.shape[2]
    rows = FOX_HEADS * n_q

    @pl.when(p == 0)
    def _init():
        m_ref[...] = jnp.full(m_ref.shape, NEG_INF, F32)
        l_ref[...] = jnp.zeros(l_ref.shape, F32)
        acc_ref[...] = jnp.zeros(acc_ref.shape, F32)
        carry_ref[...] = jnp.zeros(carry_ref.shape, F32)

    eye = (lax.broadcasted_iota(jnp.int32, (FOX_HEADS, FOX_HEADS), 0)
           == lax.broadcasted_iota(jnp.int32, (FOX_HEADS, FOX_HEADS), 1)).astype(F32)

    def process(get_k, get_v, lf, mask):
        lf_t = _dot_nt(eye, lf, HIGHEST)
        cum = _dot(lf_t, _upper_tri(page), HIGHEST) + carry_ref[...]
        carry_ref[...] = cum[:, page - 1:page]
        bias = jnp.concatenate([jnp.broadcast_to(cum[h:h + 1, :], (n_q, page)) for h in range(FOX_HEADS)], axis=0)
        s = jnp.concatenate([_dot_nt(q_ref[0, h].astype(BF16), get_k(h)) for h in range(FOX_HEADS)], axis=0) - bias
        if mask is not None:
            s = jnp.where(mask, s, NEG_INF)
        m_old = m_ref[...]
        m_new = jnp.maximum(m_old, jnp.max(s, axis=1, keepdims=True))
        alpha = jnp.exp(m_old - m_new)
        pr = jnp.exp(s - m_new)
        l_ref[...] = alpha * l_ref[...] + jnp.sum(pr, axis=1, keepdims=True)
        m_ref[...] = m_new
        for h in range(FOX_HEADS):
            sl = slice(h * n_q, (h + 1) * n_q)
            acc_ref[sl, :] = alpha[sl] * acc_ref[sl, :] + _dot(pr[sl].astype(BF16), get_v(h))

    @pl.when(p < n_pages)
    def _past():
        process(lambda h: kc_ref[pl.ds(0, 1), pl.ds(h, page, stride=FOX_HEADS), :][0].astype(BF16),
                lambda h: vc_ref[pl.ds(0, 1), pl.ds(h, page, stride=FOX_HEADS), :][0].astype(BF16),
                lfc_ref[0], None)

    @pl.when(p == n_pages)
    def _new():
        kpad_ref[...] = jnp.zeros(kpad_ref.shape, F32)
        vpad_ref[...] = jnp.zeros(vpad_ref.shape, F32)
        lfpad_ref[...] = jnp.zeros(lfpad_ref.shape, F32)
        for h in range(FOX_HEADS):
            kpad_ref[h, 0:n_q, :] = kn_ref[0, h]
            vpad_ref[h, 0:n_q, :] = vn_ref[0, h]
        lfpad_ref[0:n_q, :] = lfn_ref[0]
        key = lax.broadcasted_iota(jnp.int32, (rows, page), 1)
        qry = lax.broadcasted_iota(jnp.int32, (rows, page), 0) % n_q
        process(lambda h: kpad_ref[h].astype(BF16), lambda h: vpad_ref[h].astype(BF16), lfpad_ref[...], key <= qry)
        out = acc_ref[...] / l_ref[...]
        for h in range(FOX_HEADS):
            o_ref[0, h] = out[h * n_q:(h + 1) * n_q, :]


def _fox_sample_attention(page_table, q, cache_k, cache_v, cache_lf, k_new, v_new, lf_new):
    b, _, n_q, _ = q.shape
    n_pages = page_table.shape[1]
    page = cache_lf.shape[1]
    rows = FOX_HEADS * n_q

    def cache_map(bi, p, pt):
        return (pt[bi, jnp.minimum(p, n_pages - 1)], 0, 0)

    per_b4 = lambda bi, p, pt: (bi, 0, 0, 0)
    grid_spec = pltpu.PrefetchScalarGridSpec(
        num_scalar_prefetch=1,
        grid=(b, n_pages + 1),
        in_specs=[
            pl.BlockSpec((1, FOX_HEADS, n_q, FOX_HEAD_DIM), per_b4),
            pl.BlockSpec((1, page * FOX_HEADS, FOX_HEAD_DIM), cache_map),
            pl.BlockSpec((1, page * FOX_HEADS, FOX_HEAD_DIM), cache_map),
            pl.BlockSpec((1, page, FOX_HEADS), cache_map),
            pl.BlockSpec((1, FOX_HEADS, n_q, FOX_HEAD_DIM), per_b4),
            pl.BlockSpec((1, FOX_HEADS, n_q, FOX_HEAD_DIM), per_b4),
            pl.BlockSpec((1, n_q, FOX_HEADS), lambda bi, p, pt: (bi, 0, 0)),
        ],
        out_specs=pl.BlockSpec((1, FOX_HEADS, n_q, FOX_HEAD_DIM), per_b4),
        scratch_shapes=[
            pltpu.VMEM((rows, 1), F32),
            pltpu.VMEM((rows, 1), F32),
            pltpu.VMEM((rows, FOX_HEAD_DIM), F32),
            pltpu.VMEM((FOX_HEADS, 1), F32),
            pltpu.VMEM((FOX_HEADS, page, FOX_HEAD_DIM), F32),
            pltpu.VMEM((FOX_HEADS, page, FOX_HEAD_DIM), F32),
            pltpu.VMEM((page, FOX_HEADS), F32),
        ],
    )
    return pl.pallas_call(
        functools.partial(_fox_sample_kernel, n_pages=n_pages, page=page),
        grid_spec=grid_spec,
        out_shape=jax.ShapeDtypeStruct((b, FOX_HEADS, n_q, FOX_HEAD_DIM), F32),
        compiler_params=_params("parallel", "arbitrary"),
        name="fox_sample_attention",
    )(page_table, q, cache_k, cache_v, cache_lf, k_new, v_new, lf_new)


def _outproj_router_kernel(x_ref, o_ref, w_ref, g_ref, wr_ref, br_ref, xn_ref, h_ref, gate_ref):
    x_new = x_ref[...] + _dot(o_ref[...].astype(BF16), w_ref[...])
    xn_ref[...] = x_new
    h = _rmsnorm(x_new, g_ref[...])
    h_ref[...] = h.astype(BF16)
    logits = _dot(h, wr_ref[...], HIGHEST) + br_ref[...]
    lane = lax.broadcasted_iota(jnp.int32, logits.shape, 1)
    big = jnp.int32(LANES)
    is_group = (lane >= MOE_EXPERTS) & (lane < MOE_EXPERTS + MOE_GROUPS)
    gl = jnp.where(is_group, logits, NEG_INF)
    g_max = jnp.max(gl, axis=1, keepdims=True)
    g_sel = jnp.min(jnp.where(gl == g_max, lane - MOE_EXPERTS, big), axis=1, keepdims=True)
    p_g = 1.0 / jnp.sum(jnp.where(is_group, jnp.exp(gl - g_max), 0.0), axis=1, keepdims=True)
    in_group = (lane < MOE_EXPERTS) & (lane // MOE_EXPERTS_PER_GROUP == g_sel)
    el = jnp.where(in_group, logits, NEG_INF)
    v1 = jnp.max(el, axis=1, keepdims=True)
    i1 = jnp.min(jnp.where(el == v1, lane, big), axis=1, keepdims=True)
    el2 = jnp.where(lane == i1, NEG_INF, el)
    v2 = jnp.max(el2, axis=1, keepdims=True)
    i2 = jnp.min(jnp.where(el2 == v2, lane, big), axis=1, keepdims=True)
    e2 = jnp.exp(v2 - v1)
    w1 = p_g / (1.0 + e2)
    w2 = p_g * e2 / (1.0 + e2)
    gate_ref[...] = jnp.where(lane == i1, w1, 0.0) + jnp.where(lane == i2, w2, 0.0)


def _outproj_router(x, o, w_out, g_ffn, w_r, b_r, tm):
    n = x.shape[0]
    row = lambda i: (i, 0)
    const = lambda i: (0, 0)
    return pl.pallas_call(
        _outproj_router_kernel,
        grid=(n // tm,),
        in_specs=[
            pl.BlockSpec((tm, D_MODEL), row),
            pl.BlockSpec((tm, D_MODEL), row),
            pl.BlockSpec((D_MODEL, D_MODEL), const),
            pl.BlockSpec((1, D_MODEL), const),
            pl.BlockSpec((D_MODEL, LANES), const),
            pl.BlockSpec((1, LANES), const),
        ],
        out_specs=[
            pl.BlockSpec((tm, D_MODEL), row),
            pl.BlockSpec((tm, D_MODEL), row),
            pl.BlockSpec((tm, LANES), row),
        ],
        out_shape=[
            jax.ShapeDtypeStruct((n, D_MODEL), F32),
            jax.ShapeDtypeStruct((n, D_MODEL), BF16),
            jax.ShapeDtypeStruct((n, LANES), F32),
        ],
        compiler_params=_params("parallel"),
        name="outproj_router",
    )(x, o, w_out, g_ffn, w_r, b_r)


def _moe_kernel(x_ref, h_ref, gate_ref, wg_ref, wu_ref, wd_ref, gf_ref, o_ref, acc_ref, *, final_norm):
    e = pl.program_id(1)

    @pl.when(e == 0)
    def _():
        acc_ref[...] = jnp.zeros(acc_ref.shape, F32)

    gates = gate_ref[...]
    lane = lax.broadcasted_iota(jnp.int32, gates.shape, 1)
    gcol = jnp.sum(jnp.where(lane == e, gates, 0.0), axis=1, keepdims=True)
    h = h_ref[...]
    hg = _dot(h, wg_ref[0])
    hu = _dot(h, wu_ref[0])
    act = hg * _sigmoid(hg) * hu * gcol
    acc_ref[...] += _dot(act.astype(BF16), wd_ref[0])

    @pl.when(e == pl.num_programs(1) - 1)
    def _():
        y = x_ref[...] + acc_ref[...]
        if final_norm:
            y = _rmsnorm(y, gf_ref[...])
        o_ref[...] = y


def _moe(x, h, gates, w_gate, w_up, w_down, g_final, tm, final_norm):
    n = x.shape[0]
    row = lambda i, e: (i, 0)
    return pl.pallas_call(
        functools.partial(_moe_kernel, final_norm=final_norm),
        grid=(n // tm, MOE_EXPERTS),
        in_specs=[
            pl.BlockSpec((tm, D_MODEL), row),
            pl.BlockSpec((tm, D_MODEL), row),
            pl.BlockSpec((tm, LANES), row),
            pl.BlockSpec((1, D_MODEL, MOE_HIDDEN), lambda i, e: (e, 0, 0)),
            pl.BlockSpec((1, D_MODEL, MOE_HIDDEN), lambda i, e: (e, 0, 0)),
            pl.BlockSpec((1, MOE_HIDDEN, D_MODEL), lambda i, e: (e, 0, 0)),
            pl.BlockSpec((1, D_MODEL), lambda i, e: (0, 0)),
        ],
        out_specs=pl.BlockSpec((tm, D_MODEL), row),
        out_shape=jax.ShapeDtypeStruct((n, D_MODEL), F32),
        scratch_shapes=[pltpu.VMEM((tm, D_MODEL), F32)],
        compiler_params=_params("parallel", "arbitrary"),
        name="moe_experts",
    )(x, h, gates, w_gate, w_up, w_down, g_final)


def _mlstm_kernel(q_ref, k_ref, kt_ref, v_ref, o_ref, gc_ref, gr_ref, hn_ref, c0_ref, n0_ref, m0_ref,
                  h_ref, c_out_ref, n_out_ref, m_out_ref, c_s, n_s, m_s, *, chunk, pps):
    ci = pl.program_id(2)

    @pl.when(ci == 0)
    def _():
        c_s[...] = c0_ref[0]
        n_s[...] = n0_ref[0]
        m_s[...] = m0_ref[0]

    lane = lax.broadcasted_iota(jnp.int32, (1, LANES), 1)
    srow = lax.broadcasted_iota(jnp.int32, (LANES, 1), 0)
    row = lax.broadcasted_iota(jnp.int32, (chunk, chunk), 0)
    col = lax.broadcasted_iota(jnp.int32, (chunk, chunk), 1)
    causal = col <= row
    lower = causal.astype(F32)
    upper = (row <= col).astype(F32)
    for pp in range(pps):
        q2 = q_ref[0, :, pp * LANES:(pp + 1) * LANES]
        k2 = k_ref[0, :, pp * LANES:(pp + 1) * LANES]
        kt2 = kt_ref[0, pp * LANES:(pp + 1) * LANES, :]
        gc = gc_ref[0, pp]
        gr = gr_ref[0, pp]
        c_prev = c_s[pp]
        n_prev = n_s[pp]
        m_prev_row = m_s[pp]
        b_cols = _dot(lower, gc, HIGHEST)
        b_rows = _dot(gr, upper, HIGHEST)
        c_prev_b = c_prev.astype(BF16)
        c_new = jnp.zeros((LANES, MLSTM_DV), F32)
        n_new = jnp.zeros((1, LANES), F32)
        scale_rows = jnp.zeros((LANES, 1), F32)
        scale_lanes = jnp.zeros((1, LANES), F32)
        m_lanes = jnp.zeros((1, LANES), F32)
        for hl in range(2):
            in_head = lane // MLSTM_DK == hl
            vsl = slice((2 * pp + hl) * MLSTM_DV, (2 * pp + hl + 1) * MLSTM_DV)
            b_col = b_cols[:, 2 + hl:3 + hl]
            b_row = b_rows[2 + hl:3 + hl, :]
            i_row = gr[hl:hl + 1, :]
            m_prev = m_prev_row[:, hl * MLSTM_DK:hl * MLSTM_DK + 1]
            d = jnp.where(causal, b_col - b_row + i_row, NEG_INF)
            inter = b_col + m_prev
            m_t = jnp.maximum(inter, jnp.max(d, axis=1, keepdims=True))
            w_inter = jnp.exp(inter - m_t)
            qh = jnp.where(in_head, q2, jnp.zeros_like(q2))
            s = jnp.exp(d - m_t) * _dot(qh, kt2)
            vh = v_ref[0, :, vsl]
            num = _dot(s.astype(BF16), vh) + w_inter * _dot(qh, c_prev_b)
            qn = jnp.sum(qh.astype(F32) * n_prev, axis=1, keepdims=True)
            den = jnp.sum(s, axis=1, keepdims=True) + w_inter * qn
            hval = num / jnp.maximum(jnp.abs(den), jnp.exp(-m_t))
            hnorm = hval * lax.rsqrt(jnp.mean(hval * hval, axis=1, keepdims=True) + RMS_EPS)
            hnorm = hnorm * hn_ref[:, vsl]
            og = o_ref[0, :, vsl].astype(F32)
            h_ref[0, :, vsl] = (_sigmoid(og) * hnorm).astype(h_ref.dtype)
            b_last = b_row[:, chunk - 1:chunk]
            decay = b_last - b_row + i_row
            m_new = jnp.maximum(b_last + m_prev, jnp.max(decay, axis=1, keepdims=True))
            w_k = jnp.exp(decay - m_new)
            carry_scale = jnp.exp(b_last + m_prev - m_new)
            ktw = jnp.where(srow // MLSTM_DK == hl, kt2.astype(F32) * w_k, 0.0).astype(BF16)
            c_new = c_new + _dot(ktw, vh)
            kh = jnp.where(in_head, k2, jnp.zeros_like(k2)).astype(F32)
            n_new = n_new + _dot(w_k, kh, HIGHEST)
            scale_rows = jnp.where(srow // MLSTM_DK == hl, carry_scale, scale_rows)
            scale_lanes = jnp.where(in_head, carry_scale, scale_lanes)
            m_lanes = jnp.where(in_head, m_new, m_lanes)
        c_s[pp] = scale_rows * c_prev + c_new
        n_s[pp] = scale_lanes * n_prev + n_new
        m_s[pp] = m_lanes

    @pl.when(ci == pl.num_programs(2) - 1)
    def _():
        c_out_ref[0] = c_s[...]
        n_out_ref[0] = n_s[...]
        m_out_ref[0] = m_s[...]


def _mlstm(q, k, kt, v, o, gcol, grow, head_norm, c0, n0, m0, chunk, pps):
    b, t, _ = q.shape
    pairs = MLSTM_HEADS // 2
    state = lambda bi, p, ci: (bi, p, 0, 0)
    seq = lambda bi, p, ci: (bi, ci, p)
    return pl.pallas_call(
        functools.partial(_mlstm_kernel, chunk=chunk, pps=pps),
        grid=(b, pairs // pps, t // chunk),
        in_specs=[
            pl.BlockSpec((1, chunk, pps * LANES), seq),
            pl.BlockSpec((1, chunk, pps * LANES), seq),
            pl.BlockSpec((1, pps * LANES, chunk), lambda bi, p, ci: (bi, p, ci)),
            pl.BlockSpec((1, chunk, pps * 2 * MLSTM_DV), seq),
            pl.BlockSpec((1, chunk, pps * 2 * MLSTM_DV), seq),
            pl.BlockSpec((1, pps, chunk, 4), lambda bi, p, ci: (bi, p, ci, 0)),
            pl.BlockSpec((1, pps, 4, chunk), lambda bi, p, ci: (bi, p, 0, ci)),
            pl.BlockSpec((1, pps * 2 * MLSTM_DV), lambda bi, p, ci: (0, p)),
            pl.BlockSpec((1, pps, 2 * MLSTM_DK, MLSTM_DV), state),
            pl.BlockSpec((1, pps, 1, 2 * MLSTM_DK), state),
            pl.BlockSpec((1, pps, 1, 2 * MLSTM_DK), state),
        ],
        out_specs=[
            pl.BlockSpec((1, chunk, pps * 2 * MLSTM_DV), seq),
            pl.BlockSpec((1, pps, 2 * MLSTM_DK, MLSTM_DV), state),
            pl.BlockSpec((1, pps, 1, 2 * MLSTM_DK), state),
            pl.BlockSpec((1, pps, 1, 2 * MLSTM_DK), state),
        ],
        out_shape=[
            jax.ShapeDtypeStruct((b, t, D_MODEL), BF16),
            jax.ShapeDtypeStruct((b, pairs, 2 * MLSTM_DK, MLSTM_DV), F32),
            jax.ShapeDtypeStruct((b, pairs, 1, 2 * MLSTM_DK), F32),
            jax.ShapeDtypeStruct((b, pairs, 1, 2 * MLSTM_DK), F32),
        ],
        scratch_shapes=[
            pltpu.VMEM((pps, 2 * MLSTM_DK, MLSTM_DV), F32),
            pltpu.VMEM((pps, 1, 2 * MLSTM_DK), F32),
            pltpu.VMEM((pps, 1, 2 * MLSTM_DK), F32),
        ],
        compiler_params=_params("parallel", "parallel", "arbitrary"),
        name="mlstm_chunkwise",
    )(q, k, kt, v, o, gcol, grow, head_norm, c0, n0, m0)


def _pad_lanes(w, width=LANES):
    return jnp.pad(w, ((0, 0), (0, width - w.shape[1])))


def _router_weights(w_group, b_group, w_router, b_router):
    w = _pad_lanes(jnp.concatenate([w_router, w_group], axis=1).astype(F32))
    b = _pad_lanes(jnp.concatenate([b_router, b_group])[None, :].astype(F32))
    return w, b


def _mlstm_mixer(x, g, w_in, b_i, b_f, head_norm, c0, n0, m0, tm, chunk, pps):
    b, t, _ = x.shape
    qk = MLSTM_HEADS * MLSTM_DK
    pairs = MLSTM_HEADS // 2
    w_main = w_in[:, :2 * qk + 2 * D_MODEL].astype(BF16)
    w_g = _pad_lanes(w_in[:, 2 * qk + 2 * D_MODEL:]).astype(BF16)
    b_g = _pad_lanes(jnp.concatenate([b_i, b_f])[None, :].astype(F32))
    q, k, v, o, gates = _mlstm_proj(x.reshape(b * t, D_MODEL), g, w_main, w_g, b_g, tm)
    q = q.reshape(b, t, qk)
    k = k.reshape(b, t, qk)
    kt = jnp.swapaxes(k, 1, 2)
    v = v.reshape(b, t, D_MODEL)
    o = o.reshape(b, t, D_MODEL)
    gates = gates.reshape(b, t, 2, pairs, 2)
    gcol = gates.transpose(0, 3, 1, 2, 4).reshape(b, pairs, t, 4)
    grow = jnp.swapaxes(gcol, 2, 3)
    c0 = c0.reshape(b, pairs, 2 * MLSTM_DK, MLSTM_DV)
    n0 = n0.reshape(b, pairs, 1, 2 * MLSTM_DK)
    m0 = jnp.repeat(m0.reshape(b, pairs, 1, 2), MLSTM_DK, axis=3)
    h, c_f, n_f, m_f = _mlstm(q, k, kt, v, o, gcol, grow, head_norm.reshape(1, D_MODEL), c0, n0, m0, chunk, pps)
    c_f = c_f.reshape(b, MLSTM_HEADS, MLSTM_DK, MLSTM_DV)
    n_f = n_f.reshape(b, MLSTM_HEADS, MLSTM_DK)
    m_f = m_f.reshape(b, MLSTM_HEADS, MLSTM_DK)[:, :, 0]
    return h.reshape(b * t, D_MODEL), c_f, n_f, m_f


def kernel(x_prompt, x_sample, cache_k, cache_v, cache_logf, page_table, state_C, state_n, state_m, norm_mix, norm_ffn, norm_final, fox_w_in, fox_b_f, fox_w_out, mlstm_w_in, mlstm_b_i, mlstm_b_f, mlstm_head_norm, mlstm_w_out, moe_w_group, moe_b_group, moe_w_router, moe_b_router, moe_w_gate, moe_w_up, moe_w_down):
    bp, tp, _ = x_prompt.shape
    bs, ts, _ = x_sample.shape
    n_p, n_s = bp * tp, bs * ts
    tm_p, tm_s = min(512, n_p), min(512, n_s)
    xp = x_prompt.reshape(n_p, D_MODEL)
    xs = x_sample.reshape(n_s, D_MODEL)
    g_final = norm_final[None, :]

    def ffn(x, o, w_out, layer, tm, final_norm):
        w_r, b_r = _router_weights(moe_w_group[layer], moe_b_group[layer], moe_w_router[layer], moe_b_router[layer])
        x_new, h, gates = _outproj_router(x, o, w_out.astype(BF16), norm_ffn[layer][None, :], w_r, b_r, tm)
        return _moe(x_new, h, gates, moe_w_gate[layer].astype(BF16), moe_w_up[layer].astype(BF16),
                    moe_w_down[layer].astype(BF16), g_final, tm, final_norm)

    g0 = norm_mix[0][None, :]
    w_qkv = fox_w_in[0][:, :3 * D_MODEL].astype(BF16)
    w_f = _pad_lanes(fox_w_in[0][:, 3 * D_MODEL:]).astype(BF16)
    b_f = _pad_lanes(fox_b_f[0][None, :].astype(F32))
    qp, kp, vp, kpb, vpb, lfp = _fox_proj(xp, g0, w_qkv, w_f, b_f, tm_p, BF16)
    qs, ks, vs, _, _, lfs = _fox_proj(xs, g0, w_qkv, w_f, b_f, tm_s, F32)

    lf_t = jnp.swapaxes(lfp.reshape(bp, tp, FOX_HEADS), 1, 2).reshape(bp * FOX_HEADS, tp)
    c_t = _cumsum_lanes(lf_t).reshape(bp, FOX_HEADS // 2, 2, tp)
    op = _fox_prompt_attention(qp.reshape(bp, tp, D_MODEL), kpb.reshape(bp, tp, D_MODEL),
                               vpb.reshape(bp, tp, D_MODEL), c_t, min(512, tp))
    n_pool, page = cache_k.shape[1], cache_k.shape[2]
    heads_major = lambda a: a.reshape(bs, ts, FOX_HEADS, FOX_HEAD_DIM).transpose(0, 2, 1, 3)
    os_ = _fox_sample_attention(page_table, heads_major(qs),
                                cache_k[0].reshape(n_pool, page * FOX_HEADS, FOX_HEAD_DIM),
                                cache_v[0].reshape(n_pool, page * FOX_HEADS, FOX_HEAD_DIM),
                                cache_logf[0], heads_major(ks), heads_major(vs), lfs.reshape(bs, ts, FOX_HEADS))
    os_ = os_.transpose(0, 2, 1, 3)
    xp = ffn(xp, op.reshape(n_p, D_MODEL), fox_w_out[0], 0, tm_p, False)
    xs = ffn(xs, os_.reshape(n_s, D_MODEL), fox_w_out[0], 0, tm_s, False)

    g1 = norm_mix[1][None, :]
    zc = jnp.zeros((bp, MLSTM_HEADS, MLSTM_DK, MLSTM_DV), F32)
    zn = jnp.zeros((bp, MLSTM_HEADS, MLSTM_DK), F32)
    zm = jnp.zeros((bp, MLSTM_HEADS), F32)
    hp, cpf, npf, mpf = _mlstm_mixer(xp.reshape(bp, tp, D_MODEL), g1, mlstm_w_in[0], mlstm_b_i[0], mlstm_b_f[0],
                                     mlstm_head_norm[0], zc, zn, zm, tm_p, 256, 1)
    hs, csf, nsf, msf = _mlstm_mixer(xs.reshape(bs, ts, D_MODEL), g1, mlstm_w_in[0], mlstm_b_i[0], mlstm_b_f[0],
                                     mlstm_head_norm[0], state_C[0], state_n[0], state_m[0], tm_s, ts, MLSTM_HEADS // 2)
    yp = ffn(xp, hp, mlstm_w_out[0], 1, tm_p, True)
    ys = ffn(xs, hs, mlstm_w_out[0], 1, tm_s, True)

    hd = (FOX_HEADS, FOX_HEAD_DIM)
    return (yp.reshape(bp, tp, D_MODEL), ys.reshape(bs, ts, D_MODEL),
            kp.reshape(1, bp, tp, *hd), vp.reshape(1, bp, tp, *hd), lfp.reshape(1, bp, tp, FOX_HEADS),
            ks.reshape(1, bs, ts, *hd), vs.reshape(1, bs, ts, *hd), lfs.reshape(1, bs, ts, FOX_HEADS),
            cpf[None], npf[None], mpf[None], csf[None], nsf[None], msf[None])
```

```python
import functools

import jax
import jax.numpy as jnp
from jax import lax
from jax.experimental import pallas as pl
from jax.experimental.pallas import tpu as pltpu

D_MODEL = 1024
FOX_HEADS = 16
FOX_HEAD_DIM = 64
MLSTM_HEADS = 8
MLSTM_DK = 64
MLSTM_DV = 128
MLSTM_QK = MLSTM_HEADS * MLSTM_DK
MOE_GROUPS = 4
MOE_EXPERTS_PER_GROUP = 8
MOE_EXPERTS = 32
MOE_HIDDEN = 256
RMS_EPS = 1e-6

LANES = 128
VMEM_LIMIT_BYTES = 56 * 1024 * 1024

F32 = jnp.float32
BF16 = jnp.bfloat16
HIGHEST = lax.Precision.HIGHEST
NEG_INF = float("-inf")


def _params(*sem):
    return pltpu.CompilerParams(dimension_semantics=sem, vmem_limit_bytes=VMEM_LIMIT_BYTES)


def _dot(a, b, precision=None):
    return jnp.dot(a, b, preferred_element_type=F32, precision=precision)


def _dot_nt(a, b, precision=None):
    return lax.dot_general(a, b, (((1,), (1,)), ((), ())), preferred_element_type=F32, precision=precision)


def _rmsnorm(x, g):
    return x * lax.rsqrt(jnp.mean(x * x, axis=-1, keepdims=True) + RMS_EPS) * g


def _log_sigmoid(z):
    return jnp.minimum(z, 0.0) - jnp.log1p(jnp.exp(-jnp.abs(z)))


def _sigmoid(z):
    return 1.0 / (1.0 + jnp.exp(-z))


def _upper_tri(n):
    r = lax.broadcasted_iota(jnp.int32, (n, n), 0)
    c = lax.broadcasted_iota(jnp.int32, (n, n), 1)
    return (r <= c).astype(F32)


def _fox_proj_kernel(x_ref, g_ref, wq_ref, wk_ref, wv_ref, wf_ref, bf_ref,
                     q_ref, k_ref, v_ref, kb_ref, vb_ref, lf_ref, *, transposed):
    hb = _rmsnorm(x_ref[...], g_ref[...]).astype(BF16)
    q = _dot(hb, wq_ref[...])
    q_ref[...] = (q * (FOX_HEAD_DIM ** -0.5)).astype(q_ref.dtype)
    if transposed:
        k = _dot_nt(wk_ref[...], hb)
        v = _dot_nt(wv_ref[...], hb)
        z = _dot_nt(wf_ref[...], hb) + bf_ref[...]
        k_ref[0] = k
        kb_ref[0] = k.astype(BF16)
        v_ref[0] = v
        vb_ref[0] = v.astype(BF16)
        lf_ref[0] = _log_sigmoid(z)[:FOX_HEADS, :]
    else:
        k = _dot(hb, wk_ref[...])
        v = _dot(hb, wv_ref[...])
        z = _dot(hb, wf_ref[...]) + bf_ref[...]
        k_ref[...] = k
        kb_ref[...] = k.astype(BF16)
        v_ref[...] = v
        vb_ref[...] = v.astype(BF16)
        lf_ref[...] = _log_sigmoid(z)[:, :FOX_HEADS]


def _fox_proj(x, g, wq, wk, wv, wf, bf, tm, q_dtype, seq_len=None):
    n = x.shape[0]
    transposed = seq_len is not None
    row = lambda i: (i, 0)
    const = lambda i: (0, 0)
    if transposed:
        nt = seq_len // tm
        b = n // seq_len
        tmap = lambda i: (i // nt, 0, i % nt)
        kv_spec = pl.BlockSpec((1, D_MODEL, tm), tmap)
        lf_spec = pl.BlockSpec((1, FOX_HEADS, tm), tmap)
        kv_shape = (b, D_MODEL, seq_len)
        lf_shape = (b, FOX_HEADS, seq_len)
    else:
        kv_spec = pl.BlockSpec((tm, D_MODEL), row)
        lf_spec = pl.BlockSpec((tm, FOX_HEADS), row)
        kv_shape = (n, D_MODEL)
        lf_shape = (n, FOX_HEADS)
    return pl.pallas_call(
        functools.partial(_fox_proj_kernel, transposed=transposed),
        grid=(n // tm,),
        in_specs=[
            pl.BlockSpec((tm, D_MODEL), row),
            pl.BlockSpec((1, D_MODEL), const),
            pl.BlockSpec(wq.shape, const),
            pl.BlockSpec(wk.shape, const),
            pl.BlockSpec(wv.shape, const),
            pl.BlockSpec(wf.shape, const),
            pl.BlockSpec(bf.shape, const),
        ],
        out_specs=[pl.BlockSpec((tm, D_MODEL), row), kv_spec, kv_spec, kv_spec, kv_spec, lf_spec],
        out_shape=[
            jax.ShapeDtypeStruct((n, D_MODEL), q_dtype),
            jax.ShapeDtypeStruct(kv_shape, F32),
            jax.ShapeDtypeStruct(kv_shape, F32),
            jax.ShapeDtypeStruct(kv_shape, BF16),
            jax.ShapeDtypeStruct(kv_shape, BF16),
            jax.ShapeDtypeStruct(lf_shape, F32),
        ],
        compiler_params=_params("parallel"),
        name="fox_proj",
    )(x, g, wq, wk, wv, wf, bf)


def _mlstm_proj_kernel(x_ref, g_ref, w_ref, wk_ref, wg_ref, bg_ref, q_ref, k_ref, v_ref, o_ref, gate_ref, *, transposed):
    hb = _rmsnorm(x_ref[...], g_ref[...]).astype(BF16)
    q = _dot(hb, w_ref[:, 0:MLSTM_QK])
    q_ref[...] = (q * (MLSTM_DK ** -0.5)).astype(BF16)
    if transposed:
        k_ref[0] = _dot_nt(wk_ref[...], hb).astype(BF16)
    else:
        k_ref[...] = _dot(hb, wk_ref[...]).astype(BF16)
    v_ref[...] = _dot(hb, w_ref[:, MLSTM_QK:MLSTM_QK + D_MODEL]).astype(BF16)
    o_ref[...] = _dot(hb, w_ref[:, MLSTM_QK + D_MODEL:MLSTM_QK + 2 * D_MODEL]).astype(BF16)
    z = _dot(hb, wg_ref[...]) + bg_ref[...]
    lane = lax.broadcasted_iota(jnp.int32, z.shape, 1)
    gates = jnp.where(lane < MLSTM_HEADS, z, _log_sigmoid(z))
    gate_ref[...] = gates[:, :2 * MLSTM_HEADS]


def _mlstm_proj(x, g, w_qvo, wk, w_g, b_g, tm, seq_len=None):
    n = x.shape[0]
    transposed = seq_len is not None
    row = lambda i: (i, 0)
    const = lambda i: (0, 0)
    if transposed:
        nt = seq_len // tm
        k_spec = pl.BlockSpec((1, MLSTM_QK, tm), lambda i: (i // nt, 0, i % nt))
        k_shape = (n // seq_len, MLSTM_QK, seq_len)
    else:
        k_spec = pl.BlockSpec((tm, MLSTM_QK), row)
        k_shape = (n, MLSTM_QK)
    return pl.pallas_call(
        functools.partial(_mlstm_proj_kernel, transposed=transposed),
        grid=(n // tm,),
        in_specs=[
            pl.BlockSpec((tm, D_MODEL), row),
            pl.BlockSpec((1, D_MODEL), const),
            pl.BlockSpec(w_qvo.shape, const),
            pl.BlockSpec(wk.shape, const),
            pl.BlockSpec((D_MODEL, LANES), const),
            pl.BlockSpec((1, LANES), const),
        ],
        out_specs=[
            pl.BlockSpec((tm, MLSTM_QK), row),
            k_spec,
            pl.BlockSpec((tm, D_MODEL), row),
            pl.BlockSpec((tm, D_MODEL), row),
            pl.BlockSpec((tm, 2 * MLSTM_HEADS), row),
        ],
        out_shape=[
            jax.ShapeDtypeStruct((n, MLSTM_QK), BF16),
            jax.ShapeDtypeStruct(k_shape, BF16),
            jax.ShapeDtypeStruct((n, D_MODEL), BF16),
            jax.ShapeDtypeStruct((n, D_MODEL), BF16),
            jax.ShapeDtypeStruct((n, 2 * MLSTM_HEADS), F32),
        ],
        compiler_params=_params("parallel"),
        name="mlstm_proj",
    )(x, g, w_qvo, wk, w_g, b_g)


def _cumsum_lanes_kernel(x_ref, o_ref):
    rows, t = x_ref.shape
    tri = _upper_tri(LANES)
    carry = jnp.zeros((rows, 1), F32)
    for j in range(t // LANES):
        blk = _dot(x_ref[:, j * LANES:(j + 1) * LANES], tri, HIGHEST) + carry
        o_ref[:, j * LANES:(j + 1) * LANES] = blk
        carry = blk[:, LANES - 1:LANES]


def _cumsum_lanes(x):
    return pl.pallas_call(
        _cumsum_lanes_kernel,
        out_shape=jax.ShapeDtypeStruct(x.shape, F32),
        compiler_params=_params(),
        name="cumsum_lanes",
    )(x)


def _fox_prompt_kernel(q_ref, kt_ref, vt_ref, c_ref, o_ref, *, blk):
    qi = pl.program_id(2)
    q2 = q_ref[0]
    lane = lax.broadcasted_iota(jnp.int32, (1, LANES), 1)
    head0 = lane < FOX_HEAD_DIM
    zero = jnp.zeros_like(q2)
    qh = (jnp.where(head0, q2, zero), jnp.where(head0, zero, q2))
    row = lax.broadcasted_iota(jnp.int32, (blk, blk), 0)
    col = lax.broadcasted_iota(jnp.int32, (blk, blk), 1)
    causal = col <= row

    def block(j, carry, masked):
        start = pl.multiple_of(j * blk, blk)
        kt = kt_ref[0, :, pl.ds(start, blk)]
        vt = vt_ref[0, :, pl.ds(start, blk)]
        out = []
        for h in range(2):
            m, l, acc = carry[h]
            cs = c_ref[0, 0, h:h + 1, pl.ds(start, blk)]
            s = _dot(qh[h], kt) - cs
            if masked:
                s = jnp.where(causal, s, NEG_INF)
            m_new = jnp.maximum(m, jnp.max(s, axis=1, keepdims=True))
            alpha = jnp.exp(m - m_new)
            p = jnp.exp(s - m_new)
            l = alpha * l + jnp.sum(p, axis=1, keepdims=True)
            acc = alpha * acc + _dot_nt(p.astype(BF16), vt)
            out.append((m_new, l, acc))
        return tuple(out)

    one = (jnp.full((blk, 1), NEG_INF, F32), jnp.zeros((blk, 1), F32), jnp.zeros((blk, LANES), F32))
    carry = lax.fori_loop(0, qi, functools.partial(block, masked=False), (one, one))
    (_, l0, acc0), (_, l1, acc1) = block(qi, carry, True)
    o_ref[0] = jnp.where(head0, acc0 / l0, acc1 / l1).astype(o_ref.dtype)


def _fox_prompt_attention(q, kt, vt, c_t, blk):
    b, t, _ = q.shape
    pairs = FOX_HEADS // 2
    return pl.pallas_call(
        functools.partial(_fox_prompt_kernel, blk=blk),
        grid=(b, pairs, t // blk),
        in_specs=[
            pl.BlockSpec((1, blk, LANES), lambda bi, p, qi: (bi, qi, p)),
            pl.BlockSpec((1, LANES, t), lambda bi, p, qi: (bi, p, 0)),
            pl.BlockSpec((1, LANES, t), lambda bi, p, qi: (bi, p, 0)),
            pl.BlockSpec((1, 1, 2, t), lambda bi, p, qi: (bi, p, 0, 0)),
        ],
        out_specs=pl.BlockSpec((1, blk, LANES), lambda bi, p, qi: (bi, qi, p)),
        out_shape=jax.ShapeDtypeStruct((b, t, D_MODEL), BF16),
        compiler_params=_params("parallel", "parallel", "arbitrary"),
        name="fox_prompt_attention",
    )(q, kt, vt, c_t)


def _fox_sample_kernel(pt_ref, q_ref, kc_ref, vc_ref, lfc_ref, kn_ref, vn_ref, lfn_ref, o_ref,
                       qbd_ref, m_ref, l_ref, acc_ref, carry_ref, kpad_ref, vpad_ref, lfpad_ref, *, n_pages, page):
    p = pl.program_id(1)
    n_q = q_ref.shape[1]
    chunk = 4 * FOX_HEAD_DIM
    n_chunks = D_MODEL // chunk
    rows = FOX_HEADS * n_q
    crow = 4 * n_q
    lane_c = lax.broadcasted_iota(jnp.int32, (1, chunk), 1)

    @pl.when(p == 0)
    def _init():
        q = q_ref[0]
        for c in range(n_chunks):
            qc = q[:, c * chunk:(c + 1) * chunk]
            pieces = [jnp.where(lane_c // FOX_HEAD_DIM == hl, qc, 0.0) for hl in range(4)]
            qbd_ref[c] = jnp.concatenate(pieces, axis=0).astype(BF16)
        m_ref[...] = jnp.full(m_ref.shape, NEG_INF, F32)
        l_ref[...] = jnp.zeros(l_ref.shape, F32)
        acc_ref[...] = jnp.zeros(acc_ref.shape, F32)
        carry_ref[...] = jnp.zeros(carry_ref.shape, F32)

    def process(get_kt, get_vt, lf_t, mask):
        cum = _dot(lf_t, _upper_tri(page), HIGHEST) + carry_ref[...]
        carry_ref[...] = cum[:, page - 1:page]
        bias = jnp.concatenate([jnp.broadcast_to(cum[h:h + 1, :], (n_q, page)) for h in range(FOX_HEADS)], axis=0)
        s = jnp.concatenate([_dot(qbd_ref[c], get_kt(c)) for c in range(n_chunks)], axis=0) - bias
        if mask is not None:
            s = jnp.where(mask, s, NEG_INF)
        m_old = m_ref[...]
        m_new = jnp.maximum(m_old, jnp.max(s, axis=1, keepdims=True))
        alpha = jnp.exp(m_old - m_new)
        pr = jnp.exp(s - m_new)
        l_ref[...] = alpha * l_ref[...] + jnp.sum(pr, axis=1, keepdims=True)
        m_ref[...] = m_new
        pb = pr.astype(BF16)
        for c in range(n_chunks):
            sl = slice(c * crow, (c + 1) * crow)
            acc_ref[sl, :] = alpha[sl] * acc_ref[sl, :] + _dot_nt(pb[sl], get_vt(c))

    @pl.when(p < n_pages)
    def _past():
        process(lambda c: kc_ref[0, c * chunk:(c + 1) * chunk, :].astype(BF16),
                lambda c: vc_ref[0, c * chunk:(c + 1) * chunk, :].astype(BF16),
                lfc_ref[0], None)

    @pl.when(p == n_pages)
    def _new():
        kpad_ref[...] = jnp.zeros(kpad_ref.shape, F32)
        vpad_ref[...] = jnp.zeros(vpad_ref.shape, F32)
        lfpad_ref[...] = jnp.zeros(lfpad_ref.shape, F32)
        kpad_ref[:, 0:n_q] = kn_ref[0]
        vpad_ref[:, 0:n_q] = vn_ref[0]
        lfpad_ref[:, 0:n_q] = lfn_ref[0]
        key = lax.broadcasted_iota(jnp.int32, (rows, page), 1)
        qry = lax.broadcasted_iota(jnp.int32, (rows, page), 0) % n_q
        process(lambda c: kpad_ref[c * chunk:(c + 1) * chunk, :].astype(BF16),
                lambda c: vpad_ref[c * chunk:(c + 1) * chunk, :].astype(BF16),
                lfpad_ref[...], key <= qry)
        inv_l = 1.0 / l_ref[...]
        for c in range(n_chunks):
            blk = acc_ref[c * crow:(c + 1) * crow, :] * inv_l[c * crow:(c + 1) * crow]
            out = jnp.zeros((n_q, chunk), F32)
            for hl in range(4):
                out = out + jnp.where(lane_c // FOX_HEAD_DIM == hl, blk[hl * n_q:(hl + 1) * n_q, :], 0.0)
            o_ref[0, :, c * chunk:(c + 1) * chunk] = out


def _fox_sample_attention(page_table, q, cache_kt, cache_vt, cache_lft, kt_new, vt_new, lft_new):
    b, n_q, _ = q.shape
    n_pages = page_table.shape[1]
    page = cache_kt.shape[2]
    rows = FOX_HEADS * n_q
    chunk = 4 * FOX_HEAD_DIM

    def cache_map(bi, p, pt):
        return (pt[bi, jnp.minimum(p, n_pages - 1)], 0, 0)

    per_b = lambda bi, p, pt: (bi, 0, 0)
    grid_spec = pltpu.PrefetchScalarGridSpec(
        num_scalar_prefetch=1,
        grid=(b, n_pages + 1),
        in_specs=[
            pl.BlockSpec((1, n_q, D_MODEL), per_b),
            pl.BlockSpec((1, D_MODEL, page), cache_map),
            pl.BlockSpec((1, D_MODEL, page), cache_map),
            pl.BlockSpec((1, FOX_HEADS, page), cache_map),
            pl.BlockSpec((1, D_MODEL, n_q), per_b),
            pl.BlockSpec((1, D_MODEL, n_q), per_b),
            pl.BlockSpec((1, FOX_HEADS, n_q), per_b),
        ],
        out_specs=pl.BlockSpec((1, n_q, D_MODEL), per_b),
        scratch_shapes=[
            pltpu.VMEM((D_MODEL // chunk, 4 * n_q, chunk), BF16),
            pltpu.VMEM((rows, 1), F32),
            pltpu.VMEM((rows, 1), F32),
            pltpu.VMEM((rows, chunk), F32),
            pltpu.VMEM((FOX_HEADS, 1), F32),
            pltpu.VMEM((D_MODEL, page), F32),
            pltpu.VMEM((D_MODEL, page), F32),
            pltpu.VMEM((FOX_HEADS, page), F32),
        ],
    )
    return pl.pallas_call(
        functools.partial(_fox_sample_kernel, n_pages=n_pages, page=page),
        grid_spec=grid_spec,
        out_shape=jax.ShapeDtypeStruct((b, n_q, D_MODEL), F32),
        compiler_params=_params("parallel", "arbitrary"),
        name="fox_sample_attention",
    )(page_table, q, cache_kt, cache_vt, cache_lft, kt_new, vt_new, lft_new)


def _outproj_router_kernel(x_ref, o_ref, w_ref, g_ref, wr_ref, br_ref, xn_ref, h_ref, gate_ref):
    x_new = x_ref[...] + _dot(o_ref[...].astype(BF16), w_ref[...])
    xn_ref[...] = x_new
    h = _rmsnorm(x_new, g_ref[...])
    h_ref[...] = h.astype(BF16)
    logits = _dot(h, wr_ref[...], HIGHEST) + br_ref[...]
    lane = lax.broadcasted_iota(jnp.int32, logits.shape, 1)
    big = jnp.int32(LANES)
    is_group = (lane >= MOE_EXPERTS) & (lane < MOE_EXPERTS + MOE_GROUPS)
    gl = jnp.where(is_group, logits, NEG_INF)
    g_max = jnp.max(gl, axis=1, keepdims=True)
    g_sel = jnp.min(jnp.where(gl == g_max, lane - MOE_EXPERTS, big), axis=1, keepdims=True)
    p_g = 1.0 / jnp.sum(jnp.where(is_group, jnp.exp(gl - g_max), 0.0), axis=1, keepdims=True)
    in_group = (lane < MOE_EXPERTS) & (lane // MOE_EXPERTS_PER_GROUP == g_sel)
    el = jnp.where(in_group, logits, NEG_INF)
    v1 = jnp.max(el, axis=1, keepdims=True)
    i1 = jnp.min(jnp.where(el == v1, lane, big), axis=1, keepdims=True)
    el2 = jnp.where(lane == i1, NEG_INF, el)
    v2 = jnp.max(el2, axis=1, keepdims=True)
    i2 = jnp.min(jnp.where(el2 == v2, lane, big), axis=1, keepdims=True)
    e2 = jnp.exp(v2 - v1)
    w1 = p_g / (1.0 + e2)
    w2 = p_g * e2 / (1.0 + e2)
    gate_ref[...] = jnp.where(lane == i1, w1, 0.0) + jnp.where(lane == i2, w2, 0.0)


def _outproj_router(x, o, w_out, g_ffn, w_r, b_r, tm):
    n = x.shape[0]
    row = lambda i: (i, 0)
    const = lambda i: (0, 0)
    return pl.pallas_call(
        _outproj_router_kernel,
        grid=(n // tm,),
        in_specs=[
            pl.BlockSpec((tm, D_MODEL), row),
            pl.BlockSpec((tm, D_MODEL), row),
            pl.BlockSpec((D_MODEL, D_MODEL), const),
            pl.BlockSpec((1, D_MODEL), const),
            pl.BlockSpec((D_MODEL, LANES), const),
            pl.BlockSpec((1, LANES), const),
        ],
        out_specs=[
            pl.BlockSpec((tm, D_MODEL), row),
            pl.BlockSpec((tm, D_MODEL), row),
            pl.BlockSpec((tm, LANES), row),
        ],
        out_shape=[
            jax.ShapeDtypeStruct((n, D_MODEL), F32),
            jax.ShapeDtypeStruct((n, D_MODEL), BF16),
            jax.ShapeDtypeStruct((n, LANES), F32),
        ],
        compiler_params=_params("parallel"),
        name="outproj_router",
    )(x, o, w_out, g_ffn, w_r, b_r)


def _moe_kernel(x_ref, h_ref, gate_ref, wg_ref, wu_ref, wd_ref, gf_ref, o_ref, acc_ref, *, final_norm):
    e = pl.program_id(1)

    @pl.when(e == 0)
    def _():
        acc_ref[...] = jnp.zeros(acc_ref.shape, F32)

    gates = gate_ref[...]
    lane = lax.broadcasted_iota(jnp.int32, gates.shape, 1)
    gcol = jnp.sum(jnp.where(lane == e, gates, 0.0), axis=1, keepdims=True)
    h = h_ref[...]
    hg = _dot(h, wg_ref[0])
    hu = _dot(h, wu_ref[0])
    act = hg * _sigmoid(hg) * hu * gcol
    acc_ref[...] += _dot(act.astype(BF16), wd_ref[0])

    @pl.when(e == pl.num_programs(1) - 1)
    def _():
        y = x_ref[...] + acc_ref[...]
        if final_norm:
            y = _rmsnorm(y, gf_ref[...])
        o_ref[...] = y


def _moe(x, h, gates, w_gate, w_up, w_down, g_final, tm, final_norm):
    n = x.shape[0]
    row = lambda i, e: (i, 0)
    return pl.pallas_call(
        functools.partial(_moe_kernel, final_norm=final_norm),
        grid=(n // tm, MOE_EXPERTS),
        in_specs=[
            pl.BlockSpec((tm, D_MODEL), row),
            pl.BlockSpec((tm, D_MODEL), row),
            pl.BlockSpec((tm, LANES), row),
            pl.BlockSpec((1, D_MODEL, MOE_HIDDEN), lambda i, e: (e, 0, 0)),
            pl.BlockSpec((1, D_MODEL, MOE_HIDDEN), lambda i, e: (e, 0, 0)),
            pl.BlockSpec((1, MOE_HIDDEN, D_MODEL), lambda i, e: (e, 0, 0)),
            pl.BlockSpec((1, D_MODEL), lambda i, e: (0, 0)),
        ],
        out_specs=pl.BlockSpec((tm, D_MODEL), row),
        out_shape=jax.ShapeDtypeStruct((n, D_MODEL), F32),
        scratch_shapes=[pltpu.VMEM((tm, D_MODEL), F32)],
        compiler_params=_params("parallel", "arbitrary"),
        name="moe_experts",
    )(x, h, gates, w_gate, w_up, w_down, g_final)


def _mlstm_kernel(q_ref, kt_ref, v_ref, o_ref, gc_ref, gr_ref, hn_ref, c0_ref, n0_ref, m0_ref,
                  h_ref, c_out_ref, n_out_ref, m_out_ref, c_s, n_s, m_s, *, chunk, pps):
    ci = pl.program_id(2)

    @pl.when(ci == 0)
    def _():
        c_s[...] = c0_ref[0]
        n_s[...] = n0_ref[0]
        m_s[...] = m0_ref[0]

    lane = lax.broadcasted_iota(jnp.int32, (1, LANES), 1)
    srow = lax.broadcasted_iota(jnp.int32, (LANES, 1), 0)
    row = lax.broadcasted_iota(jnp.int32, (chunk, chunk), 0)
    col = lax.broadcasted_iota(jnp.int32, (chunk, chunk), 1)
    causal = col <= row
    lower = causal.astype(F32)
    upper = (row <= col).astype(F32)
    for pp in range(pps):
        q2 = q_ref[0, :, pp * LANES:(pp + 1) * LANES]
        kt2 = kt_ref[0, pp * LANES:(pp + 1) * LANES, :]
        gc = gc_ref[0, pp]
        gr = gr_ref[0, pp]
        c_prev = c_s[pp]
        n_prev = n_s[pp]
        m_prev_row = m_s[pp]
        b_cols = _dot(lower, gc, HIGHEST)
        b_rows = _dot(gr, upper, HIGHEST)
        c_prev_b = c_prev.astype(BF16)
        kt2f = kt2.astype(F32)
        c_new = jnp.zeros((LANES, MLSTM_DV), F32)
        n_new = jnp.zeros((1, LANES), F32)
        scale_rows = jnp.zeros((LANES, 1), F32)
        scale_lanes = jnp.zeros((1, LANES), F32)
        m_lanes = jnp.zeros((1, LANES), F32)
        for hl in range(2):
            in_head = lane // MLSTM_DK == hl
            in_head_rows = srow // MLSTM_DK == hl
            vsl = slice((2 * pp + hl) * MLSTM_DV, (2 * pp + hl + 1) * MLSTM_DV)
            b_col = b_cols[:, 2 + hl:3 + hl]
            b_row = b_rows[2 + hl:3 + hl, :]
            i_row = gr[hl:hl + 1, :]
            m_prev = m_prev_row[:, hl * MLSTM_DK:hl * MLSTM_DK + 1]
            d = jnp.where(causal, b_col - b_row + i_row, NEG_INF)
            inter = b_col + m_prev
            m_t = jnp.maximum(inter, jnp.max(d, axis=1, keepdims=True))
            w_inter = jnp.exp(inter - m_t)
            qh = jnp.where(in_head, q2, jnp.zeros_like(q2))
            s = jnp.exp(d - m_t) * _dot(qh, kt2)
            vh = v_ref[0, :, vsl]
            num = _dot(s.astype(BF16), vh) + w_inter * _dot(qh, c_prev_b)
            qn = jnp.sum(qh.astype(F32) * n_prev, axis=1, keepdims=True)
            den = jnp.sum(s, axis=1, keepdims=True) + w_inter * qn
            hval = num / jnp.maximum(jnp.abs(den), jnp.exp(-m_t))
            hnorm = hval * lax.rsqrt(jnp.mean(hval * hval, axis=1, keepdims=True) + RMS_EPS)
            hnorm = hnorm * hn_ref[:, vsl]
            og = o_ref[0, :, vsl].astype(F32)
            h_ref[0, :, vsl] = (_sigmoid(og) * hnorm).astype(h_ref.dtype)
            b_last = b_row[:, chunk - 1:chunk]
            decay = b_last - b_row + i_row
            m_new = jnp.maximum(b_last + m_prev, jnp.max(decay, axis=1, keepdims=True))
            w_k = jnp.exp(decay - m_new)
            carry_scale = jnp.exp(b_last + m_prev - m_new)
            kth = jnp.where(in_head_rows, kt2f, 0.0)
            c_new = c_new + _dot((kth * w_k).astype(BF16), vh)
            n_new = n_new + _dot_nt(w_k, kth, HIGHEST)
            scale_rows = jnp.where(in_head_rows, carry_scale, scale_rows)
            scale_lanes = jnp.where(in_head, carry_scale, scale_lanes)
            m_lanes = jnp.where(in_head, m_new, m_lanes)
        c_s[pp] = scale_rows * c_prev + c_new
        n_s[pp] = scale_lanes * n_prev + n_new
        m_s[pp] = m_lanes

    @pl.when(ci == pl.num_programs(2) - 1)
    def _():
        c_out_ref[0] = c_s[...]
        n_out_ref[0] = n_s[...]
        m_out_ref[0] = m_s[...]


def _mlstm(q, kt, v, o, gcol, grow, head_norm, c0, n0, m0, chunk, pps):
    b, t, _ = q.shape
    pairs = MLSTM_HEADS // 2
    state = lambda bi, p, ci: (bi, p, 0, 0)
    seq = lambda bi, p, ci: (bi, ci, p)
    return pl.pallas_call(
        functools.partial(_mlstm_kernel, chunk=chunk, pps=pps),
        grid=(b, pairs // pps, t // chunk),
        in_specs=[
            pl.BlockSpec((1, chunk, pps * LANES), seq),
            pl.BlockSpec((1, pps * LANES, chunk), lambda bi, p, ci: (bi, p, ci)),
            pl.BlockSpec((1, chunk, pps * 2 * MLSTM_DV), seq),
            pl.BlockSpec((1, chunk, pps * 2 * MLSTM_DV), seq),
            pl.BlockSpec((1, pps, chunk, 4), lambda bi, p, ci: (bi, p, ci, 0)),
            pl.BlockSpec((1, pps, 4, chunk), lambda bi, p, ci: (bi, p, 0, ci)),
            pl.BlockSpec((1, pps * 2 * MLSTM_DV), lambda bi, p, ci: (0, p)),
            pl.BlockSpec((1, pps, 2 * MLSTM_DK, MLSTM_DV), state),
            pl.BlockSpec((1, pps, 1, 2 * MLSTM_DK), state),
            pl.BlockSpec((1, pps, 1, 2 * MLSTM_DK), state),
        ],
        out_specs=[
            pl.BlockSpec((1, chunk, pps * 2 * MLSTM_DV), seq),
            pl.BlockSpec((1, pps, 2 * MLSTM_DK, MLSTM_DV), state),
            pl.BlockSpec((1, pps, 1, 2 * MLSTM_DK), state),
            pl.BlockSpec((1, pps, 1, 2 * MLSTM_DK), state),
        ],
        out_shape=[
            jax.ShapeDtypeStruct((b, t, D_MODEL), BF16),
            jax.ShapeDtypeStruct((b, pairs, 2 * MLSTM_DK, MLSTM_DV), F32),
            jax.ShapeDtypeStruct((b, pairs, 1, 2 * MLSTM_DK), F32),
            jax.ShapeDtypeStruct((b, pairs, 1, 2 * MLSTM_DK), F32),
        ],
        scratch_shapes=[
            pltpu.VMEM((pps, 2 * MLSTM_DK, MLSTM_DV), F32),
            pltpu.VMEM((pps, 1, 2 * MLSTM_DK), F32),
            pltpu.VMEM((pps, 1, 2 * MLSTM_DK), F32),
        ],
        compiler_params=_params("parallel", "parallel", "arbitrary"),
        name="mlstm_chunkwise",
    )(q, kt, v, o, gcol, grow, head_norm, c0, n0, m0)


def _pad_lanes(w, width=LANES):
    return jnp.pad(w, ((0, 0), (0, width - w.shape[1])))


def _pad_rows(w, height=LANES):
    return jnp.pad(w, ((0, height - w.shape[0]), (0, 0)))


def _router_weights(w_group, b_group, w_router, b_router):
    w = _pad_lanes(jnp.concatenate([w_router, w_group], axis=1).astype(F32))
    b = _pad_lanes(jnp.concatenate([b_router, b_group])[None, :].astype(F32))
    return w, b


def _mlstm_mixer(x, g, w_in, b_i, b_f, head_norm, c0, n0, m0, tm, chunk, pps, transposed):
    b, t, _ = x.shape
    pairs = MLSTM_HEADS // 2
    v_end = 2 * MLSTM_QK + 2 * D_MODEL
    w_qvo = jnp.concatenate([w_in[:, :MLSTM_QK], w_in[:, 2 * MLSTM_QK:v_end]], axis=1).astype(BF16)
    w_k = w_in[:, MLSTM_QK:2 * MLSTM_QK]
    w_g = _pad_lanes(w_in[:, v_end:]).astype(BF16)
    b_g = _pad_lanes(jnp.concatenate([b_i, b_f])[None, :].astype(F32))
    x2 = x.reshape(b * t, D_MODEL)
    if transposed:
        q, kt, v, o, gates = _mlstm_proj(x2, g, w_qvo, w_k.T.astype(BF16), w_g, b_g, tm, seq_len=t)
    else:
        q, k, v, o, gates = _mlstm_proj(x2, g, w_qvo, w_k.astype(BF16), w_g, b_g, tm)
        kt = jnp.swapaxes(k.reshape(b, t, MLSTM_QK), 1, 2)
    q = q.reshape(b, t, MLSTM_QK)
    v = v.reshape(b, t, D_MODEL)
    o = o.reshape(b, t, D_MODEL)
    gates = gates.reshape(b, t, 2, pairs, 2)
    gcol = gates.transpose(0, 3, 1, 2, 4).reshape(b, pairs, t, 4)
    grow = jnp.swapaxes(gcol, 2, 3)
    c0 = c0.reshape(b, pairs, 2 * MLSTM_DK, MLSTM_DV)
    n0 = n0.reshape(b, pairs, 1, 2 * MLSTM_DK)
    m0 = jnp.repeat(m0.reshape(b, pairs, 1, 2), MLSTM_DK, axis=3)
    h, c_f, n_f, m_f = _mlstm(q, kt, v, o, gcol, grow, head_norm.reshape(1, D_MODEL), c0, n0, m0, chunk, pps)
    c_f = c_f.reshape(b, MLSTM_HEADS, MLSTM_DK, MLSTM_DV)
    n_f = n_f.reshape(b, MLSTM_HEADS, MLSTM_DK)
    m_f = m_f.reshape(b, MLSTM_HEADS, MLSTM_DK)[:, :, 0]
    return h.reshape(b * t, D_MODEL), c_f, n_f, m_f


def kernel(x_prompt, x_sample, cache_k, cache_v, cache_logf, page_table, state_C, state_n, state_m, norm_mix, norm_ffn, norm_final, fox_w_in, fox_b_f, fox_w_out, mlstm_w_in, mlstm_b_i, mlstm_b_f, mlstm_head_norm, mlstm_w_out, moe_w_group, moe_b_group, moe_w_router, moe_b_router, moe_w_gate, moe_w_up, moe_w_down):
    bp, tp, _ = x_prompt.shape
    bs, ts, _ = x_sample.shape
    n_p, n_s = bp * tp, bs * ts
    tm_p, tm_s = min(512, tp), min(512, n_s)
    xp = x_prompt.reshape(n_p, D_MODEL)
    xs = x_sample.reshape(n_s, D_MODEL)
    g_final = norm_final[None, :]

    def ffn(x, o, w_out, layer, tm, final_norm):
        w_r, b_r = _router_weights(moe_w_group[layer], moe_b_group[layer], moe_w_router[layer], moe_b_router[layer])
        x_new, h, gates = _outproj_router(x, o, w_out.astype(BF16), norm_ffn[layer][None, :], w_r, b_r, tm)
        return _moe(x_new, h, gates, moe_w_gate[layer].astype(BF16), moe_w_up[layer].astype(BF16),
                    moe_w_down[layer].astype(BF16), g_final, tm, final_norm)

    g0 = norm_mix[0][None, :]
    w_in = fox_w_in[0]
    wq, wk, wv, wf = (w_in[:, :D_MODEL], w_in[:, D_MODEL:2 * D_MODEL], w_in[:, 2 * D_MODEL:3 * D_MODEL],
                      w_in[:, 3 * D_MODEL:])
    b_f = fox_b_f[0].astype(F32)
    wq_b = wq.astype(BF16)
    qp, kpt, vpt, kptb, vptb, lfpt = _fox_proj(
        xp, g0, wq_b, wk.T.astype(BF16), wv.T.astype(BF16), _pad_rows(wf.T).astype(BF16),
        _pad_rows(b_f[:, None]), tm_p, BF16, seq_len=tp)
    qs, ks, vs, _, _, lfs = _fox_proj(
        xs, g0, wq_b, wk.astype(BF16), wv.astype(BF16), _pad_lanes(wf).astype(BF16),
        _pad_lanes(b_f[None, :]), tm_s, F32)

    c_t = _cumsum_lanes(lfpt.reshape(bp * FOX_HEADS, tp)).reshape(bp, FOX_HEADS // 2, 2, tp)
    op = _fox_prompt_attention(qp.reshape(bp, tp, D_MODEL), kptb, vptb, c_t, min(512, tp))
    n_pool, page = cache_k.shape[1], cache_k.shape[2]
    feat_major = lambda c: c.transpose(0, 2, 3, 1).reshape(n_pool, D_MODEL, page)
    seq_last = lambda a: jnp.swapaxes(a.reshape(bs, ts, -1), 1, 2)
    os_ = _fox_sample_attention(page_table, qs.reshape(bs, ts, D_MODEL), feat_major(cache_k[0]), feat_major(cache_v[0]),
                                jnp.swapaxes(cache_logf[0], 1, 2), seq_last(ks), seq_last(vs), seq_last(lfs))
    xp = ffn(xp, op.reshape(n_p, D_MODEL), fox_w_out[0], 0, tm_p, False)
    xs = ffn(xs, os_.reshape(n_s, D_MODEL), fox_w_out[0], 0, tm_s, False)

    g1 = norm_mix[1][None, :]
    zc = jnp.zeros((bp, MLSTM_HEADS, MLSTM_DK, MLSTM_DV), F32)
    zn = jnp.zeros((bp, MLSTM_HEADS, MLSTM_DK), F32)
    zm = jnp.zeros((bp, MLSTM_HEADS), F32)
    hp, cpf, npf, mpf = _mlstm_mixer(xp.reshape(bp, tp, D_MODEL), g1, mlstm_w_in[0], mlstm_b_i[0], mlstm_b_f[0],
                                     mlstm_head_norm[0], zc, zn, zm, tm_p, min(256, tp), 1, True)
    hs, csf, nsf, msf = _mlstm_mixer(xs.reshape(bs, ts, D_MODEL), g1, mlstm_w_in[0], mlstm_b_i[0], mlstm_b_f[0],
                                     mlstm_head_norm[0], state_C[0], state_n[0], state_m[0], tm_s, ts,
                                     MLSTM_HEADS // 2, False)
    yp = ffn(xp, hp, mlstm_w_out[0], 1, tm_p, True)
    ys = ffn(xs, hs, mlstm_w_out[0], 1, tm_s, True)

    hd = (FOX_HEADS, FOX_HEAD_DIM)
    time_major = lambda a: a.reshape(bp, FOX_HEADS, -1, tp).transpose(0, 3, 1, 2)
    return (yp.reshape(bp, tp, D_MODEL), ys.reshape(bs, ts, D_MODEL),
            time_major(kpt)[None], time_major(vpt)[None], jnp.swapaxes(lfpt, 1, 2)[None],
            ks.reshape(1, bs, ts, *hd), vs.reshape(1, bs, ts, *hd), lfs.reshape(1, bs, ts, FOX_HEADS),
            cpf[None], npf[None], mpf[None], csf[None], nsf[None], msf[None])
```

```python
import functools

import jax
import jax.numpy as jnp
from jax import lax
from jax.experimental import pallas as pl
from jax.experimental.pallas import tpu as pltpu

D_MODEL = 1024
FOX_HEADS = 16
FOX_HEAD_DIM = 64
MLSTM_HEADS = 8
MLSTM_DK = 64
MLSTM_DV = 128
MLSTM_QK = MLSTM_HEADS * MLSTM_DK
MOE_GROUPS = 4
MOE_EXPERTS_PER_GROUP = 8
MOE_EXPERTS = 32
MOE_HIDDEN = 256
RMS_EPS = 1e-6

LANES = 128
VMEM_LIMIT_BYTES = 56 * 1024 * 1024

F32 = jnp.float32
BF16 = jnp.bfloat16
HIGHEST = lax.Precision.HIGHEST
NEG_INF = float("-inf")


def _params(*sem):
    return pltpu.CompilerParams(dimension_semantics=sem, vmem_limit_bytes=VMEM_LIMIT_BYTES)


def _dot(a, b, precision=None):
    return jnp.dot(a, b, preferred_element_type=F32, precision=precision)


def _dot_nt(a, b, precision=None):
    return lax.dot_general(a, b, (((1,), (1,)), ((), ())), preferred_element_type=F32, precision=precision)


def _rmsnorm(x, g):
    return x * lax.rsqrt(jnp.mean(x * x, axis=-1, keepdims=True) + RMS_EPS) * g


def _log_sigmoid(z):
    return jnp.minimum(z, 0.0) - jnp.log1p(jnp.exp(-jnp.abs(z)))


def _sigmoid(z):
    return 1.0 / (1.0 + jnp.exp(-z))


def _upper_tri(n):
    r = lax.broadcasted_iota(jnp.int32, (n, n), 0)
    c = lax.broadcasted_iota(jnp.int32, (n, n), 1)
    return (r <= c).astype(F32)


def _fox_proj_kernel(x_ref, g_ref, wq_ref, wk_ref, wv_ref, wf_ref, bf_ref,
                     q_ref, k_ref, v_ref, kb_ref, vb_ref, lf_ref, *, transposed):
    hb = _rmsnorm(x_ref[...], g_ref[...]).astype(BF16)
    q = _dot(hb, wq_ref[...])
    q_ref[...] = (q * (FOX_HEAD_DIM ** -0.5)).astype(q_ref.dtype)
    if transposed:
        k = _dot_nt(wk_ref[...], hb)
        v = _dot_nt(wv_ref[...], hb)
        z = _dot_nt(wf_ref[...], hb) + bf_ref[...]
        k_ref[0] = k
        kb_ref[0] = k.astype(BF16)
        v_ref[0] = v
        vb_ref[0] = v.astype(BF16)
        lf_ref[0] = _log_sigmoid(z)[:FOX_HEADS, :]
    else:
        k = _dot(hb, wk_ref[...])
        v = _dot(hb, wv_ref[...])
        z = _dot(hb, wf_ref[...]) + bf_ref[...]
        k_ref[...] = k
        kb_ref[...] = k.astype(BF16)
        v_ref[...] = v
        vb_ref[...] = v.astype(BF16)
        lf_ref[...] = _log_sigmoid(z)[:, :FOX_HEADS]


def _fox_proj(x, g, wq, wk, wv, wf, bf, tm, q_dtype, seq_len=None):
    n = x.shape[0]
    transposed = seq_len is not None
    row = lambda i: (i, 0)
    const = lambda i: (0, 0)
    if transposed:
        nt = seq_len // tm
        b = n // seq_len
        tmap = lambda i: (i // nt, 0, i % nt)
        kv_spec = pl.BlockSpec((1, D_MODEL, tm), tmap)
        lf_spec = pl.BlockSpec((1, FOX_HEADS, tm), tmap)
        kv_shape = (b, D_MODEL, seq_len)
        lf_shape = (b, FOX_HEADS, seq_len)
    else:
        kv_spec = pl.BlockSpec((tm, D_MODEL), row)
        lf_spec = pl.BlockSpec((tm, FOX_HEADS), row)
        kv_shape = (n, D_MODEL)
        lf_shape = (n, FOX_HEADS)
    return pl.pallas_call(
        functools.partial(_fox_proj_kernel, transposed=transposed),
        grid=(n // tm,),
        in_specs=[
            pl.BlockSpec((tm, D_MODEL), row),
            pl.BlockSpec((1, D_MODEL), const),
            pl.BlockSpec(wq.shape, const),
            pl.BlockSpec(wk.shape, const),
            pl.BlockSpec(wv.shape, const),
            pl.BlockSpec(wf.shape, const),
            pl.BlockSpec(bf.shape, const),
        ],
        out_specs=[pl.BlockSpec((tm, D_MODEL), row), kv_spec, kv_spec, kv_spec, kv_spec, lf_spec],
        out_shape=[
            jax.ShapeDtypeStruct((n, D_MODEL), q_dtype),
            jax.ShapeDtypeStruct(kv_shape, F32),
            jax.ShapeDtypeStruct(kv_shape, F32),
            jax.ShapeDtypeStruct(kv_shape, BF16),
            jax.ShapeDtypeStruct(kv_shape, BF16),
            jax.ShapeDtypeStruct(lf_shape, F32),
        ],
        compiler_params=_params("parallel"),
        name="fox_proj",
    )(x, g, wq, wk, wv, wf, bf)


def _mlstm_proj_kernel(x_ref, y3_ref, g_ref, w_ref, wk_ref, wg_ref, bg_ref,
                       x1_ref, q_ref, k_ref, v_ref, o_ref, gate_ref, *, transposed):
    x1 = x_ref[...] + _from_slabs(y3_ref)
    x1_ref[...] = x1
    hb = _rmsnorm(x1, g_ref[...]).astype(BF16)
    q = _dot(hb, w_ref[:, 0:MLSTM_QK])
    q_ref[...] = (q * (MLSTM_DK ** -0.5)).astype(BF16)
    if transposed:
        k_ref[0] = _dot_nt(wk_ref[...], hb).astype(BF16)
    else:
        k_ref[...] = _dot(hb, wk_ref[...]).astype(BF16)
    v_ref[...] = _dot(hb, w_ref[:, MLSTM_QK:MLSTM_QK + D_MODEL]).astype(BF16)
    o_ref[...] = _dot(hb, w_ref[:, MLSTM_QK + D_MODEL:MLSTM_QK + 2 * D_MODEL]).astype(BF16)
    z = _dot(hb, wg_ref[...]) + bg_ref[...]
    lane = lax.broadcasted_iota(jnp.int32, z.shape, 1)
    gates = jnp.where(lane < MLSTM_HEADS, z, _log_sigmoid(z))
    gate_ref[...] = gates[:, :2 * MLSTM_HEADS]


def _mlstm_proj(x, y3, g, w_qvo, wk, w_g, b_g, tm, seq_len=None):
    n = x.shape[0]
    transposed = seq_len is not None
    row = lambda i: (i, 0)
    const = lambda i: (0, 0)
    if transposed:
        nt = seq_len // tm
        k_spec = pl.BlockSpec((1, MLSTM_QK, tm), lambda i: (i // nt, 0, i % nt))
        k_shape = (n // seq_len, MLSTM_QK, seq_len)
    else:
        k_spec = pl.BlockSpec((tm, MLSTM_QK), row)
        k_shape = (n, MLSTM_QK)
    return pl.pallas_call(
        functools.partial(_mlstm_proj_kernel, transposed=transposed),
        grid=(n // tm,),
        in_specs=[
            pl.BlockSpec((tm, D_MODEL), row),
            pl.BlockSpec((tm, D_MODEL // LANES, LANES), lambda i: (i, 0, 0)),
            pl.BlockSpec((1, D_MODEL), const),
            pl.BlockSpec(w_qvo.shape, const),
            pl.BlockSpec(wk.shape, const),
            pl.BlockSpec((D_MODEL, LANES), const),
            pl.BlockSpec((1, LANES), const),
        ],
        out_specs=[
            pl.BlockSpec((tm, D_MODEL), row),
            pl.BlockSpec((tm, MLSTM_QK), row),
            k_spec,
            pl.BlockSpec((tm, D_MODEL), row),
            pl.BlockSpec((tm, D_MODEL), row),
            pl.BlockSpec((tm, 2 * MLSTM_HEADS), row),
        ],
        out_shape=[
            jax.ShapeDtypeStruct((n, D_MODEL), F32),
            jax.ShapeDtypeStruct((n, MLSTM_QK), BF16),
            jax.ShapeDtypeStruct(k_shape, BF16),
            jax.ShapeDtypeStruct((n, D_MODEL), BF16),
            jax.ShapeDtypeStruct((n, D_MODEL), BF16),
            jax.ShapeDtypeStruct((n, 2 * MLSTM_HEADS), F32),
        ],
        compiler_params=_params("parallel"),
        name="mlstm_proj",
    )(x, y3, g, w_qvo, wk, w_g, b_g)


def _cumsum_lanes_kernel(x_ref, o_ref):
    rows, t = x_ref.shape
    tri = _upper_tri(LANES)
    carry = jnp.zeros((rows, 1), F32)
    for j in range(t // LANES):
        blk = _dot(x_ref[:, j * LANES:(j + 1) * LANES], tri, HIGHEST) + carry
        o_ref[:, j * LANES:(j + 1) * LANES] = blk
        carry = blk[:, LANES - 1:LANES]


def _cumsum_lanes(x):
    return pl.pallas_call(
        _cumsum_lanes_kernel,
        out_shape=jax.ShapeDtypeStruct(x.shape, F32),
        compiler_params=_params(),
        name="cumsum_lanes",
    )(x)


def _fox_prompt_kernel(q_ref, kt_ref, vt_ref, c_ref, o_ref, *, blk):
    qi = pl.program_id(2)
    q2 = q_ref[0]
    lane = lax.broadcasted_iota(jnp.int32, (1, LANES), 1)
    head0 = lane < FOX_HEAD_DIM
    zero = jnp.zeros_like(q2)
    qh = (jnp.where(head0, q2, zero), jnp.where(head0, zero, q2))
    row = lax.broadcasted_iota(jnp.int32, (blk, blk), 0)
    col = lax.broadcasted_iota(jnp.int32, (blk, blk), 1)
    causal = col <= row

    def block(j, carry, masked):
        start = pl.multiple_of(j * blk, blk)
        kt = kt_ref[0, :, pl.ds(start, blk)]
        vt = vt_ref[0, :, pl.ds(start, blk)]
        out = []
        for h in range(2):
            m, l, acc = carry[h]
            cs = c_ref[0, 0, h:h + 1, pl.ds(start, blk)]
            s = _dot(qh[h], kt) - cs
            if masked:
                s = jnp.where(causal, s, NEG_INF)
            m_new = jnp.maximum(m, jnp.max(s, axis=1, keepdims=True))
            alpha = jnp.exp(m - m_new)
            p = jnp.exp(s - m_new)
            l = alpha * l + jnp.sum(p, axis=1, keepdims=True)
            acc = alpha * acc + _dot_nt(p.astype(BF16), vt)
            out.append((m_new, l, acc))
        return tuple(out)

    one = (jnp.full((blk, 1), NEG_INF, F32), jnp.zeros((blk, 1), F32), jnp.zeros((blk, LANES), F32))
    carry = lax.fori_loop(0, qi, functools.partial(block, masked=False), (one, one))
    (_, l0, acc0), (_, l1, acc1) = block(qi, carry, True)
    o_ref[0] = jnp.where(head0, acc0 / l0, acc1 / l1).astype(o_ref.dtype)


def _fox_prompt_attention(q, kt, vt, c_t, blk):
    b, t, _ = q.shape
    pairs = FOX_HEADS // 2
    return pl.pallas_call(
        functools.partial(_fox_prompt_kernel, blk=blk),
        grid=(b, pairs, t // blk),
        in_specs=[
            pl.BlockSpec((1, blk, LANES), lambda bi, p, qi: (bi, qi, p)),
            pl.BlockSpec((1, LANES, t), lambda bi, p, qi: (bi, p, 0)),
            pl.BlockSpec((1, LANES, t), lambda bi, p, qi: (bi, p, 0)),
            pl.BlockSpec((1, 1, 2, t), lambda bi, p, qi: (bi, p, 0, 0)),
        ],
        out_specs=pl.BlockSpec((1, blk, LANES), lambda bi, p, qi: (bi, qi, p)),
        out_shape=jax.ShapeDtypeStruct((b, t, D_MODEL), BF16),
        compiler_params=_params("parallel", "parallel", "arbitrary"),
        name="fox_prompt_attention",
    )(q, kt, vt, c_t)


def _fox_sample_kernel(pt_ref, q_ref, *refs, n_pages, page, pps):
    kc_refs, vc_refs, lfc_refs = refs[0:pps], refs[pps:2 * pps], refs[2 * pps:3 * pps]
    (kn_ref, vn_ref, lfn_ref, o_ref,
     qbd_ref, m_ref, l_ref, acc_ref, carry_ref, kpad_ref, vpad_ref, lfpad_ref) = refs[3 * pps:]
    n_steps = n_pages // pps
    p = pl.program_id(1)
    n_q = q_ref.shape[1]
    chunk = 4 * FOX_HEAD_DIM
    n_chunks = D_MODEL // chunk
    rows = FOX_HEADS * n_q
    crow = 4 * n_q
    lane_c = lax.broadcasted_iota(jnp.int32, (1, chunk), 1)

    @pl.when(p == 0)
    def _init():
        q = q_ref[0]
        for c in range(n_chunks):
            qc = q[:, c * chunk:(c + 1) * chunk]
            pieces = [jnp.where(lane_c // FOX_HEAD_DIM == hl, qc, 0.0) for hl in range(4)]
            qbd_ref[c] = jnp.concatenate(pieces, axis=0).astype(BF16)
        m_ref[...] = jnp.full(m_ref.shape, NEG_INF, F32)
        l_ref[...] = jnp.zeros(l_ref.shape, F32)
        acc_ref[...] = jnp.zeros(acc_ref.shape, F32)
        carry_ref[...] = jnp.zeros(carry_ref.shape, F32)

    def process(get_kt, get_vt, lf_t, mask):
        cum = _dot(lf_t, _upper_tri(page), HIGHEST) + carry_ref[...]
        carry_ref[...] = cum[:, page - 1:page]
        bias = jnp.concatenate([jnp.broadcast_to(cum[h:h + 1, :], (n_q, page)) for h in range(FOX_HEADS)], axis=0)
        s = jnp.concatenate([_dot(qbd_ref[c], get_kt(c)) for c in range(n_chunks)], axis=0) - bias
        if mask is not None:
            s = jnp.where(mask, s, NEG_INF)
        m_old = m_ref[...]
        m_new = jnp.maximum(m_old, jnp.max(s, axis=1, keepdims=True))
        alpha = jnp.exp(m_old - m_new)
        pr = jnp.exp(s - m_new)
        l_ref[...] = alpha * l_ref[...] + jnp.sum(pr, axis=1, keepdims=True)
        m_ref[...] = m_new
        pb = pr.astype(BF16)
        for c in range(n_chunks):
            sl = slice(c * crow, (c + 1) * crow)
            acc_ref[sl, :] = alpha[sl] * acc_ref[sl, :] + _dot_nt(pb[sl], get_vt(c))

    @pl.when(p < n_steps)
    def _past():
        for kc_ref, vc_ref, lfc_ref in zip(kc_refs, vc_refs, lfc_refs):
            process(lambda c, r=kc_ref: r[0, c * chunk:(c + 1) * chunk, :].astype(BF16),
                    lambda c, r=vc_ref: r[0, c * chunk:(c + 1) * chunk, :].astype(BF16),
                    lfc_ref[0], None)

    @pl.when(p == n_steps)
    def _new():
        kpad_ref[...] = jnp.zeros(kpad_ref.shape, F32)
        vpad_ref[...] = jnp.zeros(vpad_ref.shape, F32)
        lfpad_ref[...] = jnp.zeros(lfpad_ref.shape, F32)
        kpad_ref[:, 0:n_q] = kn_ref[0]
        vpad_ref[:, 0:n_q] = vn_ref[0]
        lfpad_ref[:, 0:n_q] = lfn_ref[0]
        key = lax.broadcasted_iota(jnp.int32, (rows, page), 1)
        qry = lax.broadcasted_iota(jnp.int32, (rows, page), 0) % n_q
        process(lambda c: kpad_ref[c * chunk:(c + 1) * chunk, :].astype(BF16),
                lambda c: vpad_ref[c * chunk:(c + 1) * chunk, :].astype(BF16),
                lfpad_ref[...], key <= qry)
        inv_l = 1.0 / l_ref[...]
        for c in range(n_chunks):
            blk = acc_ref[c * crow:(c + 1) * crow, :] * inv_l[c * crow:(c + 1) * crow]
            out = jnp.zeros((n_q, chunk), F32)
            for hl in range(4):
                out = out + jnp.where(lane_c // FOX_HEAD_DIM == hl, blk[hl * n_q:(hl + 1) * n_q, :], 0.0)
            o_ref[0, :, c * chunk:(c + 1) * chunk] = out


def _fox_sample_attention(page_table, q, cache_kt, cache_vt, cache_lft, kt_new, vt_new, lft_new):
    b, n_q, _ = q.shape
    n_pages = page_table.shape[1]
    page = cache_kt.shape[2]
    rows = FOX_HEADS * n_q
    chunk = 4 * FOX_HEAD_DIM
    pps = max(d for d in (1, 2, 4) if n_pages % d == 0)

    def cache_map(j):
        return lambda bi, p, pt: (pt[bi, jnp.minimum(p * pps + j, n_pages - 1)], 0, 0)

    per_b = lambda bi, p, pt: (bi, 0, 0)
    grid_spec = pltpu.PrefetchScalarGridSpec(
        num_scalar_prefetch=1,
        grid=(b, n_pages // pps + 1),
        in_specs=[
            pl.BlockSpec((1, n_q, D_MODEL), per_b),
            *[pl.BlockSpec((1, D_MODEL, page), cache_map(j)) for j in range(pps)],
            *[pl.BlockSpec((1, D_MODEL, page), cache_map(j)) for j in range(pps)],
            *[pl.BlockSpec((1, FOX_HEADS, page), cache_map(j)) for j in range(pps)],
            pl.BlockSpec((1, D_MODEL, n_q), per_b),
            pl.BlockSpec((1, D_MODEL, n_q), per_b),
            pl.BlockSpec((1, FOX_HEADS, n_q), per_b),
        ],
        out_specs=pl.BlockSpec((1, n_q, D_MODEL), per_b),
        scratch_shapes=[
            pltpu.VMEM((D_MODEL // chunk, 4 * n_q, chunk), BF16),
            pltpu.VMEM((rows, 1), F32),
            pltpu.VMEM((rows, 1), F32),
            pltpu.VMEM((rows, chunk), F32),
            pltpu.VMEM((FOX_HEADS, 1), F32),
            pltpu.VMEM((D_MODEL, page), F32),
            pltpu.VMEM((D_MODEL, page), F32),
            pltpu.VMEM((FOX_HEADS, page), F32),
        ],
    )
    return pl.pallas_call(
        functools.partial(_fox_sample_kernel, n_pages=n_pages, page=page, pps=pps),
        grid_spec=grid_spec,
        out_shape=jax.ShapeDtypeStruct((b, n_q, D_MODEL), F32),
        compiler_params=_params("parallel", "arbitrary"),
        name="fox_sample_attention",
    )(page_table, q, *[cache_kt] * pps, *[cache_vt] * pps, *[cache_lft] * pps, kt_new, vt_new, lft_new)


def _to_slabs(ref3, x):
    for j in range(D_MODEL // LANES):
        ref3[:, j, :] = x[:, j * LANES:(j + 1) * LANES]


def _from_slabs(ref3):
    return jnp.concatenate([ref3[:, j, :] for j in range(D_MODEL // LANES)], axis=1)


def _outproj_router_kernel(x_ref, o_ref, w_ref, g_ref, wr_ref, br_ref, xn_ref, h3_ref, route_ref):
    x_new = x_ref[...] + _dot(o_ref[...].astype(BF16), w_ref[...])
    xn_ref[...] = x_new
    h = _rmsnorm(x_new, g_ref[...])
    _to_slabs(h3_ref, h)
    logits = _dot(h, wr_ref[...], HIGHEST) + br_ref[...]
    lane = lax.broadcasted_iota(jnp.int32, logits.shape, 1)
    big = jnp.int32(LANES)
    is_group = (lane >= MOE_EXPERTS) & (lane < MOE_EXPERTS + MOE_GROUPS)
    gl = jnp.where(is_group, logits, NEG_INF)
    g_max = jnp.max(gl, axis=1, keepdims=True)
    g_sel = jnp.min(jnp.where(gl == g_max, lane - MOE_EXPERTS, big), axis=1, keepdims=True)
    p_g = 1.0 / jnp.sum(jnp.where(is_group, jnp.exp(gl - g_max), 0.0), axis=1, keepdims=True)
    in_group = (lane < MOE_EXPERTS) & (lane // MOE_EXPERTS_PER_GROUP == g_sel)
    el = jnp.where(in_group, logits, NEG_INF)
    v1 = jnp.max(el, axis=1, keepdims=True)
    i1 = jnp.min(jnp.where(el == v1, lane, big), axis=1, keepdims=True)
    el2 = jnp.where(lane == i1, NEG_INF, el)
    v2 = jnp.max(el2, axis=1, keepdims=True)
    i2 = jnp.min(jnp.where(el2 == v2, lane, big), axis=1, keepdims=True)
    e2 = jnp.exp(v2 - v1)
    w1 = p_g / (1.0 + e2)
    w2 = p_g * e2 / (1.0 + e2)
    route_ref[...] = jnp.where(lane == 0, i1.astype(F32),
                               jnp.where(lane == 1, i2.astype(F32),
                                         jnp.where(lane == 2, w1, jnp.where(lane == 3, w2, 0.0))))


def _outproj_router(x, o, w_out, g_ffn, w_r, b_r, tm):
    n = x.shape[0]
    row = lambda i: (i, 0)
    const = lambda i: (0, 0)
    return pl.pallas_call(
        _outproj_router_kernel,
        grid=(n // tm,),
        in_specs=[
            pl.BlockSpec((tm, D_MODEL), row),
            pl.BlockSpec((tm, D_MODEL), row),
            pl.BlockSpec((D_MODEL, D_MODEL), const),
            pl.BlockSpec((1, D_MODEL), const),
            pl.BlockSpec((D_MODEL, LANES), const),
            pl.BlockSpec((1, LANES), const),
        ],
        out_specs=[
            pl.BlockSpec((tm, D_MODEL), row),
            pl.BlockSpec((tm, D_MODEL // LANES, LANES), lambda i: (i, 0, 0)),
            pl.BlockSpec((tm, LANES), row),
        ],
        out_shape=[
            jax.ShapeDtypeStruct((n, D_MODEL), F32),
            jax.ShapeDtypeStruct((n, D_MODEL // LANES, LANES), F32),
            jax.ShapeDtypeStruct((n, LANES), F32),
        ],
        compiler_params=_params("parallel"),
        name="outproj_router",
    )(x, o, w_out, g_ffn, w_r, b_r)


MOE_TILE = 256
MOE_TILE_PITCH = MOE_TILE + 1
MOE_UNROLL = 8


def _moe_kernel(off_ref, tok_ref, wt_ref, h3_ref, wg_ref, wu_ref, wd_ref, y3_ref, xg_ref, yg_ref):
    c = pl.program_id(0)
    e = pl.program_id(1)
    n_col = D_MODEL // LANES
    pitch = MOE_TILE_PITCH

    @pl.when((c == 0) & (e == 0))
    def _():
        xg_ref[...] = jnp.zeros(xg_ref.shape, F32)

    @pl.when(e == 0)
    def _():
        y3_ref[...] = jnp.zeros(y3_ref.shape, F32)

    base = off_ref[c, e]
    n = off_ref[c, e + 1] - base

    def gather_row(i, start):
        xg_ref[pl.ds(i, n_col, stride=pitch), :] = h3_ref[tok_ref[c, start + i]]

    def sub_tile(t, carry):
        start = base + t * MOE_TILE
        cnt = jnp.minimum(MOE_TILE, n - t * MOE_TILE)
        full = cnt // MOE_UNROLL

        def gather_group(g, _):
            for u in range(MOE_UNROLL):
                gather_row(g * MOE_UNROLL + u, start)
            return 0

        def gather_one(i, _):
            gather_row(i, start)
            return 0

        lax.fori_loop(0, full, gather_group, 0)
        lax.fori_loop(full * MOE_UNROLL, cnt, gather_one, 0)
        x = jnp.concatenate([xg_ref[j * pitch:j * pitch + MOE_TILE, :] for j in range(n_col)], axis=1).astype(BF16)
        hg = _dot(x, wg_ref[0].astype(BF16))
        hu = _dot(x, wu_ref[0].astype(BF16))
        y = _dot((hg * _sigmoid(hg) * hu).astype(BF16), wd_ref[0].astype(BF16))
        for j in range(n_col):
            yg_ref[j * pitch:j * pitch + MOE_TILE, :] = y[:, j * LANES:(j + 1) * LANES]

        def scatter_group(g, _):
            rows, vals = [], []
            for u in range(MOE_UNROLL):
                i = g * MOE_UNROLL + u
                r = tok_ref[c, start + i]
                rows.append(r)
                vals.append(y3_ref[r] + wt_ref[c, start + i] * yg_ref[pl.ds(i, n_col, stride=pitch), :])
            for r, v in zip(rows, vals):
                y3_ref[r] = v
            return 0

        def scatter_one(i, _):
            r = tok_ref[c, start + i]
            y3_ref[r] = y3_ref[r] + wt_ref[c, start + i] * yg_ref[pl.ds(i, n_col, stride=pitch), :]
            return 0

        lax.fori_loop(0, full, scatter_group, 0)
        lax.fori_loop(full * MOE_UNROLL, cnt, scatter_one, 0)
        return carry

    lax.fori_loop(0, (n + MOE_TILE - 1) // MOE_TILE, sub_tile, 0)


def _moe(h3, route, w_gate, w_up, w_down, chunk):
    n = h3.shape[0]
    n_chunks = n // chunk
    pairs = 2 * chunk
    expert = route[:, 0:2].astype(jnp.int32).reshape(n_chunks, pairs)
    weight = route[:, 2:4].reshape(n_chunks, pairs)
    order = jnp.argsort(expert, axis=1, stable=True)
    expert_sorted = jnp.take_along_axis(expert, order, axis=1)
    tok = (order // 2).astype(jnp.int32)
    wt = jnp.take_along_axis(weight, order, axis=1)
    bounds = jnp.arange(MOE_EXPERTS + 1, dtype=jnp.int32)
    off = jnp.sum(expert_sorted[:, None, :] < bounds[None, :, None], axis=2).astype(jnp.int32)

    n_col = D_MODEL // LANES
    tok_map = lambda c, e, *_: (c, 0, 0)
    grid_spec = pltpu.PrefetchScalarGridSpec(
        num_scalar_prefetch=3,
        grid=(n_chunks, MOE_EXPERTS),
        in_specs=[
            pl.BlockSpec((chunk, n_col, LANES), tok_map),
            pl.BlockSpec((1, D_MODEL, MOE_HIDDEN), lambda c, e, *_: (e, 0, 0)),
            pl.BlockSpec((1, D_MODEL, MOE_HIDDEN), lambda c, e, *_: (e, 0, 0)),
            pl.BlockSpec((1, MOE_HIDDEN, D_MODEL), lambda c, e, *_: (e, 0, 0)),
        ],
        out_specs=pl.BlockSpec((chunk, n_col, LANES), tok_map),
        scratch_shapes=[
            pltpu.VMEM((n_col * MOE_TILE_PITCH, LANES), F32),
            pltpu.VMEM((n_col * MOE_TILE_PITCH, LANES), F32),
        ],
    )
    return pl.pallas_call(
        _moe_kernel,
        grid_spec=grid_spec,
        out_shape=jax.ShapeDtypeStruct((n, n_col, LANES), F32),
        compiler_params=_params("arbitrary", "arbitrary"),
        name="moe_experts",
    )(off, tok, wt, h3, w_gate, w_up, w_down)


def _residual_norm_kernel(x_ref, y3_ref, g_ref, o_ref):
    o_ref[...] = _rmsnorm(x_ref[...] + _from_slabs(y3_ref), g_ref[...])


def _residual_norm(x, y3, g, tm):
    n = x.shape[0]
    row = lambda i: (i, 0)
    return pl.pallas_call(
        _residual_norm_kernel,
        grid=(n // tm,),
        in_specs=[
            pl.BlockSpec((tm, D_MODEL), row),
            pl.BlockSpec((tm, D_MODEL // LANES, LANES), lambda i: (i, 0, 0)),
            pl.BlockSpec((1, D_MODEL), lambda i: (0, 0)),
        ],
        out_specs=pl.BlockSpec((tm, D_MODEL), row),
        out_shape=jax.ShapeDtypeStruct((n, D_MODEL), F32),
        compiler_params=_params("parallel"),
        name="residual_norm",
    )(x, y3, g)


def _mlstm_kernel(q_ref, kt_ref, v_ref, o_ref, gc_ref, gr_ref, hn_ref, c0_ref, n0_ref, m0_ref,
                  h_ref, c_out_ref, n_out_ref, m_out_ref, c_s, n_s, m_s, *, chunk, pps):
    ci = pl.program_id(2)

    @pl.when(ci == 0)
    def _():
        c_s[...] = c0_ref[0]
        n_s[...] = n0_ref[0]
        m_s[...] = m0_ref[0]

    lane = lax.broadcasted_iota(jnp.int32, (1, LANES), 1)
    srow = lax.broadcasted_iota(jnp.int32, (LANES, 1), 0)
    row = lax.broadcasted_iota(jnp.int32, (chunk, chunk), 0)
    col = lax.broadcasted_iota(jnp.int32, (chunk, chunk), 1)
    causal = col <= row
    lower = causal.astype(F32)
    upper = (row <= col).astype(F32)
    for pp in range(pps):
        q2 = q_ref[0, :, pp * LANES:(pp + 1) * LANES]
        kt2 = kt_ref[0, pp * LANES:(pp + 1) * LANES, :]
        gc = gc_ref[0, pp]
        gr = gr_ref[0, pp]
        c_prev = c_s[pp]
        n_prev = n_s[pp]
        m_prev_row = m_s[pp]
        b_cols = _dot(lower, gc, HIGHEST)
        b_rows = _dot(gr, upper, HIGHEST)
        c_prev_b = c_prev.astype(BF16)
        kt2f = kt2.astype(F32)
        c_new = jnp.zeros((LANES, MLSTM_DV), F32)
        n_new = jnp.zeros((1, LANES), F32)
        scale_rows = jnp.zeros((LANES, 1), F32)
        scale_lanes = jnp.zeros((1, LANES), F32)
        m_lanes = jnp.zeros((1, LANES), F32)
        for hl in range(2):
            in_head = lane // MLSTM_DK == hl
            in_head_rows = srow // MLSTM_DK == hl
            vsl = slice((2 * pp + hl) * MLSTM_DV, (2 * pp + hl + 1) * MLSTM_DV)
            b_col = b_cols[:, 2 + hl:3 + hl]
            b_row = b_rows[2 + hl:3 + hl, :]
            i_row = gr[hl:hl + 1, :]
            m_prev = m_prev_row[:, hl * MLSTM_DK:hl * MLSTM_DK + 1]
            d = jnp.where(causal, b_col - b_row + i_row, NEG_INF)
            inter = b_col + m_prev
            m_t = jnp.maximum(inter, jnp.max(d, axis=1, keepdims=True))
            w_inter = jnp.exp(inter - m_t)
            qh = jnp.where(in_head, q2, jnp.zeros_like(q2))
            s = jnp.exp(d - m_t) * _dot(qh, kt2)
            vh = v_ref[0, :, vsl]
            num = _dot(s.astype(BF16), vh) + w_inter * _dot(qh, c_prev_b)
            qn = jnp.sum(qh.astype(F32) * n_prev, axis=1, keepdims=True)
            den = jnp.sum(s, axis=1, keepdims=True) + w_inter * qn
            hval = num / jnp.maximum(jnp.abs(den), jnp.exp(-m_t))
            hnorm = hval * lax.rsqrt(jnp.mean(hval * hval, axis=1, keepdims=True) + RMS_EPS)
            hnorm = hnorm * hn_ref[:, vsl]
            og = o_ref[0, :, vsl].astype(F32)
            h_ref[0, :, vsl] = (_sigmoid(og) * hnorm).astype(h_ref.dtype)
            b_last = b_row[:, chunk - 1:chunk]
            decay = b_last - b_row + i_row
            m_new = jnp.maximum(b_last + m_prev, jnp.max(decay, axis=1, keepdims=True))
            w_k = jnp.exp(decay - m_new)
            carry_scale = jnp.exp(b_last + m_prev - m_new)
            kth = jnp.where(in_head_rows, kt2f, 0.0)
            c_new = c_new + _dot((kth * w_k).astype(BF16), vh)
            n_new = n_new + _dot_nt(w_k, kth, HIGHEST)
            scale_rows = jnp.where(in_head_rows, carry_scale, scale_rows)
            scale_lanes = jnp.where(in_head, carry_scale, scale_lanes)
            m_lanes = jnp.where(in_head, m_new, m_lanes)
        c_s[pp] = scale_rows * c_prev + c_new
        n_s[pp] = scale_lanes * n_prev + n_new
        m_s[pp] = m_lanes

    @pl.when(ci == pl.num_programs(2) - 1)
    def _():
        c_out_ref[0] = c_s[...]
        n_out_ref[0] = n_s[...]
        m_out_ref[0] = m_s[...]


def _mlstm(q, kt, v, o, gcol, grow, head_norm, c0, n0, m0, chunk, pps):
    b, t, _ = q.shape
    pairs = MLSTM_HEADS // 2
    state = lambda bi, p, ci: (bi, p, 0, 0)
    seq = lambda bi, p, ci: (bi, ci, p)
    return pl.pallas_call(
        functools.partial(_mlstm_kernel, chunk=chunk, pps=pps),
        grid=(b, pairs // pps, t // chunk),
        in_specs=[
            pl.BlockSpec((1, chunk, pps * LANES), seq),
            pl.BlockSpec((1, pps * LANES, chunk), lambda bi, p, ci: (bi, p, ci)),
            pl.BlockSpec((1, chunk, pps * 2 * MLSTM_DV), seq),
            pl.BlockSpec((1, chunk, pps * 2 * MLSTM_DV), seq),
            pl.BlockSpec((1, pps, chunk, 4), lambda bi, p, ci: (bi, p, ci, 0)),
            pl.BlockSpec((1, pps, 4, chunk), lambda bi, p, ci: (bi, p, 0, ci)),
            pl.BlockSpec((1, pps * 2 * MLSTM_DV), lambda bi, p, ci: (0, p)),
            pl.BlockSpec((1, pps, 2 * MLSTM_DK, MLSTM_DV), state),
            pl.BlockSpec((1, pps, 1, 2 * MLSTM_DK), state),
            pl.BlockSpec((1, pps, 1, 2 * MLSTM_DK), state),
        ],
        out_specs=[
            pl.BlockSpec((1, chunk, pps * 2 * MLSTM_DV), seq),
            pl.BlockSpec((1, pps, 2 * MLSTM_DK, MLSTM_DV), state),
            pl.BlockSpec((1, pps, 1, 2 * MLSTM_DK), state),
            pl.BlockSpec((1, pps, 1, 2 * MLSTM_DK), state),
        ],
        out_shape=[
            jax.ShapeDtypeStruct((b, t, D_MODEL), BF16),
            jax.ShapeDtypeStruct((b, pairs, 2 * MLSTM_DK, MLSTM_DV), F32),
            jax.ShapeDtypeStruct((b, pairs, 1, 2 * MLSTM_DK), F32),
            jax.ShapeDtypeStruct((b, pairs, 1, 2 * MLSTM_DK), F32),
        ],
        scratch_shapes=[
            pltpu.VMEM((pps, 2 * MLSTM_DK, MLSTM_DV), F32),
            pltpu.VMEM((pps, 1, 2 * MLSTM_DK), F32),
            pltpu.VMEM((pps, 1, 2 * MLSTM_DK), F32),
        ],
        compiler_params=_params("parallel", "parallel", "arbitrary"),
        name="mlstm_chunkwise",
    )(q, kt, v, o, gcol, grow, head_norm, c0, n0, m0)


def _pad_lanes(w, width=LANES):
    return jnp.pad(w, ((0, 0), (0, width - w.shape[1])))


def _pad_rows(w, height=LANES):
    return jnp.pad(w, ((0, height - w.shape[0]), (0, 0)))


def _router_weights(w_group, b_group, w_router, b_router):
    w = _pad_lanes(jnp.concatenate([w_router, w_group], axis=1).astype(F32))
    b = _pad_lanes(jnp.concatenate([b_router, b_group])[None, :].astype(F32))
    return w, b


def _mlstm_mixer(x2, y3, b, t, g, w_in, b_i, b_f, head_norm, c0, n0, m0, tm, chunk, pps, transposed):
    pairs = MLSTM_HEADS // 2
    v_end = 2 * MLSTM_QK + 2 * D_MODEL
    w_qvo = jnp.concatenate([w_in[:, :MLSTM_QK], w_in[:, 2 * MLSTM_QK:v_end]], axis=1).astype(BF16)
    w_k = w_in[:, MLSTM_QK:2 * MLSTM_QK]
    w_g = _pad_lanes(w_in[:, v_end:]).astype(BF16)
    b_g = _pad_lanes(jnp.concatenate([b_i, b_f])[None, :].astype(F32))
    if transposed:
        x1, q, kt, v, o, gates = _mlstm_proj(x2, y3, g, w_qvo, w_k.T.astype(BF16), w_g, b_g, tm, seq_len=t)
    else:
        x1, q, k, v, o, gates = _mlstm_proj(x2, y3, g, w_qvo, w_k.astype(BF16), w_g, b_g, tm)
        kt = jnp.swapaxes(k.reshape(b, t, MLSTM_QK), 1, 2)
    q = q.reshape(b, t, MLSTM_QK)
    v = v.reshape(b, t, D_MODEL)
    o = o.reshape(b, t, D_MODEL)
    gates = gates.reshape(b, t, 2, pairs, 2)
    gcol = gates.transpose(0, 3, 1, 2, 4).reshape(b, pairs, t, 4)
    grow = jnp.swapaxes(gcol, 2, 3)
    c0 = c0.reshape(b, pairs, 2 * MLSTM_DK, MLSTM_DV)
    n0 = n0.reshape(b, pairs, 1, 2 * MLSTM_DK)
    m0 = jnp.repeat(m0.reshape(b, pairs, 1, 2), MLSTM_DK, axis=3)
    h, c_f, n_f, m_f = _mlstm(q, kt, v, o, gcol, grow, head_norm.reshape(1, D_MODEL), c0, n0, m0, chunk, pps)
    c_f = c_f.reshape(b, MLSTM_HEADS, MLSTM_DK, MLSTM_DV)
    n_f = n_f.reshape(b, MLSTM_HEADS, MLSTM_DK)
    m_f = m_f.reshape(b, MLSTM_HEADS, MLSTM_DK)[:, :, 0]
    return x1, h.reshape(b * t, D_MODEL), c_f, n_f, m_f


def kernel(x_prompt, x_sample, cache_k, cache_v, cache_logf, page_table, state_C, state_n, state_m, norm_mix, norm_ffn, norm_final, fox_w_in, fox_b_f, fox_w_out, mlstm_w_in, mlstm_b_i, mlstm_b_f, mlstm_head_norm, mlstm_w_out, moe_w_group, moe_b_group, moe_w_router, moe_b_router, moe_w_gate, moe_w_up, moe_w_down):
    bp, tp, _ = x_prompt.shape
    bs, ts, _ = x_sample.shape
    n_p, n_s = bp * tp, bs * ts
    tm_p, tm_s = min(512, tp), min(512, n_s)
    xp = x_prompt.reshape(n_p, D_MODEL)
    xs = x_sample.reshape(n_s, D_MODEL)
    g_final = norm_final[None, :]
    chunk_p, chunk_s = min(2048, n_p), min(2048, n_s)

    def ffn(x, o, w_out, layer, tm, chunk):
        w_r, b_r = _router_weights(moe_w_group[layer], moe_b_group[layer], moe_w_router[layer], moe_b_router[layer])
        x_new, h3, route = _outproj_router(x, o, w_out.astype(BF16), norm_ffn[layer][None, :], w_r, b_r, tm)
        return x_new, _moe(h3, route, moe_w_gate[layer], moe_w_up[layer], moe_w_down[layer], chunk)

    g0 = norm_mix[0][None, :]
    w_in = fox_w_in[0]
    wq, wk, wv, wf = (w_in[:, :D_MODEL], w_in[:, D_MODEL:2 * D_MODEL], w_in[:, 2 * D_MODEL:3 * D_MODEL],
                      w_in[:, 3 * D_MODEL:])
    b_f = fox_b_f[0].astype(F32)
    wq_b = wq.astype(BF16)
    qp, kpt, vpt, kptb, vptb, lfpt = _fox_proj(
        xp, g0, wq_b, wk.T.astype(BF16), wv.T.astype(BF16), _pad_rows(wf.T).astype(BF16),
        _pad_rows(b_f[:, None]), tm_p, BF16, seq_len=tp)
    qs, ks, vs, _, _, lfs = _fox_proj(
        xs, g0, wq_b, wk.astype(BF16), wv.astype(BF16), _pad_lanes(wf).astype(BF16),
        _pad_lanes(b_f[None, :]), tm_s, F32)

    c_t = _cumsum_lanes(lfpt.reshape(bp * FOX_HEADS, tp)).reshape(bp, FOX_HEADS // 2, 2, tp)
    op = _fox_prompt_attention(qp.reshape(bp, tp, D_MODEL), kptb, vptb, c_t, min(512, tp))
    n_pool, page = cache_k.shape[1], cache_k.shape[2]
    feat_major = lambda c: c.transpose(0, 2, 3, 1).reshape(n_pool, D_MODEL, page)
    seq_last = lambda a: jnp.swapaxes(a.reshape(bs, ts, -1), 1, 2)
    os_ = _fox_sample_attention(page_table, qs.reshape(bs, ts, D_MODEL), feat_major(cache_k[0]), feat_major(cache_v[0]),
                                jnp.swapaxes(cache_logf[0], 1, 2), seq_last(ks), seq_last(vs), seq_last(lfs))
    xp, yp3 = ffn(xp, op.reshape(n_p, D_MODEL), fox_w_out[0], 0, tm_p, chunk_p)
    xs, ys3 = ffn(xs, os_.reshape(n_s, D_MODEL), fox_w_out[0], 0, tm_s, chunk_s)

    g1 = norm_mix[1][None, :]
    zc = jnp.zeros((bp, MLSTM_HEADS, MLSTM_DK, MLSTM_DV), F32)
    zn = jnp.zeros((bp, MLSTM_HEADS, MLSTM_DK), F32)
    zm = jnp.zeros((bp, MLSTM_HEADS), F32)
    xp, hp, cpf, npf, mpf = _mlstm_mixer(xp, yp3, bp, tp, g1, mlstm_w_in[0], mlstm_b_i[0], mlstm_b_f[0],
                                         mlstm_head_norm[0], zc, zn, zm, tm_p, min(256, tp), 1, True)
    xs, hs, csf, nsf, msf = _mlstm_mixer(xs, ys3, bs, ts, g1, mlstm_w_in[0], mlstm_b_i[0], mlstm_b_f[0],
                                         mlstm_head_norm[0], state_C[0], state_n[0], state_m[0], tm_s, ts,
                                         MLSTM_HEADS // 2, False)
    xp, yp3 = ffn(xp, hp, mlstm_w_out[0], 1, tm_p, chunk_p)
    xs, ys3 = ffn(xs, hs, mlstm_w_out[0], 1, tm_s, chunk_s)
    yp = _residual_norm(xp, yp3, g_final, tm_p)
    ys = _residual_norm(xs, ys3, g_final, tm_s)

    hd = (FOX_HEADS, FOX_HEAD_DIM)
    time_major = lambda a: a.reshape(bp, FOX_HEADS, -1, tp).transpose(0, 3, 1, 2)
    return (yp.reshape(bp, tp, D_MODEL), ys.reshape(bs, ts, D_MODEL),
            time_major(kpt)[None], time_major(vpt)[None], jnp.swapaxes(lfpt, 1, 2)[None],
            ks.reshape(1, bs, ts, *hd), vs.reshape(1, bs, ts, *hd), lfs.reshape(1, bs, ts, FOX_HEADS),
            cpf[None], npf[None], mpf[None], csf[None], nsf[None], msf[None])
```

```python
import functools

import jax
import jax.numpy as jnp
from jax import lax
from jax.experimental import pallas as pl
from jax.experimental.pallas import tpu as pltpu

D_MODEL = 1024
FOX_HEADS = 16
FOX_HEAD_DIM = 64
MLSTM_HEADS = 8
MLSTM_DK = 64
MLSTM_DV = 128
MLSTM_QK = MLSTM_HEADS * MLSTM_DK
MOE_GROUPS = 4
MOE_EXPERTS_PER_GROUP = 8
MOE_EXPERTS = 32
MOE_HIDDEN = 256
RMS_EPS = 1e-6

LANES = 128
VMEM_LIMIT_BYTES = 56 * 1024 * 1024

F32 = jnp.float32
BF16 = jnp.bfloat16
HIGHEST = lax.Precision.HIGHEST
NEG_INF = float("-inf")


def _params(*sem):
    return pltpu.CompilerParams(dimension_semantics=sem, vmem_limit_bytes=VMEM_LIMIT_BYTES)


def _dot(a, b, precision=None):
    return jnp.dot(a, b, preferred_element_type=F32, precision=precision)


def _dot_nt(a, b, precision=None):
    return lax.dot_general(a, b, (((1,), (1,)), ((), ())), preferred_element_type=F32, precision=precision)


def _rmsnorm(x, g):
    return x * lax.rsqrt(jnp.mean(x * x, axis=-1, keepdims=True) + RMS_EPS) * g


def _log_sigmoid(z):
    return jnp.minimum(z, 0.0) - jnp.log1p(jnp.exp(-jnp.abs(z)))


def _sigmoid(z):
    return 1.0 / (1.0 + jnp.exp(-z))


def _upper_tri(n):
    r = lax.broadcasted_iota(jnp.int32, (n, n), 0)
    c = lax.broadcasted_iota(jnp.int32, (n, n), 1)
    return (r <= c).astype(F32)


def _fox_proj_kernel(x_ref, g_ref, wq_ref, wk_ref, wv_ref, wf_ref, bf_ref,
                     q_ref, k_ref, v_ref, kb_ref, vb_ref, lf_ref, *, transposed):
    hb = _rmsnorm(x_ref[...], g_ref[...]).astype(BF16)
    q = _dot(hb, wq_ref[...])
    q_ref[...] = (q * (FOX_HEAD_DIM ** -0.5)).astype(q_ref.dtype)
    if transposed:
        k = _dot_nt(wk_ref[...], hb)
        v = _dot_nt(wv_ref[...], hb)
        z = _dot_nt(wf_ref[...], hb) + bf_ref[...]
        k_ref[0] = k
        kb_ref[0] = k.astype(BF16)
        v_ref[0] = v
        vb_ref[0] = v.astype(BF16)
        lf_ref[0] = _log_sigmoid(z)[:FOX_HEADS, :]
    else:
        k = _dot(hb, wk_ref[...])
        v = _dot(hb, wv_ref[...])
        z = _dot(hb, wf_ref[...]) + bf_ref[...]
        k_ref[...] = k
        kb_ref[...] = k.astype(BF16)
        v_ref[...] = v
        vb_ref[...] = v.astype(BF16)
        lf_ref[...] = _log_sigmoid(z)[:, :FOX_HEADS]


def _fox_proj(x, g, wq, wk, wv, wf, bf, tm, q_dtype, seq_len=None):
    n = x.shape[0]
    transposed = seq_len is not None
    row = lambda i: (i, 0)
    const = lambda i: (0, 0)
    if transposed:
        nt = seq_len // tm
        b = n // seq_len
        tmap = lambda i: (i // nt, 0, i % nt)
        kv_spec = pl.BlockSpec((1, D_MODEL, tm), tmap)
        lf_spec = pl.BlockSpec((1, FOX_HEADS, tm), tmap)
        kv_shape = (b, D_MODEL, seq_len)
        lf_shape = (b, FOX_HEADS, seq_len)
    else:
        kv_spec = pl.BlockSpec((tm, D_MODEL), row)
        lf_spec = pl.BlockSpec((tm, FOX_HEADS), row)
        kv_shape = (n, D_MODEL)
        lf_shape = (n, FOX_HEADS)
    return pl.pallas_call(
        functools.partial(_fox_proj_kernel, transposed=transposed),
        grid=(n // tm,),
        in_specs=[
            pl.BlockSpec((tm, D_MODEL), row),
            pl.BlockSpec((1, D_MODEL), const),
            pl.BlockSpec(wq.shape, const),
            pl.BlockSpec(wk.shape, const),
            pl.BlockSpec(wv.shape, const),
            pl.BlockSpec(wf.shape, const),
            pl.BlockSpec(bf.shape, const),
        ],
        out_specs=[pl.BlockSpec((tm, D_MODEL), row), kv_spec, kv_spec, kv_spec, kv_spec, lf_spec],
        out_shape=[
            jax.ShapeDtypeStruct((n, D_MODEL), q_dtype),
            jax.ShapeDtypeStruct(kv_shape, F32),
            jax.ShapeDtypeStruct(kv_shape, F32),
            jax.ShapeDtypeStruct(kv_shape, BF16),
            jax.ShapeDtypeStruct(kv_shape, BF16),
            jax.ShapeDtypeStruct(lf_shape, F32),
        ],
        compiler_params=_params("parallel"),
        name="fox_proj",
    )(x, g, wq, wk, wv, wf, bf)


def _mlstm_proj_kernel(x_ref, y3_ref, g_ref, w_ref, wk_ref, wg_ref, bg_ref,
                       x1_ref, q_ref, k_ref, v_ref, o_ref, gate_ref, *, transposed):
    x1 = x_ref[...] + _from_slabs(y3_ref)
    x1_ref[...] = x1
    hb = _rmsnorm(x1, g_ref[...]).astype(BF16)
    q = _dot(hb, w_ref[:, 0:MLSTM_QK])
    q_ref[...] = (q * (MLSTM_DK ** -0.5)).astype(BF16)
    if transposed:
        k_ref[0] = _dot_nt(wk_ref[...], hb).astype(BF16)
    else:
        k_ref[...] = _dot(hb, wk_ref[...]).astype(BF16)
    v_ref[...] = _dot(hb, w_ref[:, MLSTM_QK:MLSTM_QK + D_MODEL]).astype(BF16)
    o_ref[...] = _dot(hb, w_ref[:, MLSTM_QK + D_MODEL:MLSTM_QK + 2 * D_MODEL]).astype(BF16)
    z = _dot(hb, wg_ref[...]) + bg_ref[...]
    lane = lax.broadcasted_iota(jnp.int32, z.shape, 1)
    gates = jnp.where(lane < MLSTM_HEADS, z, _log_sigmoid(z))
    gate_ref[...] = gates[:, :2 * MLSTM_HEADS]


def _mlstm_proj(x, y3, g, w_qvo, wk, w_g, b_g, tm, seq_len=None):
    n = x.shape[0]
    transposed = seq_len is not None
    row = lambda i: (i, 0)
    const = lambda i: (0, 0)
    if transposed:
        nt = seq_len // tm
        k_spec = pl.BlockSpec((1, MLSTM_QK, tm), lambda i: (i // nt, 0, i % nt))
        k_shape = (n // seq_len, MLSTM_QK, seq_len)
    else:
        k_spec = pl.BlockSpec((tm, MLSTM_QK), row)
        k_shape = (n, MLSTM_QK)
    return pl.pallas_call(
        functools.partial(_mlstm_proj_kernel, transposed=transposed),
        grid=(n // tm,),
        in_specs=[
            pl.BlockSpec((tm, D_MODEL), row),
            pl.BlockSpec((tm, D_MODEL // LANES, LANES), lambda i: (i, 0, 0)),
            pl.BlockSpec((1, D_MODEL), const),
            pl.BlockSpec(w_qvo.shape, const),
            pl.BlockSpec(wk.shape, const),
            pl.BlockSpec((D_MODEL, LANES), const),
            pl.BlockSpec((1, LANES), const),
        ],
        out_specs=[
            pl.BlockSpec((tm, D_MODEL), row),
            pl.BlockSpec((tm, MLSTM_QK), row),
            k_spec,
            pl.BlockSpec((tm, D_MODEL), row),
            pl.BlockSpec((tm, D_MODEL), row),
            pl.BlockSpec((tm, 2 * MLSTM_HEADS), row),
        ],
        out_shape=[
            jax.ShapeDtypeStruct((n, D_MODEL), F32),
            jax.ShapeDtypeStruct((n, MLSTM_QK), BF16),
            jax.ShapeDtypeStruct(k_shape, BF16),
            jax.ShapeDtypeStruct((n, D_MODEL), BF16),
            jax.ShapeDtypeStruct((n, D_MODEL), BF16),
            jax.ShapeDtypeStruct((n, 2 * MLSTM_HEADS), F32),
        ],
        compiler_params=_params("parallel"),
        name="mlstm_proj",
    )(x, y3, g, w_qvo, wk, w_g, b_g)


def _cumsum_lanes_kernel(x_ref, o_ref):
    rows, t = x_ref.shape
    tri = _upper_tri(LANES)
    carry = jnp.zeros((rows, 1), F32)
    for j in range(t // LANES):
        blk = _dot(x_ref[:, j * LANES:(j + 1) * LANES], tri, HIGHEST) + carry
        o_ref[:, j * LANES:(j + 1) * LANES] = blk
        carry = blk[:, LANES - 1:LANES]


def _cumsum_lanes(x):
    return pl.pallas_call(
        _cumsum_lanes_kernel,
        out_shape=jax.ShapeDtypeStruct(x.shape, F32),
        compiler_params=_params(),
        name="cumsum_lanes",
    )(x)


def _fox_prompt_kernel(q_ref, kt_ref, vt_ref, c_ref, o_ref, *, blk):
    qi = pl.program_id(2)
    q2 = q_ref[0]
    lane = lax.broadcasted_iota(jnp.int32, (1, LANES), 1)
    head0 = lane < FOX_HEAD_DIM
    zero = jnp.zeros_like(q2)
    qh = (jnp.where(head0, q2, zero), jnp.where(head0, zero, q2))
    row = lax.broadcasted_iota(jnp.int32, (blk, blk), 0)
    col = lax.broadcasted_iota(jnp.int32, (blk, blk), 1)
    causal = col <= row

    def block(j, carry, masked):
        start = pl.multiple_of(j * blk, blk)
        kt = kt_ref[0, :, pl.ds(start, blk)]
        vt = vt_ref[0, :, pl.ds(start, blk)]
        out = []
        for h in range(2):
            m, l, acc = carry[h]
            cs = c_ref[0, 0, h:h + 1, pl.ds(start, blk)]
            s = _dot(qh[h], kt) - cs
            if masked:
                s = jnp.where(causal, s, NEG_INF)
            m_new = jnp.maximum(m, jnp.max(s, axis=1, keepdims=True))
            alpha = jnp.exp(m - m_new)
            p = jnp.exp(s - m_new)
            l = alpha * l + jnp.sum(p, axis=1, keepdims=True)
            acc = alpha * acc + _dot_nt(p.astype(BF16), vt)
            out.append((m_new, l, acc))
        return tuple(out)

    one = (jnp.full((blk, 1), NEG_INF, F32), jnp.zeros((blk, 1), F32), jnp.zeros((blk, LANES), F32))
    carry = lax.fori_loop(0, qi, functools.partial(block, masked=False), (one, one))
    (_, l0, acc0), (_, l1, acc1) = block(qi, carry, True)
    o_ref[0] = jnp.where(head0, acc0 / l0, acc1 / l1).astype(o_ref.dtype)


def _fox_prompt_attention(q, kt, vt, c_t, blk):
    b, t, _ = q.shape
    pairs = FOX_HEADS // 2
    return pl.pallas_call(
        functools.partial(_fox_prompt_kernel, blk=blk),
        grid=(b, pairs, t // blk),
        in_specs=[
            pl.BlockSpec((1, blk, LANES), lambda bi, p, qi: (bi, qi, p)),
            pl.BlockSpec((1, LANES, t), lambda bi, p, qi: (bi, p, 0)),
            pl.BlockSpec((1, LANES, t), lambda bi, p, qi: (bi, p, 0)),
            pl.BlockSpec((1, 1, 2, t), lambda bi, p, qi: (bi, p, 0, 0)),
        ],
        out_specs=pl.BlockSpec((1, blk, LANES), lambda bi, p, qi: (bi, qi, p)),
        out_shape=jax.ShapeDtypeStruct((b, t, D_MODEL), BF16),
        compiler_params=_params("parallel", "parallel", "arbitrary"),
        name="fox_prompt_attention",
    )(q, kt, vt, c_t)


def _fox_sample_kernel(pt_ref, q_ref, *refs, n_pages, page, pps):
    kc_refs, vc_refs, lfc_refs = refs[0:pps], refs[pps:2 * pps], refs[2 * pps:3 * pps]
    (kn_ref, vn_ref, lfn_ref, o_ref,
     qbd_ref, m_ref, l_ref, acc_ref, carry_ref, kpad_ref, vpad_ref, lfpad_ref) = refs[3 * pps:]
    n_steps = n_pages // pps
    p = pl.program_id(1)
    n_q = q_ref.shape[1]
    chunk = 4 * FOX_HEAD_DIM
    n_chunks = D_MODEL // chunk
    rows = FOX_HEADS * n_q
    crow = 4 * n_q
    lane_c = lax.broadcasted_iota(jnp.int32, (1, chunk), 1)

    @pl.when(p == 0)
    def _init():
        q = q_ref[0]
        for c in range(n_chunks):
            qc = q[:, c * chunk:(c + 1) * chunk]
            pieces = [jnp.where(lane_c // FOX_HEAD_DIM == hl, qc, 0.0) for hl in range(4)]
            qbd_ref[c] = jnp.concatenate(pieces, axis=0).astype(BF16)
        m_ref[...] = jnp.full(m_ref.shape, NEG_INF, F32)
        l_ref[...] = jnp.zeros(l_ref.shape, F32)
        acc_ref[...] = jnp.zeros(acc_ref.shape, F32)
        carry_ref[...] = jnp.zeros(carry_ref.shape, F32)

    def process(get_kt, get_vt, lf_ts, mask):
        nb = len(lf_ts)
        local = _dot(jnp.concatenate(lf_ts, axis=0), _upper_tri(page), HIGHEST)
        carry = carry_ref[...]
        biases = []
        for j in range(nb):
            cum = local[j * FOX_HEADS:(j + 1) * FOX_HEADS] + carry
            carry = cum[:, page - 1:page]
            biases.append(jnp.concatenate(
                [jnp.broadcast_to(cum[h:h + 1, :], (n_q, page)) for h in range(FOX_HEADS)], axis=0))
        carry_ref[...] = carry
        bias = jnp.concatenate(biases, axis=1)
        s = jnp.concatenate(
            [_dot(qbd_ref[c], jnp.concatenate([get_kt(j, c) for j in range(nb)], axis=1)) for c in range(n_chunks)],
            axis=0) - bias
        if mask is not None:
            s = jnp.where(mask, s, NEG_INF)
        m_old = m_ref[...]
        m_new = jnp.maximum(m_old, jnp.max(s, axis=1, keepdims=True))
        alpha = jnp.exp(m_old - m_new)
        pr = jnp.exp(s - m_new)
        l_ref[...] = alpha * l_ref[...] + jnp.sum(pr, axis=1, keepdims=True)
        m_ref[...] = m_new
        pb = pr.astype(BF16)
        for c in range(n_chunks):
            sl = slice(c * crow, (c + 1) * crow)
            vt = jnp.concatenate([get_vt(j, c) for j in range(nb)], axis=1)
            acc_ref[sl, :] = alpha[sl] * acc_ref[sl, :] + _dot_nt(pb[sl], vt)

    @pl.when(p < n_steps)
    def _past():
        process(lambda j, c: kc_refs[j][0, c * chunk:(c + 1) * chunk, :].astype(BF16),
                lambda j, c: vc_refs[j][0, c * chunk:(c + 1) * chunk, :].astype(BF16),
                [r[0] for r in lfc_refs], None)

    @pl.when(p == n_steps)
    def _new():
        kpad_ref[...] = jnp.zeros(kpad_ref.shape, F32)
        vpad_ref[...] = jnp.zeros(vpad_ref.shape, F32)
        lfpad_ref[...] = jnp.zeros(lfpad_ref.shape, F32)
        kpad_ref[:, 0:n_q] = kn_ref[0]
        vpad_ref[:, 0:n_q] = vn_ref[0]
        lfpad_ref[:, 0:n_q] = lfn_ref[0]
        key = lax.broadcasted_iota(jnp.int32, (rows, page), 1)
        qry = lax.broadcasted_iota(jnp.int32, (rows, page), 0) % n_q
        process(lambda j, c: kpad_ref[c * chunk:(c + 1) * chunk, :].astype(BF16),
                lambda j, c: vpad_ref[c * chunk:(c + 1) * chunk, :].astype(BF16),
                [lfpad_ref[...]], key <= qry)
        inv_l = 1.0 / l_ref[...]
        for c in range(n_chunks):
            blk = acc_ref[c * crow:(c + 1) * crow, :] * inv_l[c * crow:(c + 1) * crow]
            out = jnp.zeros((n_q, chunk), F32)
            for hl in range(4):
                out = out + jnp.where(lane_c // FOX_HEAD_DIM == hl, blk[hl * n_q:(hl + 1) * n_q, :], 0.0)
            o_ref[0, :, c * chunk:(c + 1) * chunk] = out


def _fox_sample_attention(page_table, q, cache_kt, cache_vt, cache_lft, kt_new, vt_new, lft_new):
    b, n_q, _ = q.shape
    n_pages = page_table.shape[1]
    page = cache_kt.shape[2]
    rows = FOX_HEADS * n_q
    chunk = 4 * FOX_HEAD_DIM
    pps = max(d for d in (1, 2, 4) if n_pages % d == 0)

    def cache_map(j):
        return lambda bi, p, pt: (pt[bi, jnp.minimum(p * pps + j, n_pages - 1)], 0, 0)

    per_b = lambda bi, p, pt: (bi, 0, 0)
    grid_spec = pltpu.PrefetchScalarGridSpec(
        num_scalar_prefetch=1,
        grid=(b, n_pages // pps + 1),
        in_specs=[
            pl.BlockSpec((1, n_q, D_MODEL), per_b),
            *[pl.BlockSpec((1, D_MODEL, page), cache_map(j)) for j in range(pps)],
            *[pl.BlockSpec((1, D_MODEL, page), cache_map(j)) for j in range(pps)],
            *[pl.BlockSpec((1, FOX_HEADS, page), cache_map(j)) for j in range(pps)],
            pl.BlockSpec((1, D_MODEL, n_q), per_b),
            pl.BlockSpec((1, D_MODEL, n_q), per_b),
            pl.BlockSpec((1, FOX_HEADS, n_q), per_b),
        ],
        out_specs=pl.BlockSpec((1, n_q, D_MODEL), per_b),
        scratch_shapes=[
            pltpu.VMEM((D_MODEL // chunk, 4 * n_q, chunk), BF16),
            pltpu.VMEM((rows, 1), F32),
            pltpu.VMEM((rows, 1), F32),
            pltpu.VMEM((rows, chunk), F32),
            pltpu.VMEM((FOX_HEADS, 1), F32),
            pltpu.VMEM((D_MODEL, page), F32),
            pltpu.VMEM((D_MODEL, page), F32),
            pltpu.VMEM((FOX_HEADS, page), F32),
        ],
    )
    return pl.pallas_call(
        functools.partial(_fox_sample_kernel, n_pages=n_pages, page=page, pps=pps),
        grid_spec=grid_spec,
        out_shape=jax.ShapeDtypeStruct((b, n_q, D_MODEL), F32),
        compiler_params=_params("parallel", "arbitrary"),
        name="fox_sample_attention",
    )(page_table, q, *[cache_kt] * pps, *[cache_vt] * pps, *[cache_lft] * pps, kt_new, vt_new, lft_new)


def _to_slabs(ref3, x):
    for j in range(D_MODEL // LANES):
        ref3[:, j, :] = x[:, j * LANES:(j + 1) * LANES]


def _from_slabs(ref3):
    return jnp.concatenate([ref3[:, j, :] for j in range(D_MODEL // LANES)], axis=1)


def _outproj_router_kernel(x_ref, o_ref, w_ref, g_ref, wr_ref, br_ref, xn_ref, h3_ref, route_ref):
    x_new = x_ref[...] + _dot(o_ref[...].astype(BF16), w_ref[...])
    xn_ref[...] = x_new
    h = _rmsnorm(x_new, g_ref[...])
    _to_slabs(h3_ref, h)
    w_r = wr_ref[...]
    h_hi, w_hi = h.astype(BF16), w_r.astype(BF16)
    h_lo = (h - h_hi.astype(F32)).astype(BF16)
    w_lo = (w_r - w_hi.astype(F32)).astype(BF16)
    logits = _dot(h_hi, w_hi) + (_dot(h_lo, w_hi) + _dot(h_hi, w_lo)) + br_ref[...]
    lane = lax.broadcasted_iota(jnp.int32, logits.shape, 1)
    big = jnp.int32(LANES)
    is_group = (lane >= MOE_EXPERTS) & (lane < MOE_EXPERTS + MOE_GROUPS)
    gl = jnp.where(is_group, logits, NEG_INF)
    g_max = jnp.max(gl, axis=1, keepdims=True)
    g_sel = jnp.min(jnp.where(gl == g_max, lane - MOE_EXPERTS, big), axis=1, keepdims=True)
    p_g = 1.0 / jnp.sum(jnp.where(is_group, jnp.exp(gl - g_max), 0.0), axis=1, keepdims=True)
    in_group = (lane < MOE_EXPERTS) & (lane // MOE_EXPERTS_PER_GROUP == g_sel)
    el = jnp.where(in_group, logits, NEG_INF)
    v1 = jnp.max(el, axis=1, keepdims=True)
    i1 = jnp.min(jnp.where(el == v1, lane, big), axis=1, keepdims=True)
    el2 = jnp.where(lane == i1, NEG_INF, el)
    v2 = jnp.max(el2, axis=1, keepdims=True)
    i2 = jnp.min(jnp.where(el2 == v2, lane, big), axis=1, keepdims=True)
    e2 = jnp.exp(v2 - v1)
    w1 = p_g / (1.0 + e2)
    w2 = p_g * e2 / (1.0 + e2)
    route_ref[...] = jnp.where(lane == 0, i1.astype(F32),
                               jnp.where(lane == 1, i2.astype(F32),
                                         jnp.where(lane == 2, w1, jnp.where(lane == 3, w2, 0.0))))


def _outproj_router(x, o, w_out, g_ffn, w_r, b_r, tm):
    n = x.shape[0]
    row = lambda i: (i, 0)
    const = lambda i: (0, 0)
    return pl.pallas_call(
        _outproj_router_kernel,
        grid=(n // tm,),
        in_specs=[
            pl.BlockSpec((tm, D_MODEL), row),
            pl.BlockSpec((tm, D_MODEL), row),
            pl.BlockSpec((D_MODEL, D_MODEL), const),
            pl.BlockSpec((1, D_MODEL), const),
            pl.BlockSpec((D_MODEL, LANES), const),
            pl.BlockSpec((1, LANES), const),
        ],
        out_specs=[
            pl.BlockSpec((tm, D_MODEL), row),
            pl.BlockSpec((tm, D_MODEL // LANES, LANES), lambda i: (i, 0, 0)),
            pl.BlockSpec((tm, LANES), row),
        ],
        out_shape=[
            jax.ShapeDtypeStruct((n, D_MODEL), F32),
            jax.ShapeDtypeStruct((n, D_MODEL // LANES, LANES), F32),
            jax.ShapeDtypeStruct((n, LANES), F32),
        ],
        compiler_params=_params("parallel"),
        name="outproj_router",
    )(x, o, w_out, g_ffn, w_r, b_r)


MOE_TILE = 256
MOE_TILE_PITCH = MOE_TILE + 1
MOE_UNROLL = 8


def _moe_kernel(off_ref, tok_ref, wt_ref, h3_ref, wg_ref, wu_ref, wd_ref, y3_ref, xg_ref, yg_ref):
    c = pl.program_id(0)
    e = pl.program_id(1)
    n_col = D_MODEL // LANES
    pitch = MOE_TILE_PITCH

    @pl.when((c == 0) & (e == 0))
    def _():
        xg_ref[...] = jnp.zeros(xg_ref.shape, F32)

    @pl.when(e == 0)
    def _():
        y3_ref[...] = jnp.zeros(y3_ref.shape, F32)

    base = off_ref[c, e]
    n = off_ref[c, e + 1] - base

    def gather_row(i, start):
        xg_ref[pl.ds(i, n_col, stride=pitch), :] = h3_ref[tok_ref[c, start + i]]

    def sub_tile(t, carry):
        start = base + t * MOE_TILE
        cnt = jnp.minimum(MOE_TILE, n - t * MOE_TILE)
        full = cnt // MOE_UNROLL

        def gather_group(g, _):
            for u in range(MOE_UNROLL):
                gather_row(g * MOE_UNROLL + u, start)
            return 0

        def gather_one(i, _):
            gather_row(i, start)
            return 0

        lax.fori_loop(0, full, gather_group, 0)
        lax.fori_loop(full * MOE_UNROLL, cnt, gather_one, 0)
        x = jnp.concatenate([xg_ref[j * pitch:j * pitch + MOE_TILE, :] for j in range(n_col)], axis=1).astype(BF16)
        hg = _dot(x, wg_ref[0, 0].astype(BF16))
        hu = _dot(x, wu_ref[0, 0].astype(BF16))
        y = _dot((hg * _sigmoid(hg) * hu).astype(BF16), wd_ref[0, 0].astype(BF16))
        for j in range(n_col):
            yg_ref[j * pitch:j * pitch + MOE_TILE, :] = y[:, j * LANES:(j + 1) * LANES]

        def scatter_group(g, _):
            rows, vals = [], []
            for u in range(MOE_UNROLL):
                i = g * MOE_UNROLL + u
                r = tok_ref[c, start + i]
                rows.append(r)
                vals.append(y3_ref[r] + wt_ref[c, start + i] * yg_ref[pl.ds(i, n_col, stride=pitch), :])
            for r, v in zip(rows, vals):
                y3_ref[r] = v
            return 0

        def scatter_one(i, _):
            r = tok_ref[c, start + i]
            y3_ref[r] = y3_ref[r] + wt_ref[c, start + i] * yg_ref[pl.ds(i, n_col, stride=pitch), :]
            return 0

        lax.fori_loop(0, full, scatter_group, 0)
        lax.fori_loop(full * MOE_UNROLL, cnt, scatter_one, 0)
        return carry

    lax.fori_loop(0, (n + MOE_TILE - 1) // MOE_TILE, sub_tile, 0)


def _moe(h3, route, w_gate, w_up, w_down, layer, chunk):
    n = h3.shape[0]
    n_chunks = n // chunk
    pairs = 2 * chunk
    expert = route[:, 0:2].astype(jnp.int32).reshape(n_chunks, pairs)
    weight = route[:, 2:4].reshape(n_chunks, pairs)
    order = jnp.argsort(expert, axis=1, stable=True)
    expert_sorted = jnp.take_along_axis(expert, order, axis=1)
    tok = (order // 2).astype(jnp.int32)
    wt = jnp.take_along_axis(weight, order, axis=1)
    bounds = jnp.arange(MOE_EXPERTS + 1, dtype=jnp.int32)
    off = jnp.sum(expert_sorted[:, None, :] < bounds[None, :, None], axis=2).astype(jnp.int32)

    n_col = D_MODEL // LANES
    tok_map = lambda c, e, *_: (c, 0, 0)
    grid_spec = pltpu.PrefetchScalarGridSpec(
        num_scalar_prefetch=3,
        grid=(n_chunks, MOE_EXPERTS),
        in_specs=[
            pl.BlockSpec((chunk, n_col, LANES), tok_map, pipeline_mode=pl.Buffered(1)),
            pl.BlockSpec((1, 1, D_MODEL, MOE_HIDDEN), lambda c, e, *_: (layer, e, 0, 0)),
            pl.BlockSpec((1, 1, D_MODEL, MOE_HIDDEN), lambda c, e, *_: (layer, e, 0, 0)),
            pl.BlockSpec((1, 1, MOE_HIDDEN, D_MODEL), lambda c, e, *_: (layer, e, 0, 0)),
        ],
        out_specs=pl.BlockSpec((chunk, n_col, LANES), tok_map, pipeline_mode=pl.Buffered(1)),
        scratch_shapes=[
            pltpu.VMEM((n_col * MOE_TILE_PITCH, LANES), F32),
            pltpu.VMEM((n_col * MOE_TILE_PITCH, LANES), F32),
        ],
    )
    return pl.pallas_call(
        _moe_kernel,
        grid_spec=grid_spec,
        out_shape=jax.ShapeDtypeStruct((n, n_col, LANES), F32),
        compiler_params=_params("arbitrary", "arbitrary"),
        name="moe_experts",
    )(off, tok, wt, h3, w_gate, w_up, w_down)


def _residual_norm_kernel(x_ref, y3_ref, g_ref, o_ref):
    o_ref[...] = _rmsnorm(x_ref[...] + _from_slabs(y3_ref), g_ref[...])


def _residual_norm(x, y3, g, tm):
    n = x.shape[0]
    row = lambda i: (i, 0)
    return pl.pallas_call(
        _residual_norm_kernel,
        grid=(n // tm,),
        in_specs=[
            pl.BlockSpec((tm, D_MODEL), row),
            pl.BlockSpec((tm, D_MODEL // LANES, LANES), lambda i: (i, 0, 0)),
            pl.BlockSpec((1, D_MODEL), lambda i: (0, 0)),
        ],
        out_specs=pl.BlockSpec((tm, D_MODEL), row),
        out_shape=jax.ShapeDtypeStruct((n, D_MODEL), F32),
        compiler_params=_params("parallel"),
        name="residual_norm",
    )(x, y3, g)


def _mlstm_kernel(q_ref, kt_ref, v_ref, o_ref, gc_ref, gr_ref, hn_ref, c0_ref, n0_ref, m0_ref,
                  h_ref, c_out_ref, n_out_ref, m_out_ref, c_s, n_s, m_s, *, chunk, pps, bb):
    ci = pl.program_id(2)

    @pl.when(ci == 0)
    def _():
        c_s[...] = c0_ref[...]
        n_s[...] = n0_ref[...]
        m_s[...] = m0_ref[...]

    lane = lax.broadcasted_iota(jnp.int32, (1, LANES), 1)
    srow = lax.broadcasted_iota(jnp.int32, (LANES, 1), 0)
    row = lax.broadcasted_iota(jnp.int32, (chunk, chunk), 0)
    col = lax.broadcasted_iota(jnp.int32, (chunk, chunk), 1)
    causal = col <= row
    lower = causal.astype(F32)
    upper = (row <= col).astype(F32)
    for bi, pp in [(bi, pp) for bi in range(bb) for pp in range(pps)]:
        q2 = q_ref[bi, :, pp * LANES:(pp + 1) * LANES]
        kt2 = kt_ref[bi, pp * LANES:(pp + 1) * LANES, :]
        gc = gc_ref[bi, pp]
        gr = gr_ref[bi, pp]
        c_prev = c_s[bi, pp]
        n_prev = n_s[bi, pp]
        m_prev_row = m_s[bi, pp]
        b_cols = _dot(lower, gc, HIGHEST)
        b_rows = _dot(gr, upper, HIGHEST)
        c_prev_b = c_prev.astype(BF16)
        kt2f = kt2.astype(F32)
        c_new = jnp.zeros((LANES, MLSTM_DV), F32)
        n_new = jnp.zeros((1, LANES), F32)
        scale_rows = jnp.zeros((LANES, 1), F32)
        scale_lanes = jnp.zeros((1, LANES), F32)
        m_lanes = jnp.zeros((1, LANES), F32)
        for hl in range(2):
            in_head = lane // MLSTM_DK == hl
            in_head_rows = srow // MLSTM_DK == hl
            vsl = slice((2 * pp + hl) * MLSTM_DV, (2 * pp + hl + 1) * MLSTM_DV)
            b_col = b_cols[:, 2 + hl:3 + hl]
            b_row = b_rows[2 + hl:3 + hl, :]
            i_row = gr[hl:hl + 1, :]
            m_prev = m_prev_row[:, hl * MLSTM_DK:hl * MLSTM_DK + 1]
            d = jnp.where(causal, b_col - b_row + i_row, NEG_INF)
            inter = b_col + m_prev
            m_t = jnp.maximum(inter, jnp.max(d, axis=1, keepdims=True))
            w_inter = jnp.exp(inter - m_t)
            qh = jnp.where(in_head, q2, jnp.zeros_like(q2))
            s = jnp.exp(d - m_t) * _dot(qh, kt2)
            vh = v_ref[bi, :, vsl]
            num = _dot(s.astype(BF16), vh) + w_inter * _dot(qh, c_prev_b)
            qn = jnp.sum(qh.astype(F32) * n_prev, axis=1, keepdims=True)
            den = jnp.sum(s, axis=1, keepdims=True) + w_inter * qn
            hval = num / jnp.maximum(jnp.abs(den), jnp.exp(-m_t))
            hnorm = hval * lax.rsqrt(jnp.mean(hval * hval, axis=1, keepdims=True) + RMS_EPS)
            hnorm = hnorm * hn_ref[:, vsl]
            og = o_ref[bi, :, vsl].astype(F32)
            h_ref[bi, :, vsl] = (_sigmoid(og) * hnorm).astype(h_ref.dtype)
            b_last = b_row[:, chunk - 1:chunk]
            decay = b_last - b_row + i_row
            m_new = jnp.maximum(b_last + m_prev, jnp.max(decay, axis=1, keepdims=True))
            w_k = jnp.exp(decay - m_new)
            carry_scale = jnp.exp(b_last + m_prev - m_new)
            kth = jnp.where(in_head_rows, kt2f, 0.0)
            c_new = c_new + _dot((kth * w_k).astype(BF16), vh)
            n_new = n_new + _dot_nt(w_k, kth, HIGHEST)
            scale_rows = jnp.where(in_head_rows, carry_scale, scale_rows)
            scale_lanes = jnp.where(in_head, carry_scale, scale_lanes)
            m_lanes = jnp.where(in_head, m_new, m_lanes)
        c_s[bi, pp] = scale_rows * c_prev + c_new
        n_s[bi, pp] = scale_lanes * n_prev + n_new
        m_s[bi, pp] = m_lanes

    @pl.when(ci == pl.num_programs(2) - 1)
    def _():
        c_out_ref[...] = c_s[...]
        n_out_ref[...] = n_s[...]
        m_out_ref[...] = m_s[...]


def _mlstm(q, kt, v, o, gcol, grow, head_norm, c0, n0, m0, chunk, pps, bb):
    b, t, _ = q.shape
    pairs = MLSTM_HEADS // 2
    state = lambda bi, p, ci: (bi, p, 0, 0)
    seq = lambda bi, p, ci: (bi, ci, p)
    return pl.pallas_call(
        functools.partial(_mlstm_kernel, chunk=chunk, pps=pps, bb=bb),
        grid=(b // bb, pairs // pps, t // chunk),
        in_specs=[
            pl.BlockSpec((bb, chunk, pps * LANES), seq),
            pl.BlockSpec((bb, pps * LANES, chunk), lambda bi, p, ci: (bi, p, ci)),
            pl.BlockSpec((bb, chunk, pps * 2 * MLSTM_DV), seq),
            pl.BlockSpec((bb, chunk, pps * 2 * MLSTM_DV), seq),
            pl.BlockSpec((bb, pps, chunk, 4), lambda bi, p, ci: (bi, p, ci, 0)),
            pl.BlockSpec((bb, pps, 4, chunk), lambda bi, p, ci: (bi, p, 0, ci)),
            pl.BlockSpec((1, pps * 2 * MLSTM_DV), lambda bi, p, ci: (0, p)),
            pl.BlockSpec((bb, pps, 2 * MLSTM_DK, MLSTM_DV), state),
            pl.BlockSpec((bb, pps, 1, 2 * MLSTM_DK), state),
            pl.BlockSpec((bb, pps, 1, 2 * MLSTM_DK), state),
        ],
        out_specs=[
            pl.BlockSpec((bb, chunk, pps * 2 * MLSTM_DV), seq),
            pl.BlockSpec((bb, pps, 2 * MLSTM_DK, MLSTM_DV), state),
            pl.BlockSpec((bb, pps, 1, 2 * MLSTM_DK), state),
            pl.BlockSpec((bb, pps, 1, 2 * MLSTM_DK), state),
        ],
        out_shape=[
            jax.ShapeDtypeStruct((b, t, D_MODEL), BF16),
            jax.ShapeDtypeStruct((b, pairs, 2 * MLSTM_DK, MLSTM_DV), F32),
            jax.ShapeDtypeStruct((b, pairs, 1, 2 * MLSTM_DK), F32),
            jax.ShapeDtypeStruct((b, pairs, 1, 2 * MLSTM_DK), F32),
        ],
        scratch_shapes=[
            pltpu.VMEM((bb, pps, 2 * MLSTM_DK, MLSTM_DV), F32),
            pltpu.VMEM((bb, pps, 1, 2 * MLSTM_DK), F32),
            pltpu.VMEM((bb, pps, 1, 2 * MLSTM_DK), F32),
        ],
        compiler_params=_params("parallel", "parallel", "arbitrary"),
        name="mlstm_chunkwise",
    )(q, kt, v, o, gcol, grow, head_norm, c0, n0, m0)


def _pad_lanes(w, width=LANES):
    return jnp.pad(w, ((0, 0), (0, width - w.shape[1])))


def _pad_rows(w, height=LANES):
    return jnp.pad(w, ((0, height - w.shape[0]), (0, 0)))


def _router_weights(w_group, b_group, w_router, b_router):
    w = _pad_lanes(jnp.concatenate([w_router, w_group], axis=1).astype(F32))
    b = _pad_lanes(jnp.concatenate([b_router, b_group])[None, :].astype(F32))
    return w, b


def _mlstm_mixer(x2, y3, b, t, g, w_in, b_i, b_f, head_norm, c0, n0, m0, tm, chunk, pps, bb, transposed):
    pairs = MLSTM_HEADS // 2
    v_end = 2 * MLSTM_QK + 2 * D_MODEL
    w_qvo = jnp.concatenate([w_in[:, :MLSTM_QK], w_in[:, 2 * MLSTM_QK:v_end]], axis=1).astype(BF16)
    w_k = w_in[:, MLSTM_QK:2 * MLSTM_QK]
    w_g = _pad_lanes(w_in[:, v_end:]).astype(BF16)
    b_g = _pad_lanes(jnp.concatenate([b_i, b_f])[None, :].astype(F32))
    if transposed:
        x1, q, kt, v, o, gates = _mlstm_proj(x2, y3, g, w_qvo, w_k.T.astype(BF16), w_g, b_g, tm, seq_len=t)
    else:
        x1, q, k, v, o, gates = _mlstm_proj(x2, y3, g, w_qvo, w_k.astype(BF16), w_g, b_g, tm)
        kt = jnp.swapaxes(k.reshape(b, t, MLSTM_QK), 1, 2)
    q = q.reshape(b, t, MLSTM_QK)
    v = v.reshape(b, t, D_MODEL)
    o = o.reshape(b, t, D_MODEL)
    gates = gates.reshape(b, t, 2, pairs, 2)
    gcol = gates.transpose(0, 3, 1, 2, 4).reshape(b, pairs, t, 4)
    grow = jnp.swapaxes(gcol, 2, 3)
    c0 = c0.reshape(b, pairs, 2 * MLSTM_DK, MLSTM_DV)
    n0 = n0.reshape(b, pairs, 1, 2 * MLSTM_DK)
    m0 = jnp.repeat(m0.reshape(b, pairs, 1, 2), MLSTM_DK, axis=3)
    h, c_f, n_f, m_f = _mlstm(q, kt, v, o, gcol, grow, head_norm.reshape(1, D_MODEL), c0, n0, m0, chunk, pps, bb)
    c_f = c_f.reshape(b, MLSTM_HEADS, MLSTM_DK, MLSTM_DV)
    n_f = n_f.reshape(b, MLSTM_HEADS, MLSTM_DK)
    m_f = m_f.reshape(b, MLSTM_HEADS, MLSTM_DK)[:, :, 0]
    return x1, h.reshape(b * t, D_MODEL), c_f, n_f, m_f


def kernel(x_prompt, x_sample, cache_k, cache_v, cache_logf, page_table, state_C, state_n, state_m, norm_mix, norm_ffn, norm_final, fox_w_in, fox_b_f, fox_w_out, mlstm_w_in, mlstm_b_i, mlstm_b_f, mlstm_head_norm, mlstm_w_out, moe_w_group, moe_b_group, moe_w_router, moe_b_router, moe_w_gate, moe_w_up, moe_w_down):
    bp, tp, _ = x_prompt.shape
    bs, ts, _ = x_sample.shape
    n_p, n_s = bp * tp, bs * ts
    tm_p, tm_s = min(512, tp), min(512, n_s)
    xp = x_prompt.reshape(n_p, D_MODEL)
    xs = x_sample.reshape(n_s, D_MODEL)
    g_final = norm_final[None, :]
    chunk_p, chunk_s = min(4096, n_p), min(4096, n_s)

    def ffn(x, o, w_out, layer, tm, chunk):
        w_r, b_r = _router_weights(moe_w_group[layer], moe_b_group[layer], moe_w_router[layer], moe_b_router[layer])
        x_new, h3, route = _outproj_router(x, o, w_out.astype(BF16), norm_ffn[layer][None, :], w_r, b_r, tm)
        return x_new, _moe(h3, route, moe_w_gate, moe_w_up, moe_w_down, layer, chunk)

    g0 = norm_mix[0][None, :]
    w_in = fox_w_in[0]
    wq, wk, wv, wf = (w_in[:, :D_MODEL], w_in[:, D_MODEL:2 * D_MODEL], w_in[:, 2 * D_MODEL:3 * D_MODEL],
                      w_in[:, 3 * D_MODEL:])
    b_f = fox_b_f[0].astype(F32)
    wq_b = wq.astype(BF16)
    qp, kpt, vpt, kptb, vptb, lfpt = _fox_proj(
        xp, g0, wq_b, wk.T.astype(BF16), wv.T.astype(BF16), _pad_rows(wf.T).astype(BF16),
        _pad_rows(b_f[:, None]), tm_p, BF16, seq_len=tp)
    qs, ks, vs, _, _, lfs = _fox_proj(
        xs, g0, wq_b, wk.astype(BF16), wv.astype(BF16), _pad_lanes(wf).astype(BF16),
        _pad_lanes(b_f[None, :]), tm_s, F32)

    c_t = _cumsum_lanes(lfpt.reshape(bp * FOX_HEADS, tp)).reshape(bp, FOX_HEADS // 2, 2, tp)
    op = _fox_prompt_attention(qp.reshape(bp, tp, D_MODEL), kptb, vptb, c_t, min(512, tp))
    n_pool, page = cache_k.shape[1], cache_k.shape[2]
    feat_major = lambda c: c.transpose(0, 2, 3, 1).reshape(n_pool, D_MODEL, page)
    seq_last = lambda a: jnp.swapaxes(a.reshape(bs, ts, -1), 1, 2)
    os_ = _fox_sample_attention(page_table, qs.reshape(bs, ts, D_MODEL), feat_major(cache_k[0]), feat_major(cache_v[0]),
                                jnp.swapaxes(cache_logf[0], 1, 2), seq_last(ks), seq_last(vs), seq_last(lfs))
    xp, yp3 = ffn(xp, op.reshape(n_p, D_MODEL), fox_w_out[0], 0, tm_p, chunk_p)
    xs, ys3 = ffn(xs, os_.reshape(n_s, D_MODEL), fox_w_out[0], 0, tm_s, chunk_s)

    g1 = norm_mix[1][None, :]
    zc = jnp.zeros((bp, MLSTM_HEADS, MLSTM_DK, MLSTM_DV), F32)
    zn = jnp.zeros((bp, MLSTM_HEADS, MLSTM_DK), F32)
    zm = jnp.zeros((bp, MLSTM_HEADS), F32)
    xp, hp, cpf, npf, mpf = _mlstm_mixer(xp, yp3, bp, tp, g1, mlstm_w_in[0], mlstm_b_i[0], mlstm_b_f[0],
                                         mlstm_head_norm[0], zc, zn, zm, tm_p, min(256, tp), 2, 1, True)
    xs, hs, csf, nsf, msf = _mlstm_mixer(xs, ys3, bs, ts, g1, mlstm_w_in[0], mlstm_b_i[0], mlstm_b_f[0],
                                         mlstm_head_norm[0], state_C[0], state_n[0], state_m[0], tm_s, ts,
                                         MLSTM_HEADS // 2, 4 if bs % 4 == 0 else 1, False)
    xp, yp3 = ffn(xp, hp, mlstm_w_out[0], 1, tm_p, chunk_p)
    xs, ys3 = ffn(xs, hs, mlstm_w_out[0], 1, tm_s, chunk_s)
    yp = _residual_norm(xp, yp3, g_final, tm_p)
    ys = _residual_norm(xs, ys3, g_final, tm_s)

    hd = (FOX_HEADS, FOX_HEAD_DIM)
    time_major = lambda a: a.reshape(bp, FOX_HEADS, -1, tp).transpose(0, 3, 1, 2)
    return (yp.reshape(bp, tp, D_MODEL), ys.reshape(bs, ts, D_MODEL),
            time_major(kpt)[None], time_major(vpt)[None], jnp.swapaxes(lfpt, 1, 2)[None],
            ks.reshape(1, bs, ts, *hd), vs.reshape(1, bs, ts, *hd), lfs.reshape(1, bs, ts, FOX_HEADS),
            cpf[None], npf[None], mpf[None], csf[None], nsf[None], msf[None])
```

```python
import functools

import jax
import jax.numpy as jnp
from jax import lax
from jax.experimental import pallas as pl
from jax.experimental.pallas import tpu as pltpu

D_MODEL = 1024
FOX_HEADS = 16
FOX_HEAD_DIM = 64
MLSTM_HEADS = 8
MLSTM_DK = 64
MLSTM_DV = 128
MLSTM_QK = MLSTM_HEADS * MLSTM_DK
MOE_GROUPS = 4
MOE_EXPERTS_PER_GROUP = 8
MOE_EXPERTS = 32
MOE_HIDDEN = 256
RMS_EPS = 1e-6

LANES = 128
VMEM_LIMIT_BYTES = 56 * 1024 * 1024

F32 = jnp.float32
BF16 = jnp.bfloat16
NEG_INF = float("-inf")


def _params(*sem):
    return pltpu.CompilerParams(dimension_semantics=sem, vmem_limit_bytes=VMEM_LIMIT_BYTES)


def _dot(a, b):
    return jnp.dot(a, b, preferred_element_type=F32)


def _dot_nt(a, b):
    return lax.dot_general(a, b, (((1,), (1,)), ((), ())), preferred_element_type=F32)


def _rmsnorm(x, g):
    return x * lax.rsqrt(jnp.mean(x * x, axis=-1, keepdims=True) + RMS_EPS) * g


def _log_sigmoid(z):
    return jnp.minimum(z, 0.0) - jnp.log1p(jnp.exp(-jnp.abs(z)))


def _sigmoid(z):
    return 1.0 / (1.0 + jnp.exp(-z))


def _upper_tri(n):
    r = lax.broadcasted_iota(jnp.int32, (n, n), 0)
    c = lax.broadcasted_iota(jnp.int32, (n, n), 1)
    return (r <= c).astype(F32)


def _split3(x):
    hi = x.astype(BF16).astype(F32)
    mid = (x - hi).astype(BF16).astype(F32)
    lo = (x - hi - mid).astype(BF16).astype(F32)
    return hi, mid, lo


def _dot_f32_by_01(x, m01):
    return _dot(jnp.concatenate(_split3(x), axis=1).astype(BF16), jnp.concatenate([m01] * 3, axis=0).astype(BF16))


def _dot_01_by_f32(m01, x):
    return _dot(jnp.concatenate([m01] * 3, axis=1).astype(BF16), jnp.concatenate(_split3(x), axis=0).astype(BF16))


def _fox_proj_kernel(x_ref, g_ref, wq_ref, wk_ref, wv_ref, wf_ref, bf_ref,
                     q_ref, k_ref, v_ref, kb_ref, vb_ref, lf_ref, *, transposed):
    hb = _rmsnorm(x_ref[...], g_ref[...]).astype(BF16)
    q = _dot(hb, wq_ref[...])
    q_ref[...] = (q * (FOX_HEAD_DIM ** -0.5)).astype(q_ref.dtype)
    if transposed:
        k = _dot_nt(wk_ref[...], hb)
        v = _dot_nt(wv_ref[...], hb)
        z = _dot_nt(wf_ref[...], hb) + bf_ref[...]
        k_ref[0] = k
        kb_ref[0] = k.astype(BF16)
        v_ref[0] = v
        vb_ref[0] = v.astype(BF16)
        lf_ref[0] = _log_sigmoid(z)[:FOX_HEADS, :]
    else:
        k = _dot(hb, wk_ref[...])
        v = _dot(hb, wv_ref[...])
        z = _dot(hb, wf_ref[...]) + bf_ref[...]
        k_ref[...] = k
        kb_ref[...] = k.astype(BF16)
        v_ref[...] = v
        vb_ref[...] = v.astype(BF16)
        lf_ref[...] = _log_sigmoid(z)[:, :FOX_HEADS]


def _fox_proj(x, g, wq, wk, wv, wf, bf, tm, q_dtype, seq_len=None):
    n = x.shape[0]
    transposed = seq_len is not None
    row = lambda i: (i, 0)
    const = lambda i: (0, 0)
    if transposed:
        nt = seq_len // tm
        b = n // seq_len
        tmap = lambda i: (i // nt, 0, i % nt)
        kv_spec = pl.BlockSpec((1, D_MODEL, tm), tmap)
        lf_spec = pl.BlockSpec((1, FOX_HEADS, tm), tmap)
        kv_shape = (b, D_MODEL, seq_len)
        lf_shape = (b, FOX_HEADS, seq_len)
    else:
        kv_spec = pl.BlockSpec((tm, D_MODEL), row)
        lf_spec = pl.BlockSpec((tm, FOX_HEADS), row)
        kv_shape = (n, D_MODEL)
        lf_shape = (n, FOX_HEADS)
    return pl.pallas_call(
        functools.partial(_fox_proj_kernel, transposed=transposed),
        grid=(n // tm,),
        in_specs=[
            pl.BlockSpec((tm, D_MODEL), row),
            pl.BlockSpec((1, D_MODEL), const),
            pl.BlockSpec(wq.shape, const),
            pl.BlockSpec(wk.shape, const),
            pl.BlockSpec(wv.shape, const),
            pl.BlockSpec(wf.shape, const),
            pl.BlockSpec(bf.shape, const),
        ],
        out_specs=[pl.BlockSpec((tm, D_MODEL), row), kv_spec, kv_spec, kv_spec, kv_spec, lf_spec],
        out_shape=[
            jax.ShapeDtypeStruct((n, D_MODEL), q_dtype),
            jax.ShapeDtypeStruct(kv_shape, F32),
            jax.ShapeDtypeStruct(kv_shape, F32),
            jax.ShapeDtypeStruct(kv_shape, BF16),
            jax.ShapeDtypeStruct(kv_shape, BF16),
            jax.ShapeDtypeStruct(lf_shape, F32),
        ],
        compiler_params=_params("parallel"),
        name="fox_proj",
    )(x, g, wq, wk, wv, wf, bf)


def _mlstm_proj_kernel(x_ref, y3_ref, g_ref, w_ref, wk_ref, wg_ref, bg_ref,
                       x1_ref, q_ref, k_ref, v_ref, o_ref, gate_ref, *, transposed):
    x1 = x_ref[...] + _from_slabs(y3_ref)
    x1_ref[...] = x1
    hb = _rmsnorm(x1, g_ref[...]).astype(BF16)
    q = _dot(hb, w_ref[:, 0:MLSTM_QK])
    q_ref[...] = (q * (MLSTM_DK ** -0.5)).astype(BF16)
    if transposed:
        k_ref[0] = _dot_nt(wk_ref[...], hb).astype(BF16)
    else:
        k_ref[...] = _dot(hb, wk_ref[...]).astype(BF16)
    v_ref[...] = _dot(hb, w_ref[:, MLSTM_QK:MLSTM_QK + D_MODEL]).astype(BF16)
    o_ref[...] = _dot(hb, w_ref[:, MLSTM_QK + D_MODEL:MLSTM_QK + 2 * D_MODEL]).astype(BF16)
    z = _dot(hb, wg_ref[...]) + bg_ref[...]
    lane = lax.broadcasted_iota(jnp.int32, z.shape, 1)
    gates = jnp.where(lane < MLSTM_HEADS, z, _log_sigmoid(z))
    gate_ref[...] = gates[:, :2 * MLSTM_HEADS]


def _mlstm_proj(x, y3, g, w_qvo, wk, w_g, b_g, tm, seq_len=None):
    n = x.shape[0]
    transposed = seq_len is not None
    row = lambda i: (i, 0)
    const = lambda i: (0, 0)
    if transposed:
        nt = seq_len // tm
        k_spec = pl.BlockSpec((1, MLSTM_QK, tm), lambda i: (i // nt, 0, i % nt))
        k_shape = (n // seq_len, MLSTM_QK, seq_len)
    else:
        k_spec = pl.BlockSpec((tm, MLSTM_QK), row)
        k_shape = (n, MLSTM_QK)
    return pl.pallas_call(
        functools.partial(_mlstm_proj_kernel, transposed=transposed),
        grid=(n // tm,),
        in_specs=[
            pl.BlockSpec((tm, D_MODEL), row),
            pl.BlockSpec((tm, D_MODEL // LANES, LANES), lambda i: (i, 0, 0)),
            pl.BlockSpec((1, D_MODEL), const),
            pl.BlockSpec(w_qvo.shape, const),
            pl.BlockSpec(wk.shape, const),
            pl.BlockSpec((D_MODEL, LANES), const),
            pl.BlockSpec((1, LANES), const),
        ],
        out_specs=[
            pl.BlockSpec((tm, D_MODEL), row),
            pl.BlockSpec((tm, MLSTM_QK), row),
            k_spec,
            pl.BlockSpec((tm, D_MODEL), row),
            pl.BlockSpec((tm, D_MODEL), row),
            pl.BlockSpec((tm, 2 * MLSTM_HEADS), row),
        ],
        out_shape=[
            jax.ShapeDtypeStruct((n, D_MODEL), F32),
            jax.ShapeDtypeStruct((n, MLSTM_QK), BF16),
            jax.ShapeDtypeStruct(k_shape, BF16),
            jax.ShapeDtypeStruct((n, D_MODEL), BF16),
            jax.ShapeDtypeStruct((n, D_MODEL), BF16),
            jax.ShapeDtypeStruct((n, 2 * MLSTM_HEADS), F32),
        ],
        compiler_params=_params("parallel"),
        name="mlstm_proj",
    )(x, y3, g, w_qvo, wk, w_g, b_g)


def _cumsum_lanes_kernel(x_ref, o_ref):
    rows, t = x_ref.shape
    tri = _upper_tri(LANES)
    carry = jnp.zeros((rows, 1), F32)
    for j in range(t // LANES):
        blk = _dot_f32_by_01(x_ref[:, j * LANES:(j + 1) * LANES], tri) + carry
        o_ref[:, j * LANES:(j + 1) * LANES] = blk
        carry = blk[:, LANES - 1:LANES]


def _cumsum_lanes(x):
    return pl.pallas_call(
        _cumsum_lanes_kernel,
        out_shape=jax.ShapeDtypeStruct(x.shape, F32),
        compiler_params=_params(),
        name="cumsum_lanes",
    )(x)


def _fox_prompt_kernel(q_ref, kt_ref, vt_ref, c_ref, o_ref, *, blk):
    qi = pl.program_id(2)
    q2 = q_ref[0]
    lane = lax.broadcasted_iota(jnp.int32, (1, LANES), 1)
    head0 = lane < FOX_HEAD_DIM
    zero = jnp.zeros_like(q2)
    qh = (jnp.where(head0, q2, zero), jnp.where(head0, zero, q2))
    row = lax.broadcasted_iota(jnp.int32, (blk, blk), 0)
    col = lax.broadcasted_iota(jnp.int32, (blk, blk), 1)
    causal = col <= row

    def block(j, carry, masked):
        start = pl.multiple_of(j * blk, blk)
        kt = kt_ref[0, :, pl.ds(start, blk)]
        vt = vt_ref[0, :, pl.ds(start, blk)]
        out = []
        for h in range(2):
            m, l, acc = carry[h]
            cs = c_ref[0, 0, h:h + 1, pl.ds(start, blk)]
            s = _dot(qh[h], kt) - cs
            if masked:
                s = jnp.where(causal, s, NEG_INF)
            m_new = jnp.maximum(m, jnp.max(s, axis=1, keepdims=True))
            alpha = jnp.exp(m - m_new)
            p = jnp.exp(s - m_new)
            l = alpha * l + jnp.sum(p, axis=1, keepdims=True)
            acc = alpha * acc + _dot_nt(p.astype(BF16), vt)
            out.append((m_new, l, acc))
        return tuple(out)

    one = (jnp.full((blk, 1), NEG_INF, F32), jnp.zeros((blk, 1), F32), jnp.zeros((blk, LANES), F32))
    carry = lax.fori_loop(0, qi, functools.partial(block, masked=False), (one, one))
    (_, l0, acc0), (_, l1, acc1) = block(qi, carry, True)
    o_ref[0] = jnp.where(head0, acc0 / l0, acc1 / l1).astype(o_ref.dtype)


def _fox_prompt_attention(q, kt, vt, c_t, blk):
    b, t, _ = q.shape
    pairs = FOX_HEADS // 2
    return pl.pallas_call(
        functools.partial(_fox_prompt_kernel, blk=blk),
        grid=(b, pairs, t // blk),
        in_specs=[
            pl.BlockSpec((1, blk, LANES), lambda bi, p, qi: (bi, qi, p)),
            pl.BlockSpec((1, LANES, t), lambda bi, p, qi: (bi, p, 0)),
            pl.BlockSpec((1, LANES, t), lambda bi, p, qi: (bi, p, 0)),
            pl.BlockSpec((1, 1, 2, t), lambda bi, p, qi: (bi, p, 0, 0)),
        ],
        out_specs=pl.BlockSpec((1, blk, LANES), lambda bi, p, qi: (bi, qi, p)),
        out_shape=jax.ShapeDtypeStruct((b, t, D_MODEL), BF16),
        compiler_params=_params("parallel", "parallel", "arbitrary"),
        name="fox_prompt_attention",
    )(q, kt, vt, c_t)


def _fox_sample_kernel(pt_ref, q_ref, *refs, n_pages, page, pps):
    kc_refs, vc_refs, lfc_refs = refs[0:pps], refs[pps:2 * pps], refs[2 * pps:3 * pps]
    (kn_ref, vn_ref, lfn_ref, o_ref,
     qbd_ref, m_ref, l_ref, acc_ref, carry_ref, kpad_ref, vpad_ref, lfpad_ref) = refs[3 * pps:]
    n_steps = n_pages // pps
    p = pl.program_id(1)
    n_q = q_ref.shape[1]
    chunk = 4 * FOX_HEAD_DIM
    n_chunks = D_MODEL // chunk
    rows = FOX_HEADS * n_q
    crow = 4 * n_q
    lane_c = lax.broadcasted_iota(jnp.int32, (1, chunk), 1)

    @pl.when(p == 0)
    def _init():
        q = q_ref[0]
        for c in range(n_chunks):
            qc = q[:, c * chunk:(c + 1) * chunk]
            pieces = [jnp.where(lane_c // FOX_HEAD_DIM == hl, qc, 0.0) for hl in range(4)]
            qbd_ref[c] = jnp.concatenate(pieces, axis=0).astype(BF16)
        m_ref[...] = jnp.full(m_ref.shape, NEG_INF, F32)
        l_ref[...] = jnp.zeros(l_ref.shape, F32)
        acc_ref[...] = jnp.zeros(acc_ref.shape, F32)
        carry_ref[...] = jnp.zeros(carry_ref.shape, F32)

    def process(get_kt, get_vt, lf_ts, mask):
        nb = len(lf_ts)
        local = _dot_f32_by_01(jnp.concatenate(lf_ts, axis=0), _upper_tri(page))
        carry = carry_ref[...]
        biases = []
        for j in range(nb):
            cum = local[j * FOX_HEADS:(j + 1) * FOX_HEADS] + carry
            carry = cum[:, page - 1:page]
            biases.append(jnp.concatenate(
                [jnp.broadcast_to(cum[h:h + 1, :], (n_q, page)) for h in range(FOX_HEADS)], axis=0))
        carry_ref[...] = carry
        bias = jnp.concatenate(biases, axis=1)
        s = jnp.concatenate(
            [_dot(qbd_ref[c], jnp.concatenate([get_kt(j, c) for j in range(nb)], axis=1)) for c in range(n_chunks)],
            axis=0) - bias
        if mask is not None:
            s = jnp.where(mask, s, NEG_INF)
        m_old = m_ref[...]
        m_new = jnp.maximum(m_old, jnp.max(s, axis=1, keepdims=True))
        alpha = jnp.exp(m_old - m_new)
        pr = jnp.exp(s - m_new)
        l_ref[...] = alpha * l_ref[...] + jnp.sum(pr, axis=1, keepdims=True)
        m_ref[...] = m_new
        pb = pr.astype(BF16)
        for c in range(n_chunks):
            sl = slice(c * crow, (c + 1) * crow)
            vt = jnp.concatenate([get_vt(j, c) for j in range(nb)], axis=1)
            acc_ref[sl, :] = alpha[sl] * acc_ref[sl, :] + _dot_nt(pb[sl], vt)

    @pl.when(p < n_steps)
    def _past():
        process(lambda j, c: kc_refs[j][0, c * chunk:(c + 1) * chunk, :].astype(BF16),
                lambda j, c: vc_refs[j][0, c * chunk:(c + 1) * chunk, :].astype(BF16),
                [r[0] for r in lfc_refs], None)

    @pl.when(p == n_steps)
    def _new():
        kpad_ref[...] = jnp.zeros(kpad_ref.shape, F32)
        vpad_ref[...] = jnp.zeros(vpad_ref.shape, F32)
        lfpad_ref[...] = jnp.zeros(lfpad_ref.shape, F32)
        kpad_ref[:, 0:n_q] = kn_ref[0]
        vpad_ref[:, 0:n_q] = vn_ref[0]
        lfpad_ref[:, 0:n_q] = lfn_ref[0]
        key = lax.broadcasted_iota(jnp.int32, (rows, page), 1)
        qry = lax.broadcasted_iota(jnp.int32, (rows, page), 0) % n_q
        process(lambda j, c: kpad_ref[c * chunk:(c + 1) * chunk, :].astype(BF16),
                lambda j, c: vpad_ref[c * chunk:(c + 1) * chunk, :].astype(BF16),
                [lfpad_ref[...]], key <= qry)
        inv_l = 1.0 / l_ref[...]
        for c in range(n_chunks):
            blk = acc_ref[c * crow:(c + 1) * crow, :] * inv_l[c * crow:(c + 1) * crow]
            out = jnp.zeros((n_q, chunk), F32)
            for hl in range(4):
                out = out + jnp.where(lane_c // FOX_HEAD_DIM == hl, blk[hl * n_q:(hl + 1) * n_q, :], 0.0)
            o_ref[0, :, c * chunk:(c + 1) * chunk] = out


def _fox_sample_attention(page_table, q, cache_kt, cache_vt, cache_lft, kt_new, vt_new, lft_new):
    b, n_q, _ = q.shape
    n_pages = page_table.shape[1]
    page = cache_kt.shape[2]
    rows = FOX_HEADS * n_q
    chunk = 4 * FOX_HEAD_DIM
    pps = max(d for d in (1, 2, 4, 8) if n_pages % d == 0)

    def cache_map(j):
        return lambda bi, p, pt: (pt[bi, jnp.minimum(p * pps + j, n_pages - 1)], 0, 0)

    per_b = lambda bi, p, pt: (bi, 0, 0)
    grid_spec = pltpu.PrefetchScalarGridSpec(
        num_scalar_prefetch=1,
        grid=(b, n_pages // pps + 1),
        in_specs=[
            pl.BlockSpec((1, n_q, D_MODEL), per_b),
            *[pl.BlockSpec((1, D_MODEL, page), cache_map(j)) for j in range(pps)],
            *[pl.BlockSpec((1, D_MODEL, page), cache_map(j)) for j in range(pps)],
            *[pl.BlockSpec((1, FOX_HEADS, page), cache_map(j)) for j in range(pps)],
            pl.BlockSpec((1, D_MODEL, n_q), per_b),
            pl.BlockSpec((1, D_MODEL, n_q), per_b),
            pl.BlockSpec((1, FOX_HEADS, n_q), per_b),
        ],
        out_specs=pl.BlockSpec((1, n_q, D_MODEL), per_b),
        scratch_shapes=[
            pltpu.VMEM((D_MODEL // chunk, 4 * n_q, chunk), BF16),
            pltpu.VMEM((rows, 1), F32),
            pltpu.VMEM((rows, 1), F32),
            pltpu.VMEM((rows, chunk), F32),
            pltpu.VMEM((FOX_HEADS, 1), F32),
            pltpu.VMEM((D_MODEL, page), F32),
            pltpu.VMEM((D_MODEL, page), F32),
            pltpu.VMEM((FOX_HEADS, page), F32),
        ],
    )
    return pl.pallas_call(
        functools.partial(_fox_sample_kernel, n_pages=n_pages, page=page, pps=pps),
        grid_spec=grid_spec,
        out_shape=jax.ShapeDtypeStruct((b, n_q, D_MODEL), F32),
        compiler_params=_params("parallel", "arbitrary"),
        name="fox_sample_attention",
    )(page_table, q, *[cache_kt] * pps, *[cache_vt] * pps, *[cache_lft] * pps, kt_new, vt_new, lft_new)


def _to_slabs(ref3, x):
    for j in range(D_MODEL // LANES):
        ref3[:, j, :] = x[:, j * LANES:(j + 1) * LANES]


def _from_slabs(ref3):
    return jnp.concatenate([ref3[:, j, :] for j in range(D_MODEL // LANES)], axis=1)


def _outproj_router_kernel(x_ref, o_ref, w_ref, g_ref, wr_ref, br_ref, xn_ref, h3_ref, route_ref):
    x_new = x_ref[...] + _dot(o_ref[...].astype(BF16), w_ref[...])
    xn_ref[...] = x_new
    h = _rmsnorm(x_new, g_ref[...])
    _to_slabs(h3_ref, h)
    w_r = wr_ref[...]
    h_hi, w_hi = h.astype(BF16), w_r.astype(BF16)
    h_lo = (h - h_hi.astype(F32)).astype(BF16)
    w_lo = (w_r - w_hi.astype(F32)).astype(BF16)
    logits = _dot(h_hi, w_hi) + (_dot(h_lo, w_hi) + _dot(h_hi, w_lo)) + br_ref[...]
    lane = lax.broadcasted_iota(jnp.int32, logits.shape, 1)
    big = jnp.int32(LANES)
    is_group = (lane >= MOE_EXPERTS) & (lane < MOE_EXPERTS + MOE_GROUPS)
    gl = jnp.where(is_group, logits, NEG_INF)
    g_max = jnp.max(gl, axis=1, keepdims=True)
    g_sel = jnp.min(jnp.where(gl == g_max, lane - MOE_EXPERTS, big), axis=1, keepdims=True)
    p_g = 1.0 / jnp.sum(jnp.where(is_group, jnp.exp(gl - g_max), 0.0), axis=1, keepdims=True)
    in_group = (lane < MOE_EXPERTS) & (lane // MOE_EXPERTS_PER_GROUP == g_sel)
    el = jnp.where(in_group, logits, NEG_INF)
    v1 = jnp.max(el, axis=1, keepdims=True)
    i1 = jnp.min(jnp.where(el == v1, lane, big), axis=1, keepdims=True)
    el2 = jnp.where(lane == i1, NEG_INF, el)
    v2 = jnp.max(el2, axis=1, keepdims=True)
    i2 = jnp.min(jnp.where(el2 == v2, lane, big), axis=1, keepdims=True)
    e2 = jnp.exp(v2 - v1)
    w1 = p_g / (1.0 + e2)
    w2 = p_g * e2 / (1.0 + e2)
    route_ref[...] = jnp.where(lane == 0, i1.astype(F32),
                               jnp.where(lane == 1, i2.astype(F32),
                                         jnp.where(lane == 2, w1, jnp.where(lane == 3, w2, 0.0))))


def _outproj_router(x, o, w_out, g_ffn, w_r, b_r, tm):
    n = x.shape[0]
    row = lambda i: (i, 0)
    const = lambda i: (0, 0)
    return pl.pallas_call(
        _outproj_router_kernel,
        grid=(n // tm,),
        in_specs=[
            pl.BlockSpec((tm, D_MODEL), row),
            pl.BlockSpec((tm, D_MODEL), row),
            pl.BlockSpec((D_MODEL, D_MODEL), const),
            pl.BlockSpec((1, D_MODEL), const),
            pl.BlockSpec((D_MODEL, LANES), const),
            pl.BlockSpec((1, LANES), const),
        ],
        out_specs=[
            pl.BlockSpec((tm, D_MODEL), row),
            pl.BlockSpec((tm, D_MODEL // LANES, LANES), lambda i: (i, 0, 0)),
            pl.BlockSpec((tm, LANES), row),
        ],
        out_shape=[
            jax.ShapeDtypeStruct((n, D_MODEL), F32),
            jax.ShapeDtypeStruct((n, D_MODEL // LANES, LANES), F32),
            jax.ShapeDtypeStruct((n, LANES), F32),
        ],
        compiler_params=_params("parallel"),
        name="outproj_router",
    )(x, o, w_out, g_ffn, w_r, b_r)


MOE_TILE = 256
MOE_TILE_PITCH = MOE_TILE + 1
MOE_UNROLL = 8


def _moe_kernel(off_ref, tok_ref, wt_ref, h3_ref, wg_ref, wu_ref, wd_ref, y3_ref, xg_ref, yg_ref):
    c = pl.program_id(0)
    e = pl.program_id(1)
    n_col = D_MODEL // LANES
    pitch = MOE_TILE_PITCH

    @pl.when((c == 0) & (e == 0))
    def _():
        xg_ref[...] = jnp.zeros(xg_ref.shape, F32)

    @pl.when(e == 0)
    def _():
        y3_ref[...] = jnp.zeros(y3_ref.shape, F32)

    base = off_ref[c * (MOE_EXPERTS + 1) + e]
    n = off_ref[c * (MOE_EXPERTS + 1) + e + 1] - base

    def gather_row(i, start):
        xg_ref[pl.ds(i, n_col, stride=pitch), :] = h3_ref[tok_ref[start + i]]

    def sub_tile(t, carry):
        start = base + t * MOE_TILE
        cnt = jnp.minimum(MOE_TILE, n - t * MOE_TILE)
        full = cnt // MOE_UNROLL

        def gather_group(g, _):
            for u in range(MOE_UNROLL):
                gather_row(g * MOE_UNROLL + u, start)
            return 0

        def gather_one(i, _):
            gather_row(i, start)
            return 0

        lax.fori_loop(0, full, gather_group, 0)
        lax.fori_loop(full * MOE_UNROLL, cnt, gather_one, 0)
        x = jnp.concatenate([xg_ref[j * pitch:j * pitch + MOE_TILE, :] for j in range(n_col)], axis=1).astype(BF16)
        hg = _dot(x, wg_ref[0, 0].astype(BF16))
        hu = _dot(x, wu_ref[0, 0].astype(BF16))
        y = _dot((hg * _sigmoid(hg) * hu).astype(BF16), wd_ref[0, 0].astype(BF16))
        for j in range(n_col):
            yg_ref[j * pitch:j * pitch + MOE_TILE, :] = y[:, j * LANES:(j + 1) * LANES]

        def scatter_group(g, _):
            rows, vals = [], []
            for u in range(MOE_UNROLL):
                i = g * MOE_UNROLL + u
                r = tok_ref[start + i]
                rows.append(r)
                vals.append(y3_ref[r] + wt_ref[start + i] * yg_ref[pl.ds(i, n_col, stride=pitch), :])
            for r, v in zip(rows, vals):
                y3_ref[r] = v
            return 0

        def scatter_one(i, _):
            r = tok_ref[start + i]
            y3_ref[r] = y3_ref[r] + wt_ref[start + i] * yg_ref[pl.ds(i, n_col, stride=pitch), :]
            return 0

        lax.fori_loop(0, full, scatter_group, 0)
        lax.fori_loop(full * MOE_UNROLL, cnt, scatter_one, 0)
        return carry

    lax.fori_loop(0, (n + MOE_TILE - 1) // MOE_TILE, sub_tile, 0)


def _moe(h3, route, w_gate, w_up, w_down, layer, chunk):
    n = h3.shape[0]
    n_chunks = n // chunk
    pairs = 2 * chunk
    expert = route[:, 0:2].astype(jnp.int32).reshape(n_chunks, pairs)
    weight = route[:, 2:4].reshape(n_chunks, pairs)
    order = jnp.argsort(expert, axis=1, stable=True)
    expert_sorted = jnp.take_along_axis(expert, order, axis=1)
    tok = (order // 2).astype(jnp.int32)
    wt = jnp.take_along_axis(weight, order, axis=1)
    bounds = jnp.arange(MOE_EXPERTS + 1, dtype=jnp.int32)
    off = jnp.sum(expert_sorted[:, None, :] < bounds[None, :, None], axis=2).astype(jnp.int32)
    off = (off + pairs * jnp.arange(n_chunks, dtype=jnp.int32)[:, None]).reshape(-1)
    tok = tok.reshape(-1)
    wt = wt.reshape(-1)

    n_col = D_MODEL // LANES
    tok_map = lambda c, e, *_: (c, 0, 0)
    grid_spec = pltpu.PrefetchScalarGridSpec(
        num_scalar_prefetch=3,
        grid=(n_chunks, MOE_EXPERTS),
        in_specs=[
            pl.BlockSpec((chunk, n_col, LANES), tok_map, pipeline_mode=pl.Buffered(1)),
            pl.BlockSpec((1, 1, D_MODEL, MOE_HIDDEN), lambda c, e, *_: (layer, e, 0, 0)),
            pl.BlockSpec((1, 1, D_MODEL, MOE_HIDDEN), lambda c, e, *_: (layer, e, 0, 0)),
            pl.BlockSpec((1, 1, MOE_HIDDEN, D_MODEL), lambda c, e, *_: (layer, e, 0, 0)),
        ],
        out_specs=pl.BlockSpec((chunk, n_col, LANES), tok_map, pipeline_mode=pl.Buffered(1)),
        scratch_shapes=[
            pltpu.VMEM((n_col * MOE_TILE_PITCH, LANES), F32),
            pltpu.VMEM((n_col * MOE_TILE_PITCH, LANES), F32),
        ],
    )
    return pl.pallas_call(
        _moe_kernel,
        grid_spec=grid_spec,
        out_shape=jax.ShapeDtypeStruct((n, n_col, LANES), F32),
        compiler_params=_params("arbitrary", "arbitrary"),
        name="moe_experts",
    )(off, tok, wt, h3, w_gate, w_up, w_down)


def _residual_norm_kernel(x_ref, y3_ref, g_ref, o_ref):
    o_ref[...] = _rmsnorm(x_ref[...] + _from_slabs(y3_ref), g_ref[...])


def _residual_norm(x, y3, g, tm):
    n = x.shape[0]
    row = lambda i: (i, 0)
    return pl.pallas_call(
        _residual_norm_kernel,
        grid=(n // tm,),
        in_specs=[
            pl.BlockSpec((tm, D_MODEL), row),
            pl.BlockSpec((tm, D_MODEL // LANES, LANES), lambda i: (i, 0, 0)),
            pl.BlockSpec((1, D_MODEL), lambda i: (0, 0)),
        ],
        out_specs=pl.BlockSpec((tm, D_MODEL), row),
        out_shape=jax.ShapeDtypeStruct((n, D_MODEL), F32),
        compiler_params=_params("parallel"),
        name="residual_norm",
    )(x, y3, g)


def _mlstm_kernel(q_ref, kt_ref, v_ref, o_ref, gc_ref, gr_ref, hn_ref, c0_ref, n0_ref, m0_ref,
                  h_ref, c_out_ref, n_out_ref, m_out_ref, c_s, n_s, m_s, *, chunk, pps, bb):
    ci = pl.program_id(2)

    @pl.when(ci == 0)
    def _():
        c_s[...] = c0_ref[...]
        n_s[...] = n0_ref[...]
        m_s[...] = m0_ref[...]

    lane = lax.broadcasted_iota(jnp.int32, (1, LANES), 1)
    srow = lax.broadcasted_iota(jnp.int32, (LANES, 1), 0)
    row = lax.broadcasted_iota(jnp.int32, (chunk, chunk), 0)
    col = lax.broadcasted_iota(jnp.int32, (chunk, chunk), 1)
    causal = col <= row
    lower = causal.astype(F32)
    upper = (row <= col).astype(F32)
    units = [(bi, pp) for bi in range(bb) for pp in range(pps)]
    old_state = {u: (c_s[u], n_s[u], m_s[u]) for u in units}
    new_state = {}
    for bi, pp in units:
        q2 = q_ref[bi, :, pp * LANES:(pp + 1) * LANES]
        kt2 = kt_ref[bi, pp * LANES:(pp + 1) * LANES, :]
        gc = gc_ref[bi, pp]
        gr = gr_ref[bi, pp]
        c_prev, n_prev, m_prev_row = old_state[(bi, pp)]
        b_cols = _dot_01_by_f32(lower, gc)
        b_rows = _dot_f32_by_01(gr, upper)
        c_prev_b = c_prev.astype(BF16)
        kt2f = kt2.astype(F32)
        c_new = jnp.zeros((LANES, MLSTM_DV), F32)
        n_new = jnp.zeros((1, LANES), F32)
        scale_rows = jnp.zeros((LANES, 1), F32)
        scale_lanes = jnp.zeros((1, LANES), F32)
        m_lanes = jnp.zeros((1, LANES), F32)
        for hl in range(2):
            in_head = lane // MLSTM_DK == hl
            in_head_rows = srow // MLSTM_DK == hl
            vsl = slice((2 * pp + hl) * MLSTM_DV, (2 * pp + hl + 1) * MLSTM_DV)
            b_col = b_cols[:, 2 + hl:3 + hl]
            b_row = b_rows[2 + hl:3 + hl, :]
            i_row = gr[hl:hl + 1, :]
            m_prev = m_prev_row[:, hl * MLSTM_DK:hl * MLSTM_DK + 1]
            d = jnp.where(causal, b_col - b_row + i_row, NEG_INF)
            inter = b_col + m_prev
            m_t = jnp.maximum(inter, jnp.max(d, axis=1, keepdims=True))
            w_inter = jnp.exp(inter - m_t)
            qh = jnp.where(in_head, q2, jnp.zeros_like(q2))
            s = jnp.exp(d - m_t) * _dot(qh, kt2)
            vh = v_ref[bi, :, vsl]
            num = _dot(s.astype(BF16), vh) + w_inter * _dot(qh, c_prev_b)
            qn = jnp.sum(qh.astype(F32) * n_prev, axis=1, keepdims=True)
            den = jnp.sum(s, axis=1, keepdims=True) + w_inter * qn
            hval = num / jnp.maximum(jnp.abs(den), jnp.exp(-m_t))
            hnorm = hval * lax.rsqrt(jnp.mean(hval * hval, axis=1, keepdims=True) + RMS_EPS)
            hnorm = hnorm * hn_ref[:, vsl]
            og = o_ref[bi, :, vsl].astype(F32)
            h_ref[bi, :, vsl] = (_sigmoid(og) * hnorm).astype(h_ref.dtype)
            b_last = b_row[:, chunk - 1:chunk]
            decay = b_last - b_row + i_row
            m_new = jnp.maximum(b_last + m_prev, jnp.max(decay, axis=1, keepdims=True))
            w_k = jnp.exp(decay - m_new)
            carry_scale = jnp.exp(b_last + m_prev - m_new)
            kth = jnp.where(in_head_rows, kt2f, 0.0)
            c_new = c_new + _dot((kth * w_k).astype(BF16), vh)
            n_new = n_new + _dot_nt(jnp.concatenate(_split3(w_k), axis=1).astype(BF16),
                                    jnp.concatenate([kth] * 3, axis=1).astype(BF16))
            scale_rows = jnp.where(in_head_rows, carry_scale, scale_rows)
            scale_lanes = jnp.where(in_head, carry_scale, scale_lanes)
            m_lanes = jnp.where(in_head, m_new, m_lanes)
        new_state[(bi, pp)] = (scale_rows * c_prev + c_new, scale_lanes * n_prev + n_new, m_lanes)
    for u in units:
        c_s[u], n_s[u], m_s[u] = new_state[u]

    @pl.when(ci == pl.num_programs(2) - 1)
    def _():
        c_out_ref[...] = c_s[...]
        n_out_ref[...] = n_s[...]
        m_out_ref[...] = m_s[...]


def _mlstm(q, kt, v, o, gcol, grow, head_norm, c0, n0, m0, chunk, pps, bb):
    b, t, _ = q.shape
    pairs = MLSTM_HEADS // 2
    state = lambda bi, p, ci: (bi, p, 0, 0)
    seq = lambda bi, p, ci: (bi, ci, p)
    return pl.pallas_call(
        functools.partial(_mlstm_kernel, chunk=chunk, pps=pps, bb=bb),
        grid=(b // bb, pairs // pps, t // chunk),
        in_specs=[
            pl.BlockSpec((bb, chunk, pps * LANES), seq),
            pl.BlockSpec((bb, pps * LANES, chunk), lambda bi, p, ci: (bi, p, ci)),
            pl.BlockSpec((bb, chunk, pps * 2 * MLSTM_DV), seq),
            pl.BlockSpec((bb, chunk, pps * 2 * MLSTM_DV), seq),
            pl.BlockSpec((bb, pps, chunk, 4), lambda bi, p, ci: (bi, p, ci, 0)),
            pl.BlockSpec((bb, pps, 4, chunk), lambda bi, p, ci: (bi, p, 0, ci)),
            pl.BlockSpec((1, pps * 2 * MLSTM_DV), lambda bi, p, ci: (0, p)),
            pl.BlockSpec((bb, pps, 2 * MLSTM_DK, MLSTM_DV), state),
            pl.BlockSpec((bb, pps, 1, 2 * MLSTM_DK), state),
            pl.BlockSpec((bb, pps, 1, 2 * MLSTM_DK), state),
        ],
        out_specs=[
            pl.BlockSpec((bb, chunk, pps * 2 * MLSTM_DV), seq),
            pl.BlockSpec((bb, pps, 2 * MLSTM_DK, MLSTM_DV), state),
            pl.BlockSpec((bb, pps, 1, 2 * MLSTM_DK), state),
            pl.BlockSpec((bb, pps, 1, 2 * MLSTM_DK), state),
        ],
        out_shape=[
            jax.ShapeDtypeStruct((b, t, D_MODEL), BF16),
            jax.ShapeDtypeStruct((b, pairs, 2 * MLSTM_DK, MLSTM_DV), F32),
            jax.ShapeDtypeStruct((b, pairs, 1, 2 * MLSTM_DK), F32),
            jax.ShapeDtypeStruct((b, pairs, 1, 2 * MLSTM_DK), F32),
        ],
        scratch_shapes=[
            pltpu.VMEM((bb, pps, 2 * MLSTM_DK, MLSTM_DV), F32),
            pltpu.VMEM((bb, pps, 1, 2 * MLSTM_DK), F32),
            pltpu.VMEM((bb, pps, 1, 2 * MLSTM_DK), F32),
        ],
        compiler_params=_params("parallel", "parallel", "arbitrary"),
        name="mlstm_chunkwise",
    )(q, kt, v, o, gcol, grow, head_norm, c0, n0, m0)


def _pad_lanes(w, width=LANES):
    return jnp.pad(w, ((0, 0), (0, width - w.shape[1])))


def _pad_rows(w, height=LANES):
    return jnp.pad(w, ((0, height - w.shape[0]), (0, 0)))


def _router_weights(w_group, b_group, w_router, b_router):
    w = _pad_lanes(jnp.concatenate([w_router, w_group], axis=1).astype(F32))
    b = _pad_lanes(jnp.concatenate([b_router, b_group])[None, :].astype(F32))
    return w, b


def _mlstm_mixer(x2, y3, b, t, g, w_in, b_i, b_f, head_norm, c0, n0, m0, tm, chunk, pps, bb, transposed):
    pairs = MLSTM_HEADS // 2
    v_end = 2 * MLSTM_QK + 2 * D_MODEL
    w_qvo = jnp.concatenate([w_in[:, :MLSTM_QK], w_in[:, 2 * MLSTM_QK:v_end]], axis=1).astype(BF16)
    w_k = w_in[:, MLSTM_QK:2 * MLSTM_QK]
    w_g = _pad_lanes(w_in[:, v_end:]).astype(BF16)
    b_g = _pad_lanes(jnp.concatenate([b_i, b_f])[None, :].astype(F32))
    if transposed:
        x1, q, kt, v, o, gates = _mlstm_proj(x2, y3, g, w_qvo, w_k.T.astype(BF16), w_g, b_g, tm, seq_len=t)
    else:
        x1, q, k, v, o, gates = _mlstm_proj(x2, y3, g, w_qvo, w_k.astype(BF16), w_g, b_g, tm)
        kt = jnp.swapaxes(k.reshape(b, t, MLSTM_QK), 1, 2)
    q = q.reshape(b, t, MLSTM_QK)
    v = v.reshape(b, t, D_MODEL)
    o = o.reshape(b, t, D_MODEL)
    gates = gates.reshape(b, t, 2, pairs, 2)
    gcol = gates.transpose(0, 3, 1, 2, 4).reshape(b, pairs, t, 4)
    grow = jnp.swapaxes(gcol, 2, 3)
    c0 = c0.reshape(b, pairs, 2 * MLSTM_DK, MLSTM_DV)
    n0 = n0.reshape(b, pairs, 1, 2 * MLSTM_DK)
    m0 = jnp.repeat(m0.reshape(b, pairs, 1, 2), MLSTM_DK, axis=3)
    h, c_f, n_f, m_f = _mlstm(q, kt, v, o, gcol, grow, head_norm.reshape(1, D_MODEL), c0, n0, m0, chunk, pps, bb)
    c_f = c_f.reshape(b, MLSTM_HEADS, MLSTM_DK, MLSTM_DV)
    n_f = n_f.reshape(b, MLSTM_HEADS, MLSTM_DK)
    m_f = m_f.reshape(b, MLSTM_HEADS, MLSTM_DK)[:, :, 0]
    return x1, h.reshape(b * t, D_MODEL), c_f, n_f, m_f


def kernel(x_prompt, x_sample, cache_k, cache_v, cache_logf, page_table, state_C, state_n, state_m, norm_mix, norm_ffn, norm_final, fox_w_in, fox_b_f, fox_w_out, mlstm_w_in, mlstm_b_i, mlstm_b_f, mlstm_head_norm, mlstm_w_out, moe_w_group, moe_b_group, moe_w_router, moe_b_router, moe_w_gate, moe_w_up, moe_w_down):
    bp, tp, _ = x_prompt.shape
    bs, ts, _ = x_sample.shape
    n_p, n_s = bp * tp, bs * ts
    tm_p, tm_s = min(512, tp), min(512, n_s)
    xp = x_prompt.reshape(n_p, D_MODEL)
    xs = x_sample.reshape(n_s, D_MODEL)
    g_final = norm_final[None, :]
    chunk_p, chunk_s = min(4096, n_p), min(4096, n_s)

    def ffn(x, o, w_out, layer, tm, chunk):
        w_r, b_r = _router_weights(moe_w_group[layer], moe_b_group[layer], moe_w_router[layer], moe_b_router[layer])
        x_new, h3, route = _outproj_router(x, o, w_out.astype(BF16), norm_ffn[layer][None, :], w_r, b_r, tm)
        return x_new, _moe(h3, route, moe_w_gate, moe_w_up, moe_w_down, layer, chunk)

    g0 = norm_mix[0][None, :]
    w_in = fox_w_in[0]
    wq, wk, wv, wf = (w_in[:, :D_MODEL], w_in[:, D_MODEL:2 * D_MODEL], w_in[:, 2 * D_MODEL:3 * D_MODEL],
                      w_in[:, 3 * D_MODEL:])
    b_f = fox_b_f[0].astype(F32)
    wq_b = wq.astype(BF16)
    qp, kpt, vpt, kptb, vptb, lfpt = _fox_proj(
        xp, g0, wq_b, wk.T.astype(BF16), wv.T.astype(BF16), _pad_rows(wf.T).astype(BF16),
        _pad_rows(b_f[:, None]), tm_p, BF16, seq_len=tp)
    qs, ks, vs, _, _, lfs = _fox_proj(
        xs, g0, wq_b, wk.astype(BF16), wv.astype(BF16), _pad_lanes(wf).astype(BF16),
        _pad_lanes(b_f[None, :]), tm_s, F32)

    c_t = _cumsum_lanes(lfpt.reshape(bp * FOX_HEADS, tp)).reshape(bp, FOX_HEADS // 2, 2, tp)
    op = _fox_prompt_attention(qp.reshape(bp, tp, D_MODEL), kptb, vptb, c_t, min(512, tp))
    n_pool, page = cache_k.shape[1], cache_k.shape[2]
    feat_major = lambda c: c.transpose(0, 2, 3, 1).reshape(n_pool, D_MODEL, page)
    seq_last = lambda a: jnp.swapaxes(a.reshape(bs, ts, -1), 1, 2)
    os_ = _fox_sample_attention(page_table, qs.reshape(bs, ts, D_MODEL), feat_major(cache_k[0]), feat_major(cache_v[0]),
                                jnp.swapaxes(cache_logf[0], 1, 2), seq_last(ks), seq_last(vs), seq_last(lfs))
    xp, yp3 = ffn(xp, op.reshape(n_p, D_MODEL), fox_w_out[0], 0, tm_p, chunk_p)
    xs, ys3 = ffn(xs, os_.reshape(n_s, D_MODEL), fox_w_out[0], 0, tm_s, chunk_s)

    g1 = norm_mix[1][None, :]
    zc = jnp.zeros((bp, MLSTM_HEADS, MLSTM_DK, MLSTM_DV), F32)
    zn = jnp.zeros((bp, MLSTM_HEADS, MLSTM_DK), F32)
    zm = jnp.zeros((bp, MLSTM_HEADS), F32)
    xp, hp, cpf, npf, mpf = _mlstm_mixer(xp, yp3, bp, tp, g1, mlstm_w_in[0], mlstm_b_i[0], mlstm_b_f[0],
                                         mlstm_head_norm[0], zc, zn, zm, tm_p, min(256, tp), 2, 1, True)
    xs, hs, csf, nsf, msf = _mlstm_mixer(xs, ys3, bs, ts, g1, mlstm_w_in[0], mlstm_b_i[0], mlstm_b_f[0],
                                         mlstm_head_norm[0], state_C[0], state_n[0], state_m[0], tm_s, ts,
                                         MLSTM_HEADS // 2, 4 if bs % 4 == 0 else 1, False)
    xp, yp3 = ffn(xp, hp, mlstm_w_out[0], 1, tm_p, chunk_p)
    xs, ys3 = ffn(xs, hs, mlstm_w_out[0], 1, tm_s, chunk_s)
    yp = _residual_norm(xp, yp3, g_final, tm_p)
    ys = _residual_norm(xs, ys3, g_final, tm_s)

    hd = (FOX_HEADS, FOX_HEAD_DIM)
    time_major = lambda a: a.reshape(bp, FOX_HEADS, -1, tp).transpose(0, 3, 1, 2)
    return (yp.reshape(bp, tp, D_MODEL), ys.reshape(bs, ts, D_MODEL),
            time_major(kpt)[None], time_major(vpt)[None], jnp.swapaxes(lfpt, 1, 2)[None],
            ks.reshape(1, bs, ts, *hd), vs.reshape(1, bs, ts, *hd), lfs.reshape(1, bs, ts, FOX_HEADS),
            cpf[None], npf[None], mpf[None], csf[None], nsf[None], msf[None])
```

```python
import functools

import jax
import jax.numpy as jnp
from jax import lax
from jax.experimental import pallas as pl
from jax.experimental.pallas import tpu as pltpu

D_MODEL = 1024
FOX_HEADS = 16
FOX_HEAD_DIM = 64
MLSTM_HEADS = 8
MLSTM_DK = 64
MLSTM_DV = 128
MLSTM_QK = MLSTM_HEADS * MLSTM_DK
MOE_GROUPS = 4
MOE_EXPERTS_PER_GROUP = 8
MOE_EXPERTS = 32
MOE_HIDDEN = 256
RMS_EPS = 1e-6

LANES = 128
VMEM_LIMIT_BYTES = 56 * 1024 * 1024

F32 = jnp.float32
BF16 = jnp.bfloat16
NEG_INF = float("-inf")


def _params(*sem):
    return pltpu.CompilerParams(dimension_semantics=sem, vmem_limit_bytes=VMEM_LIMIT_BYTES)


def _dot(a, b):
    return jnp.dot(a, b, preferred_element_type=F32)


def _dot_nt(a, b):
    return lax.dot_general(a, b, (((1,), (1,)), ((), ())), preferred_element_type=F32)


def _rmsnorm(x, g):
    return x * lax.rsqrt(jnp.mean(x * x, axis=-1, keepdims=True) + RMS_EPS) * g


def _log_sigmoid(z):
    return jnp.minimum(z, 0.0) - jnp.log1p(jnp.exp(-jnp.abs(z)))


def _sigmoid(z):
    return 1.0 / (1.0 + jnp.exp(-z))


def _upper_tri(n):
    r = lax.broadcasted_iota(jnp.int32, (n, n), 0)
    c = lax.broadcasted_iota(jnp.int32, (n, n), 1)
    return (r <= c).astype(F32)


def _split3(x):
    hi = x.astype(BF16).astype(F32)
    mid = (x - hi).astype(BF16).astype(F32)
    lo = (x - hi - mid).astype(BF16).astype(F32)
    return hi, mid, lo


def _dot_f32_by_01(x, m01):
    return _dot(jnp.concatenate(_split3(x), axis=1).astype(BF16), jnp.concatenate([m01] * 3, axis=0).astype(BF16))


def _dot_01_by_f32(m01, x):
    return _dot(jnp.concatenate([m01] * 3, axis=1).astype(BF16), jnp.concatenate(_split3(x), axis=0).astype(BF16))


def _fox_proj_kernel(x_ref, g_ref, wq_ref, wk_ref, wv_ref, wf_ref, bf_ref,
                     q_ref, k_ref, v_ref, kb_ref, vb_ref, lf_ref, *, transposed):
    hb = _rmsnorm(x_ref[...], g_ref[...]).astype(BF16)
    q = _dot(hb, wq_ref[...])
    q_ref[...] = (q * (FOX_HEAD_DIM ** -0.5)).astype(q_ref.dtype)
    if transposed:
        k = _dot_nt(wk_ref[...], hb)
        v = _dot_nt(wv_ref[...], hb)
        z = _dot_nt(wf_ref[...], hb) + bf_ref[...]
        k_ref[0] = k
        kb_ref[0] = k.astype(BF16)
        v_ref[0] = v
        vb_ref[0] = v.astype(BF16)
        lf_ref[0] = _log_sigmoid(z)[:FOX_HEADS, :]
    else:
        k = _dot(hb, wk_ref[...])
        v = _dot(hb, wv_ref[...])
        z = _dot(hb, wf_ref[...]) + bf_ref[...]
        k_ref[...] = k
        kb_ref[...] = k.astype(BF16)
        v_ref[...] = v
        vb_ref[...] = v.astype(BF16)
        lf_ref[...] = _log_sigmoid(z)[:, :FOX_HEADS]


def _fox_proj(x, g, wq, wk, wv, wf, bf, tm, q_dtype, seq_len=None):
    n = x.shape[0]
    transposed = seq_len is not None
    row = lambda i: (i, 0)
    const = lambda i: (0, 0)
    if transposed:
        nt = seq_len // tm
        b = n // seq_len
        tmap = lambda i: (i // nt, 0, i % nt)
        kv_spec = pl.BlockSpec((1, D_MODEL, tm), tmap)
        lf_spec = pl.BlockSpec((1, FOX_HEADS, tm), tmap)
        kv_shape = (b, D_MODEL, seq_len)
        lf_shape = (b, FOX_HEADS, seq_len)
    else:
        kv_spec = pl.BlockSpec((tm, D_MODEL), row)
        lf_spec = pl.BlockSpec((tm, FOX_HEADS), row)
        kv_shape = (n, D_MODEL)
        lf_shape = (n, FOX_HEADS)
    return pl.pallas_call(
        functools.partial(_fox_proj_kernel, transposed=transposed),
        grid=(n // tm,),
        in_specs=[
            pl.BlockSpec((tm, D_MODEL), row),
            pl.BlockSpec((1, D_MODEL), const),
            pl.BlockSpec(wq.shape, const),
            pl.BlockSpec(wk.shape, const),
            pl.BlockSpec(wv.shape, const),
            pl.BlockSpec(wf.shape, const),
            pl.BlockSpec(bf.shape, const),
        ],
        out_specs=[pl.BlockSpec((tm, D_MODEL), row), kv_spec, kv_spec, kv_spec, kv_spec, lf_spec],
        out_shape=[
            jax.ShapeDtypeStruct((n, D_MODEL), q_dtype),
            jax.ShapeDtypeStruct(kv_shape, F32),
            jax.ShapeDtypeStruct(kv_shape, F32),
            jax.ShapeDtypeStruct(kv_shape, BF16),
            jax.ShapeDtypeStruct(kv_shape, BF16),
            jax.ShapeDtypeStruct(lf_shape, F32),
        ],
        compiler_params=_params("parallel"),
        name="fox_proj",
    )(x, g, wq, wk, wv, wf, bf)


def _mlstm_proj_kernel(x_ref, y3_ref, g_ref, w_ref, wk_ref, wg_ref, bg_ref,
                       x1_ref, q_ref, k_ref, v_ref, o_ref, gate_ref, *, transposed):
    x1 = x_ref[...] + _from_slabs(y3_ref)
    x1_ref[...] = x1
    hb = _rmsnorm(x1, g_ref[...]).astype(BF16)
    q = _dot(hb, w_ref[:, 0:MLSTM_QK])
    q_ref[...] = (q * (MLSTM_DK ** -0.5)).astype(BF16)
    if transposed:
        k_ref[0] = _dot_nt(wk_ref[...], hb).astype(BF16)
    else:
        k_ref[...] = _dot(hb, wk_ref[...]).astype(BF16)
    v_ref[...] = _dot(hb, w_ref[:, MLSTM_QK:MLSTM_QK + D_MODEL]).astype(BF16)
    o_ref[...] = _dot(hb, w_ref[:, MLSTM_QK + D_MODEL:MLSTM_QK + 2 * D_MODEL]).astype(BF16)
    z = _dot(hb, wg_ref[...]) + bg_ref[...]
    lane = lax.broadcasted_iota(jnp.int32, z.shape, 1)
    gates = jnp.where(lane < MLSTM_HEADS, z, _log_sigmoid(z))
    gate_ref[...] = gates[:, :2 * MLSTM_HEADS]


def _mlstm_proj(x, y3, g, w_qvo, wk, w_g, b_g, tm, seq_len=None):
    n = x.shape[0]
    transposed = seq_len is not None
    row = lambda i: (i, 0)
    const = lambda i: (0, 0)
    if transposed:
        nt = seq_len // tm
        k_spec = pl.BlockSpec((1, MLSTM_QK, tm), lambda i: (i // nt, 0, i % nt))
        k_shape = (n // seq_len, MLSTM_QK, seq_len)
    else:
        k_spec = pl.BlockSpec((tm, MLSTM_QK), row)
        k_shape = (n, MLSTM_QK)
    return pl.pallas_call(
        functools.partial(_mlstm_proj_kernel, transposed=transposed),
        grid=(n // tm,),
        in_specs=[
            pl.BlockSpec((tm, D_MODEL), row),
            pl.BlockSpec((tm, D_MODEL // LANES, LANES), lambda i: (i, 0, 0)),
            pl.BlockSpec((1, D_MODEL), const),
            pl.BlockSpec(w_qvo.shape, const),
            pl.BlockSpec(wk.shape, const),
            pl.BlockSpec((D_MODEL, LANES), const),
            pl.BlockSpec((1, LANES), const),
        ],
        out_specs=[
            pl.BlockSpec((tm, D_MODEL), row),
            pl.BlockSpec((tm, MLSTM_QK), row),
            k_spec,
            pl.BlockSpec((tm, D_MODEL), row),
            pl.BlockSpec((tm, D_MODEL), row),
            pl.BlockSpec((tm, 2 * MLSTM_HEADS), row),
        ],
        out_shape=[
            jax.ShapeDtypeStruct((n, D_MODEL), F32),
            jax.ShapeDtypeStruct((n, MLSTM_QK), BF16),
            jax.ShapeDtypeStruct(k_shape, BF16),
            jax.ShapeDtypeStruct((n, D_MODEL), BF16),
            jax.ShapeDtypeStruct((n, D_MODEL), BF16),
            jax.ShapeDtypeStruct((n, 2 * MLSTM_HEADS), F32),
        ],
        compiler_params=_params("parallel"),
        name="mlstm_proj",
    )(x, y3, g, w_qvo, wk, w_g, b_g)


def _cumsum_lanes_kernel(x_ref, o_ref):
    rows, t = x_ref.shape
    tri = _upper_tri(LANES)
    carry = jnp.zeros((rows, 1), F32)
    for j in range(t // LANES):
        blk = _dot_f32_by_01(x_ref[:, j * LANES:(j + 1) * LANES], tri) + carry
        o_ref[:, j * LANES:(j + 1) * LANES] = blk
        carry = blk[:, LANES - 1:LANES]


def _cumsum_lanes(x):
    return pl.pallas_call(
        _cumsum_lanes_kernel,
        out_shape=jax.ShapeDtypeStruct(x.shape, F32),
        compiler_params=_params(),
        name="cumsum_lanes",
    )(x)


def _fox_prompt_kernel(q_ref, kt_ref, vt_ref, c_ref, o_ref, *, blk):
    qi = pl.program_id(2)
    q2 = q_ref[0]
    lane = lax.broadcasted_iota(jnp.int32, (1, LANES), 1)
    head0 = lane < FOX_HEAD_DIM
    zero = jnp.zeros_like(q2)
    qh = (jnp.where(head0, q2, zero), jnp.where(head0, zero, q2))
    row = lax.broadcasted_iota(jnp.int32, (blk, blk), 0)
    col = lax.broadcasted_iota(jnp.int32, (blk, blk), 1)
    causal = col <= row

    def block(j, carry, masked):
        start = pl.multiple_of(j * blk, blk)
        kt = kt_ref[0, :, pl.ds(start, blk)]
        vt = vt_ref[0, :, pl.ds(start, blk)]
        out = []
        for h in range(2):
            m, l, acc = carry[h]
            cs = c_ref[0, 0, h:h + 1, pl.ds(start, blk)]
            s = _dot(qh[h], kt) - cs
            if masked:
                s = jnp.where(causal, s, NEG_INF)
            m_new = jnp.maximum(m, jnp.max(s, axis=1, keepdims=True))
            alpha = jnp.exp(m - m_new)
            p = jnp.exp(s - m_new)
            l = alpha * l + jnp.sum(p, axis=1, keepdims=True)
            acc = alpha * acc + _dot_nt(p.astype(BF16), vt)
            out.append((m_new, l, acc))
        return tuple(out)

    one = (jnp.full((blk, 1), NEG_INF, F32), jnp.zeros((blk, 1), F32), jnp.zeros((blk, LANES), F32))
    carry = lax.fori_loop(0, qi, functools.partial(block, masked=False), (one, one))
    (_, l0, acc0), (_, l1, acc1) = block(qi, carry, True)
    o_ref[0] = jnp.where(head0, acc0 / l0, acc1 / l1).astype(o_ref.dtype)


def _fox_prompt_attention(q, kt, vt, c_t, blk):
    b, t, _ = q.shape
    pairs = FOX_HEADS // 2
    return pl.pallas_call(
        functools.partial(_fox_prompt_kernel, blk=blk),
        grid=(b, pairs, t // blk),
        in_specs=[
            pl.BlockSpec((1, blk, LANES), lambda bi, p, qi: (bi, qi, p)),
            pl.BlockSpec((1, LANES, t), lambda bi, p, qi: (bi, p, 0)),
            pl.BlockSpec((1, LANES, t), lambda bi, p, qi: (bi, p, 0)),
            pl.BlockSpec((1, 1, 2, t), lambda bi, p, qi: (bi, p, 0, 0)),
        ],
        out_specs=pl.BlockSpec((1, blk, LANES), lambda bi, p, qi: (bi, qi, p)),
        out_shape=jax.ShapeDtypeStruct((b, t, D_MODEL), BF16),
        compiler_params=_params("parallel", "parallel", "arbitrary"),
        name="fox_prompt_attention",
    )(q, kt, vt, c_t)


def _fox_sample_kernel(pt_ref, q_ref, *refs, n_pages, page, pps):
    (kc_hbm, vc_hbm, lfc_hbm, kn_ref, vn_ref, lfn_ref, o_ref, kbuf, vbuf, lfbuf, sem,
     qbd_ref, m_ref, l_ref, acc_ref, carry_ref, kpad_ref, vpad_ref, lfpad_ref) = refs
    n_steps = n_pages // pps
    b = pl.program_id(0)
    p = pl.program_id(1)

    def page_copies(bi, step, slot):
        out = []
        for j in range(pps):
            pg = pt_ref[bi, step * pps + j]
            out.append(pltpu.make_async_copy(kc_hbm.at[pg], kbuf.at[slot, j], sem.at[slot]))
            out.append(pltpu.make_async_copy(vc_hbm.at[pg], vbuf.at[slot, j], sem.at[slot]))
            out.append(pltpu.make_async_copy(lfc_hbm.at[pg], lfbuf.at[slot, j], sem.at[slot]))
        return out

    def start_pages(bi, step, slot):
        for i, cp in enumerate(page_copies(bi, step, slot)):
            cp.start(priority=i % 2)
    n_q = q_ref.shape[1]
    chunk = 4 * FOX_HEAD_DIM
    n_chunks = D_MODEL // chunk
    rows = FOX_HEADS * n_q
    crow = 4 * n_q
    lane_c = lax.broadcasted_iota(jnp.int32, (1, chunk), 1)

    @pl.when(p == 0)
    def _init():
        q = q_ref[0]
        for c in range(n_chunks):
            qc = q[:, c * chunk:(c + 1) * chunk]
            pieces = [jnp.where(lane_c // FOX_HEAD_DIM == hl, qc, 0.0) for hl in range(4)]
            qbd_ref[c] = jnp.concatenate(pieces, axis=0).astype(BF16)
        m_ref[...] = jnp.full(m_ref.shape, NEG_INF, F32)
        l_ref[...] = jnp.zeros(l_ref.shape, F32)
        acc_ref[...] = jnp.zeros(acc_ref.shape, F32)
        carry_ref[...] = jnp.zeros(carry_ref.shape, F32)

    def process(get_kt, get_vt, lf_ts, mask):
        nb = len(lf_ts)
        local = _dot_f32_by_01(jnp.concatenate(lf_ts, axis=0), _upper_tri(page))
        carry = carry_ref[...]
        biases = []
        for j in range(nb):
            cum = local[j * FOX_HEADS:(j + 1) * FOX_HEADS] + carry
            carry = cum[:, page - 1:page]
            biases.append(jnp.concatenate(
                [jnp.broadcast_to(cum[h:h + 1, :], (n_q, page)) for h in range(FOX_HEADS)], axis=0))
        carry_ref[...] = carry
        bias = jnp.concatenate(biases, axis=1)
        s = jnp.concatenate(
            [_dot(qbd_ref[c], jnp.concatenate([get_kt(j, c) for j in range(nb)], axis=1)) for c in range(n_chunks)],
            axis=0) - bias
        if mask is not None:
            s = jnp.where(mask, s, NEG_INF)
        m_old = m_ref[...]
        m_new = jnp.maximum(m_old, jnp.max(s, axis=1, keepdims=True))
        alpha = jnp.exp(m_old - m_new)
        pr = jnp.exp(s - m_new)
        l_ref[...] = alpha * l_ref[...] + jnp.sum(pr, axis=1, keepdims=True)
        m_ref[...] = m_new
        pb = pr.astype(BF16)
        for c in range(n_chunks):
            sl = slice(c * crow, (c + 1) * crow)
            vt = jnp.concatenate([get_vt(j, c) for j in range(nb)], axis=1)
            acc_ref[sl, :] = alpha[sl] * acc_ref[sl, :] + _dot_nt(pb[sl], vt)

    for s in range(n_steps):
        @pl.when(p == s)
        def _past(s=s):
            slot = s % 2
            if s == 0:
                @pl.when(b == 0)
                def _():
                    start_pages(b, 0, 0)
            if s + 1 < n_steps:
                start_pages(b, s + 1, (s + 1) % 2)
            else:
                @pl.when(b + 1 < pl.num_programs(0))
                def _():
                    start_pages(b + 1, 0, 0)
            for cp in page_copies(b, s, slot):
                cp.wait()
            process(lambda j, c: kbuf[slot, j, c * chunk:(c + 1) * chunk, :].astype(BF16),
                    lambda j, c: vbuf[slot, j, c * chunk:(c + 1) * chunk, :].astype(BF16),
                    [lfbuf[slot, j] for j in range(pps)], None)

    @pl.when(p == n_steps)
    def _new():
        kpad_ref[...] = jnp.zeros(kpad_ref.shape, F32)
        vpad_ref[...] = jnp.zeros(vpad_ref.shape, F32)
        lfpad_ref[...] = jnp.zeros(lfpad_ref.shape, F32)
        kpad_ref[:, 0:n_q] = kn_ref[0]
        vpad_ref[:, 0:n_q] = vn_ref[0]
        lfpad_ref[:, 0:n_q] = lfn_ref[0]
        key = lax.broadcasted_iota(jnp.int32, (rows, page), 1)
        qry = lax.broadcasted_iota(jnp.int32, (rows, page), 0) % n_q
        process(lambda j, c: kpad_ref[c * chunk:(c + 1) * chunk, :].astype(BF16),
                lambda j, c: vpad_ref[c * chunk:(c + 1) * chunk, :].astype(BF16),
                [lfpad_ref[...]], key <= qry)
        inv_l = 1.0 / l_ref[...]
        for c in range(n_chunks):
            blk = acc_ref[c * crow:(c + 1) * crow, :] * inv_l[c * crow:(c + 1) * crow]
            out = jnp.zeros((n_q, chunk), F32)
            for hl in range(4):
                out = out + jnp.where(lane_c // FOX_HEAD_DIM == hl, blk[hl * n_q:(hl + 1) * n_q, :], 0.0)
            o_ref[0, :, c * chunk:(c + 1) * chunk] = out


def _fox_sample_attention(page_table, q, cache_kt, cache_vt, cache_lft, kt_new, vt_new, lft_new):
    b, n_q, _ = q.shape
    n_pages = page_table.shape[1]
    page = cache_kt.shape[2]
    rows = FOX_HEADS * n_q
    chunk = 4 * FOX_HEAD_DIM
    pps = max(d for d in (1, 2, 4, 8) if n_pages % (2 * d) == 0)

    per_b = lambda bi, p, pt: (bi, 0, 0)
    grid_spec = pltpu.PrefetchScalarGridSpec(
        num_scalar_prefetch=1,
        grid=(b, n_pages // pps + 1),
        in_specs=[
            pl.BlockSpec((1, n_q, D_MODEL), per_b),
            pl.BlockSpec(memory_space=pl.ANY),
            pl.BlockSpec(memory_space=pl.ANY),
            pl.BlockSpec(memory_space=pl.ANY),
            pl.BlockSpec((1, D_MODEL, n_q), per_b),
            pl.BlockSpec((1, D_MODEL, n_q), per_b),
            pl.BlockSpec((1, FOX_HEADS, n_q), per_b),
        ],
        out_specs=pl.BlockSpec((1, n_q, D_MODEL), per_b),
        scratch_shapes=[
            pltpu.VMEM((2, pps, D_MODEL, page), F32),
            pltpu.VMEM((2, pps, D_MODEL, page), F32),
            pltpu.VMEM((2, pps, FOX_HEADS, page), F32),
            pltpu.SemaphoreType.DMA((2,)),
            pltpu.VMEM((D_MODEL // chunk, 4 * n_q, chunk), BF16),
            pltpu.VMEM((rows, 1), F32),
            pltpu.VMEM((rows, 1), F32),
            pltpu.VMEM((rows, chunk), F32),
            pltpu.VMEM((FOX_HEADS, 1), F32),
            pltpu.VMEM((D_MODEL, page), F32),
            pltpu.VMEM((D_MODEL, page), F32),
            pltpu.VMEM((FOX_HEADS, page), F32),
        ],
    )
    return pl.pallas_call(
        functools.partial(_fox_sample_kernel, n_pages=n_pages, page=page, pps=pps),
        grid_spec=grid_spec,
        out_shape=jax.ShapeDtypeStruct((b, n_q, D_MODEL), F32),
        compiler_params=_params("arbitrary", "arbitrary"),
        name="fox_sample_attention",
    )(page_table, q, cache_kt, cache_vt, cache_lft, kt_new, vt_new, lft_new)


def _to_slabs(ref3, x):
    for j in range(D_MODEL // LANES):
        ref3[:, j, :] = x[:, j * LANES:(j + 1) * LANES]


def _from_slabs(ref3):
    return jnp.concatenate([ref3[:, j, :] for j in range(D_MODEL // LANES)], axis=1)


def _outproj_router_kernel(x_ref, o_ref, w_ref, g_ref, wr_ref, br_ref, xn_ref, h3_ref, route_ref):
    x_new = x_ref[...] + _dot(o_ref[...].astype(BF16), w_ref[...])
    xn_ref[...] = x_new
    h = _rmsnorm(x_new, g_ref[...])
    _to_slabs(h3_ref, h)
    w_r = wr_ref[...]
    h_hi, w_hi = h.astype(BF16), w_r.astype(BF16)
    h_lo = (h - h_hi.astype(F32)).astype(BF16)
    w_lo = (w_r - w_hi.astype(F32)).astype(BF16)
    logits = _dot(h_hi, w_hi) + (_dot(h_lo, w_hi) + _dot(h_hi, w_lo)) + br_ref[...]
    lane = lax.broadcasted_iota(jnp.int32, logits.shape, 1)
    big = jnp.int32(LANES)
    is_group = (lane >= MOE_EXPERTS) & (lane < MOE_EXPERTS + MOE_GROUPS)
    gl = jnp.where(is_group, logits, NEG_INF)
    g_max = jnp.max(gl, axis=1, keepdims=True)
    g_sel = jnp.min(jnp.where(gl == g_max, lane - MOE_EXPERTS, big), axis=1, keepdims=True)
    p_g = 1.0 / jnp.sum(jnp.where(is_group, jnp.exp(gl - g_max), 0.0), axis=1, keepdims=True)
    in_group = (lane < MOE_EXPERTS) & (lane // MOE_EXPERTS_PER_GROUP == g_sel)
    el = jnp.where(in_group, logits, NEG_INF)
    v1 = jnp.max(el, axis=1, keepdims=True)
    i1 = jnp.min(jnp.where(el == v1, lane, big), axis=1, keepdims=True)
    el2 = jnp.where(lane == i1, NEG_INF, el)
    v2 = jnp.max(el2, axis=1, keepdims=True)
    i2 = jnp.min(jnp.where(el2 == v2, lane, big), axis=1, keepdims=True)
    e2 = jnp.exp(v2 - v1)
    w1 = p_g / (1.0 + e2)
    w2 = p_g * e2 / (1.0 + e2)
    route_ref[...] = jnp.where(lane == 0, i1.astype(F32),
                               jnp.where(lane == 1, i2.astype(F32),
                                         jnp.where(lane == 2, w1, jnp.where(lane == 3, w2, 0.0))))


def _outproj_router(x, o, w_out, g_ffn, w_r, b_r, tm):
    n = x.shape[0]
    row = lambda i: (i, 0)
    const = lambda i: (0, 0)
    return pl.pallas_call(
        _outproj_router_kernel,
        grid=(n // tm,),
        in_specs=[
            pl.BlockSpec((tm, D_MODEL), row),
            pl.BlockSpec((tm, D_MODEL), row),
            pl.BlockSpec((D_MODEL, D_MODEL), const),
            pl.BlockSpec((1, D_MODEL), const),
            pl.BlockSpec((D_MODEL, LANES), const),
            pl.BlockSpec((1, LANES), const),
        ],
        out_specs=[
            pl.BlockSpec((tm, D_MODEL), row),
            pl.BlockSpec((tm, D_MODEL // LANES, LANES), lambda i: (i, 0, 0)),
            pl.BlockSpec((tm, LANES), row),
        ],
        out_shape=[
            jax.ShapeDtypeStruct((n, D_MODEL), F32),
            jax.ShapeDtypeStruct((n, D_MODEL // LANES, LANES), F32),
            jax.ShapeDtypeStruct((n, LANES), F32),
        ],
        compiler_params=_params("parallel"),
        name="outproj_router",
    )(x, o, w_out, g_ffn, w_r, b_r)


MOE_TILE = 256
MOE_TILE_PITCH = MOE_TILE + 1
MOE_UNROLL = 8


def _moe_kernel(off_ref, tok_ref, wt_ref, h3_ref, wg_ref, wu_ref, wd_ref, y3_ref, xg_ref, yg_ref):
    c = pl.program_id(0)
    e = pl.program_id(1)
    n_col = D_MODEL // LANES
    pitch = MOE_TILE_PITCH

    @pl.when((c == 0) & (e == 0))
    def _():
        xg_ref[...] = jnp.zeros(xg_ref.shape, F32)

    @pl.when(e == 0)
    def _():
        y3_ref[...] = jnp.zeros(y3_ref.shape, F32)

    base = off_ref[c * (MOE_EXPERTS + 1) + e]
    n = off_ref[c * (MOE_EXPERTS + 1) + e + 1] - base

    def gather_row(i, start):
        xg_ref[pl.ds(i, n_col, stride=pitch), :] = h3_ref[tok_ref[start + i]]

    def sub_tile(t, carry):
        start = base + t * MOE_TILE
        cnt = jnp.minimum(MOE_TILE, n - t * MOE_TILE)
        full = cnt // MOE_UNROLL

        def gather_group(g, _):
            for u in range(MOE_UNROLL):
                gather_row(g * MOE_UNROLL + u, start)
            return 0

        def gather_one(i, _):
            gather_row(i, start)
            return 0

        lax.fori_loop(0, full, gather_group, 0)
        lax.fori_loop(full * MOE_UNROLL, cnt, gather_one, 0)
        x = jnp.concatenate([xg_ref[j * pitch:j * pitch + MOE_TILE, :] for j in range(n_col)], axis=1).astype(BF16)
        hg = _dot(x, wg_ref[0, 0].astype(BF16))
        hu = _dot(x, wu_ref[0, 0].astype(BF16))
        y = _dot((hg * _sigmoid(hg) * hu).astype(BF16), wd_ref[0, 0].astype(BF16))
        for j in range(n_col):
            yg_ref[j * pitch:j * pitch + MOE_TILE, :] = y[:, j * LANES:(j + 1) * LANES]

        def scatter_group(g, _):
            rows, vals = [], []
            for u in range(MOE_UNROLL):
                i = g * MOE_UNROLL + u
                r = tok_ref[start + i]
                rows.append(r)
                vals.append(y3_ref[r] + wt_ref[start + i] * yg_ref[pl.ds(i, n_col, stride=pitch), :])
            for r, v in zip(rows, vals):
                y3_ref[r] = v
            return 0

        def scatter_one(i, _):
            r = tok_ref[start + i]
            y3_ref[r] = y3_ref[r] + wt_ref[start + i] * yg_ref[pl.ds(i, n_col, stride=pitch), :]
            return 0

        lax.fori_loop(0, full, scatter_group, 0)
        lax.fori_loop(full * MOE_UNROLL, cnt, scatter_one, 0)
        return carry

    lax.fori_loop(0, (n + MOE_TILE - 1) // MOE_TILE, sub_tile, 0)


def _moe(h3, route, w_gate, w_up, w_down, layer, chunk):
    n = h3.shape[0]
    n_chunks = n // chunk
    pairs = 2 * chunk
    expert = route[:, 0:2].astype(jnp.int32).reshape(n_chunks, pairs)
    weight = route[:, 2:4].reshape(n_chunks, pairs)
    order = jnp.argsort(expert, axis=1, stable=True)
    expert_sorted = jnp.take_along_axis(expert, order, axis=1)
    tok = (order // 2).astype(jnp.int32)
    wt = jnp.take_along_axis(weight, order, axis=1)
    bounds = jnp.arange(MOE_EXPERTS + 1, dtype=jnp.int32)
    off = jnp.sum(expert_sorted[:, None, :] < bounds[None, :, None], axis=2).astype(jnp.int32)
    off = (off + pairs * jnp.arange(n_chunks, dtype=jnp.int32)[:, None]).reshape(-1)
    tok = tok.reshape(-1)
    wt = wt.reshape(-1)

    n_col = D_MODEL // LANES
    tok_map = lambda c, e, *_: (c, 0, 0)
    grid_spec = pltpu.PrefetchScalarGridSpec(
        num_scalar_prefetch=3,
        grid=(n_chunks, MOE_EXPERTS),
        in_specs=[
            pl.BlockSpec((chunk, n_col, LANES), tok_map, pipeline_mode=pl.Buffered(1)),
            pl.BlockSpec((1, 1, D_MODEL, MOE_HIDDEN), lambda c, e, *_: (layer, e, 0, 0)),
            pl.BlockSpec((1, 1, D_MODEL, MOE_HIDDEN), lambda c, e, *_: (layer, e, 0, 0)),
            pl.BlockSpec((1, 1, MOE_HIDDEN, D_MODEL), lambda c, e, *_: (layer, e, 0, 0)),
        ],
        out_specs=pl.BlockSpec((chunk, n_col, LANES), tok_map, pipeline_mode=pl.Buffered(1)),
        scratch_shapes=[
            pltpu.VMEM((n_col * MOE_TILE_PITCH, LANES), F32),
            pltpu.VMEM((n_col * MOE_TILE_PITCH, LANES), F32),
        ],
    )
    return pl.pallas_call(
        _moe_kernel,
        grid_spec=grid_spec,
        out_shape=jax.ShapeDtypeStruct((n, n_col, LANES), F32),
        compiler_params=_params("arbitrary", "arbitrary"),
        name="moe_experts",
    )(off, tok, wt, h3, w_gate, w_up, w_down)


def _residual_norm_kernel(x_ref, y3_ref, g_ref, o_ref):
    o_ref[...] = _rmsnorm(x_ref[...] + _from_slabs(y3_ref), g_ref[...])


def _residual_norm(x, y3, g, tm):
    n = x.shape[0]
    row = lambda i: (i, 0)
    return pl.pallas_call(
        _residual_norm_kernel,
        grid=(n // tm,),
        in_specs=[
            pl.BlockSpec((tm, D_MODEL), row),
            pl.BlockSpec((tm, D_MODEL // LANES, LANES), lambda i: (i, 0, 0)),
            pl.BlockSpec((1, D_MODEL), lambda i: (0, 0)),
        ],
        out_specs=pl.BlockSpec((tm, D_MODEL), row),
        out_shape=jax.ShapeDtypeStruct((n, D_MODEL), F32),
        compiler_params=_params("parallel"),
        name="residual_norm",
    )(x, y3, g)


def _mlstm_kernel(q_ref, kt_ref, v_ref, o_ref, gc_ref, gr_ref, hn_ref, c0_ref, n0_ref, m0_ref,
                  h_ref, c_out_ref, n_out_ref, m_out_ref, c_s, n_s, m_s, *, chunk, pps, bb):
    ci = pl.program_id(2)

    @pl.when(ci == 0)
    def _():
        c_s[...] = c0_ref[...]
        n_s[...] = n0_ref[...]
        m_s[...] = m0_ref[...]

    lane = lax.broadcasted_iota(jnp.int32, (1, LANES), 1)
    srow = lax.broadcasted_iota(jnp.int32, (LANES, 1), 0)
    row = lax.broadcasted_iota(jnp.int32, (chunk, chunk), 0)
    col = lax.broadcasted_iota(jnp.int32, (chunk, chunk), 1)
    causal = col <= row
    lower = causal.astype(F32)
    upper = (row <= col).astype(F32)
    units = [(bi, pp) for bi in range(bb) for pp in range(pps)]
    old_state = {u: (c_s[u], n_s[u], m_s[u]) for u in units}
    new_state = {}
    for bi, pp in units:
        q2 = q_ref[bi, :, pp * LANES:(pp + 1) * LANES]
        kt2 = kt_ref[bi, pp * LANES:(pp + 1) * LANES, :]
        gc = gc_ref[bi, pp]
        gr = gr_ref[bi, pp]
        c_prev, n_prev, m_prev_row = old_state[(bi, pp)]
        b_cols = _dot_01_by_f32(lower, gc)
        b_rows = _dot_f32_by_01(gr, upper)
        c_prev_b = c_prev.astype(BF16)
        kt2f = kt2.astype(F32)
        c_new = jnp.zeros((LANES, MLSTM_DV), F32)
        n_new = jnp.zeros((1, LANES), F32)
        scale_rows = jnp.zeros((LANES, 1), F32)
        scale_lanes = jnp.zeros((1, LANES), F32)
        m_lanes = jnp.zeros((1, LANES), F32)
        for hl in range(2):
            in_head = lane // MLSTM_DK == hl
            in_head_rows = srow // MLSTM_DK == hl
            vsl = slice((2 * pp + hl) * MLSTM_DV, (2 * pp + hl + 1) * MLSTM_DV)
            b_col = b_cols[:, 2 + hl:3 + hl]
            b_row = b_rows[2 + hl:3 + hl, :]
            i_row = gr[hl:hl + 1, :]
            m_prev = m_prev_row[:, hl * MLSTM_DK:hl * MLSTM_DK + 1]
            d = jnp.where(causal, b_col - b_row + i_row, NEG_INF)
            inter = b_col + m_prev
            m_t = jnp.maximum(inter, jnp.max(d, axis=1, keepdims=True))
            w_inter = jnp.exp(inter - m_t)
            qh = jnp.where(in_head, q2, jnp.zeros_like(q2))
            s = jnp.exp(d - m_t) * _dot(qh, kt2)
            vh = v_ref[bi, :, vsl]
            num = _dot(s.astype(BF16), vh) + w_inter * _dot(qh, c_prev_b)
            qn = jnp.sum(qh.astype(F32) * n_prev, axis=1, keepdims=True)
            den = jnp.sum(s, axis=1, keepdims=True) + w_inter * qn
            hval = num / jnp.maximum(jnp.abs(den), jnp.exp(-m_t))
            hnorm = hval * lax.rsqrt(jnp.mean(hval * hval, axis=1, keepdims=True) + RMS_EPS)
            hnorm = hnorm * hn_ref[:, vsl]
            og = o_ref[bi, :, vsl].astype(F32)
            h_ref[bi, :, vsl] = (_sigmoid(og) * hnorm).astype(h_ref.dtype)
            b_last = b_row[:, chunk - 1:chunk]
            decay = b_last - b_row + i_row
            m_new = jnp.maximum(b_last + m_prev, jnp.max(decay, axis=1, keepdims=True))
            w_k = jnp.exp(decay - m_new)
            carry_scale = jnp.exp(b_last + m_prev - m_new)
            kth = jnp.where(in_head_rows, kt2f, 0.0)
            c_new = c_new + _dot((kth * w_k).astype(BF16), vh)
            n_new = n_new + _dot_nt(jnp.concatenate(_split3(w_k), axis=1).astype(BF16),
                                    jnp.concatenate([kth] * 3, axis=1).astype(BF16))
            scale_rows = jnp.where(in_head_rows, carry_scale, scale_rows)
            scale_lanes = jnp.where(in_head, carry_scale, scale_lanes)
            m_lanes = jnp.where(in_head, m_new, m_lanes)
        new_state[(bi, pp)] = (scale_rows * c_prev + c_new, scale_lanes * n_prev + n_new, m_lanes)
    for u in units:
        c_s[u], n_s[u], m_s[u] = new_state[u]

    @pl.when(ci == pl.num_programs(2) - 1)
    def _():
        c_out_ref[...] = c_s[...]
        n_out_ref[...] = n_s[...]
        m_out_ref[...] = m_s[...]


def _mlstm(q, kt, v, o, gcol, grow, head_norm, c0, n0, m0, chunk, pps, bb):
    b, t, _ = q.shape
    pairs = MLSTM_HEADS // 2
    state = lambda bi, p, ci: (bi, p, 0, 0)
    seq = lambda bi, p, ci: (bi, ci, p)
    return pl.pallas_call(
        functools.partial(_mlstm_kernel, chunk=chunk, pps=pps, bb=bb),
        grid=(b // bb, pairs // pps, t // chunk),
        in_specs=[
            pl.BlockSpec((bb, chunk, pps * LANES), seq),
            pl.BlockSpec((bb, pps * LANES, chunk), lambda bi, p, ci: (bi, p, ci)),
            pl.BlockSpec((bb, chunk, pps * 2 * MLSTM_DV), seq),
            pl.BlockSpec((bb, chunk, pps * 2 * MLSTM_DV), seq),
            pl.BlockSpec((bb, pps, chunk, 4), lambda bi, p, ci: (bi, p, ci, 0)),
            pl.BlockSpec((bb, pps, 4, chunk), lambda bi, p, ci: (bi, p, 0, ci)),
            pl.BlockSpec((1, pps * 2 * MLSTM_DV), lambda bi, p, ci: (0, p)),
            pl.BlockSpec((bb, pps, 2 * MLSTM_DK, MLSTM_DV), state),
            pl.BlockSpec((bb, pps, 1, 2 * MLSTM_DK), state),
            pl.BlockSpec((bb, pps, 1, 2 * MLSTM_DK), state),
        ],
        out_specs=[
            pl.BlockSpec((bb, chunk, pps * 2 * MLSTM_DV), seq),
            pl.BlockSpec((bb, pps, 2 * MLSTM_DK, MLSTM_DV), state),
            pl.BlockSpec((bb, pps, 1, 2 * MLSTM_DK), state),
            pl.BlockSpec((bb, pps, 1, 2 * MLSTM_DK), state),
        ],
        out_shape=[
            jax.ShapeDtypeStruct((b, t, D_MODEL), BF16),
            jax.ShapeDtypeStruct((b, pairs, 2 * MLSTM_DK, MLSTM_DV), F32),
            jax.ShapeDtypeStruct((b, pairs, 1, 2 * MLSTM_DK), F32),
            jax.ShapeDtypeStruct((b, pairs, 1, 2 * MLSTM_DK), F32),
        ],
        scratch_shapes=[
            pltpu.VMEM((bb, pps, 2 * MLSTM_DK, MLSTM_DV), F32),
            pltpu.VMEM((bb, pps, 1, 2 * MLSTM_DK), F32),
            pltpu.VMEM((bb, pps, 1, 2 * MLSTM_DK), F32),
        ],
        compiler_params=_params("parallel", "parallel", "arbitrary"),
        name="mlstm_chunkwise",
    )(q, kt, v, o, gcol, grow, head_norm, c0, n0, m0)


def _pad_lanes(w, width=LANES):
    return jnp.pad(w, ((0, 0), (0, width - w.shape[1])))


def _pad_rows(w, height=LANES):
    return jnp.pad(w, ((0, height - w.shape[0]), (0, 0)))


def _router_weights(w_group, b_group, w_router, b_router):
    w = _pad_lanes(jnp.concatenate([w_router, w_group], axis=1).astype(F32))
    b = _pad_lanes(jnp.concatenate([b_router, b_group])[None, :].astype(F32))
    return w, b


def _mlstm_mixer(x2, y3, b, t, g, w_in, b_i, b_f, head_norm, c0, n0, m0, tm, chunk, pps, bb, transposed):
    pairs = MLSTM_HEADS // 2
    v_end = 2 * MLSTM_QK + 2 * D_MODEL
    w_qvo = jnp.concatenate([w_in[:, :MLSTM_QK], w_in[:, 2 * MLSTM_QK:v_end]], axis=1).astype(BF16)
    w_k = w_in[:, MLSTM_QK:2 * MLSTM_QK]
    w_g = _pad_lanes(w_in[:, v_end:]).astype(BF16)
    b_g = _pad_lanes(jnp.concatenate([b_i, b_f])[None, :].astype(F32))
    if transposed:
        x1, q, kt, v, o, gates = _mlstm_proj(x2, y3, g, w_qvo, w_k.T.astype(BF16), w_g, b_g, tm, seq_len=t)
    else:
        x1, q, k, v, o, gates = _mlstm_proj(x2, y3, g, w_qvo, w_k.astype(BF16), w_g, b_g, tm)
        kt = jnp.swapaxes(k.reshape(b, t, MLSTM_QK), 1, 2)
    q = q.reshape(b, t, MLSTM_QK)
    v = v.reshape(b, t, D_MODEL)
    o = o.reshape(b, t, D_MODEL)
    gates = gates.reshape(b, t, 2, pairs, 2)
    gcol = gates.transpose(0, 3, 1, 2, 4).reshape(b, pairs, t, 4)
    grow = jnp.swapaxes(gcol, 2, 3)
    c0 = c0.reshape(b, pairs, 2 * MLSTM_DK, MLSTM_DV)
    n0 = n0.reshape(b, pairs, 1, 2 * MLSTM_DK)
    m0 = jnp.repeat(m0.reshape(b, pairs, 1, 2), MLSTM_DK, axis=3)
    h, c_f, n_f, m_f = _mlstm(q, kt, v, o, gcol, grow, head_norm.reshape(1, D_MODEL), c0, n0, m0, chunk, pps, bb)
    c_f = c_f.reshape(b, MLSTM_HEADS, MLSTM_DK, MLSTM_DV)
    n_f = n_f.reshape(b, MLSTM_HEADS, MLSTM_DK)
    m_f = m_f.reshape(b, MLSTM_HEADS, MLSTM_DK)[:, :, 0]
    return x1, h.reshape(b * t, D_MODEL), c_f, n_f, m_f


def kernel(x_prompt, x_sample, cache_k, cache_v, cache_logf, page_table, state_C, state_n, state_m, norm_mix, norm_ffn, norm_final, fox_w_in, fox_b_f, fox_w_out, mlstm_w_in, mlstm_b_i, mlstm_b_f, mlstm_head_norm, mlstm_w_out, moe_w_group, moe_b_group, moe_w_router, moe_b_router, moe_w_gate, moe_w_up, moe_w_down):
    bp, tp, _ = x_prompt.shape
    bs, ts, _ = x_sample.shape
    n_p, n_s = bp * tp, bs * ts
    tm_p, tm_s = min(512, tp), min(512, n_s)
    xp = x_prompt.reshape(n_p, D_MODEL)
    xs = x_sample.reshape(n_s, D_MODEL)
    g_final = norm_final[None, :]
    chunk_p, chunk_s = min(4096, n_p), min(4096, n_s)

    def ffn(x, o, w_out, layer, tm, chunk):
        w_r, b_r = _router_weights(moe_w_group[layer], moe_b_group[layer], moe_w_router[layer], moe_b_router[layer])
        x_new, h3, route = _outproj_router(x, o, w_out.astype(BF16), norm_ffn[layer][None, :], w_r, b_r, tm)
        return x_new, _moe(h3, route, moe_w_gate, moe_w_up, moe_w_down, layer, chunk)

    g0 = norm_mix[0][None, :]
    w_in = fox_w_in[0]
    wq, wk, wv, wf = (w_in[:, :D_MODEL], w_in[:, D_MODEL:2 * D_MODEL], w_in[:, 2 * D_MODEL:3 * D_MODEL],
                      w_in[:, 3 * D_MODEL:])
    b_f = fox_b_f[0].astype(F32)
    wq_b = wq.astype(BF16)
    qp, kpt, vpt, kptb, vptb, lfpt = _fox_proj(
        xp, g0, wq_b, wk.T.astype(BF16), wv.T.astype(BF16), _pad_rows(wf.T).astype(BF16),
        _pad_rows(b_f[:, None]), tm_p, BF16, seq_len=tp)
    qs, ks, vs, _, _, lfs = _fox_proj(
        xs, g0, wq_b, wk.astype(BF16), wv.astype(BF16), _pad_lanes(wf).astype(BF16),
        _pad_lanes(b_f[None, :]), tm_s, F32)

    c_t = _cumsum_lanes(lfpt.reshape(bp * FOX_HEADS, tp)).reshape(bp, FOX_HEADS // 2, 2, tp)
    op = _fox_prompt_attention(qp.reshape(bp, tp, D_MODEL), kptb, vptb, c_t, min(512, tp))
    n_pool, page = cache_k.shape[1], cache_k.shape[2]
    feat_major = lambda c: c.transpose(0, 2, 3, 1).reshape(n_pool, D_MODEL, page)
    seq_last = lambda a: jnp.swapaxes(a.reshape(bs, ts, -1), 1, 2)
    os_ = _fox_sample_attention(page_table, qs.reshape(bs, ts, D_MODEL), feat_major(cache_k[0]), feat_major(cache_v[0]),
                                jnp.swapaxes(cache_logf[0], 1, 2), seq_last(ks), seq_last(vs), seq_last(lfs))
    xp, yp3 = ffn(xp, op.reshape(n_p, D_MODEL), fox_w_out[0], 0, tm_p, chunk_p)
    xs, ys3 = ffn(xs, os_.reshape(n_s, D_MODEL), fox_w_out[0], 0, tm_s, chunk_s)

    g1 = norm_mix[1][None, :]
    zc = jnp.zeros((bp, MLSTM_HEADS, MLSTM_DK, MLSTM_DV), F32)
    zn = jnp.zeros((bp, MLSTM_HEADS, MLSTM_DK), F32)
    zm = jnp.zeros((bp, MLSTM_HEADS), F32)
    xp, hp, cpf, npf, mpf = _mlstm_mixer(xp, yp3, bp, tp, g1, mlstm_w_in[0], mlstm_b_i[0], mlstm_b_f[0],
                                         mlstm_head_norm[0], zc, zn, zm, tm_p, min(256, tp), 2, 1, True)
    xs, hs, csf, nsf, msf = _mlstm_mixer(xs, ys3, bs, ts, g1, mlstm_w_in[0], mlstm_b_i[0], mlstm_b_f[0],
                                         mlstm_head_norm[0], state_C[0], state_n[0], state_m[0], tm_s, ts,
                                         MLSTM_HEADS // 2, 4 if bs % 4 == 0 else 1, False)
    xp, yp3 = ffn(xp, hp, mlstm_w_out[0], 1, tm_p, chunk_p)
    xs, ys3 = ffn(xs, hs, mlstm_w_out[0], 1, tm_s, chunk_s)
    yp = _residual_norm(xp, yp3, g_final, tm_p)
    ys = _residual_norm(xs, ys3, g_final, tm_s)

    hd = (FOX_HEADS, FOX_HEAD_DIM)
    time_major = lambda a: a.reshape(bp, FOX_HEADS, -1, tp).transpose(0, 3, 1, 2)
    return (yp.reshape(bp, tp, D_MODEL), ys.reshape(bs, ts, D_MODEL),
            time_major(kpt)[None], time_major(vpt)[None], jnp.swapaxes(lfpt, 1, 2)[None],
            ks.reshape(1, bs, ts, *hd), vs.reshape(1, bs, ts, *hd), lfs.reshape(1, bs, ts, FOX_HEADS),
            cpf[None], npf[None], mpf[None], csf[None], nsf[None], msf[None])
```

```python
import functools

import jax
import jax.numpy as jnp
from jax import lax
from jax.experimental import pallas as pl
from jax.experimental.pallas import tpu as pltpu

D_MODEL = 1024
FOX_HEADS = 16
FOX_HEAD_DIM = 64
MLSTM_HEADS = 8
MLSTM_DK = 64
MLSTM_DV = 128
MLSTM_QK = MLSTM_HEADS * MLSTM_DK
MOE_GROUPS = 4
MOE_EXPERTS_PER_GROUP = 8
MOE_EXPERTS = 32
MOE_HIDDEN = 256
RMS_EPS = 1e-6

LANES = 128
VMEM_LIMIT_BYTES = 56 * 1024 * 1024

F32 = jnp.float32
BF16 = jnp.bfloat16
NEG_INF = float("-inf")


def _params(*sem):
    return pltpu.CompilerParams(dimension_semantics=sem, vmem_limit_bytes=VMEM_LIMIT_BYTES)


def _dot(a, b):
    return jnp.dot(a, b, preferred_element_type=F32)


def _dot_nt(a, b):
    return lax.dot_general(a, b, (((1,), (1,)), ((), ())), preferred_element_type=F32)


def _rmsnorm(x, g):
    return x * lax.rsqrt(jnp.mean(x * x, axis=-1, keepdims=True) + RMS_EPS) * g


def _log_sigmoid(z):
    return jnp.minimum(z, 0.0) - jnp.log1p(jnp.exp(-jnp.abs(z)))


def _sigmoid(z):
    return 1.0 / (1.0 + jnp.exp(-z))


def _upper_tri(n):
    r = lax.broadcasted_iota(jnp.int32, (n, n), 0)
    c = lax.broadcasted_iota(jnp.int32, (n, n), 1)
    return (r <= c).astype(F32)


def _split3(x):
    hi = x.astype(BF16).astype(F32)
    mid = (x - hi).astype(BF16).astype(F32)
    lo = (x - hi - mid).astype(BF16).astype(F32)
    return hi, mid, lo


def _dot_f32_by_01(x, m01):
    return _dot(jnp.concatenate(_split3(x), axis=1).astype(BF16), jnp.concatenate([m01] * 3, axis=0).astype(BF16))


def _dot_01_by_f32(m01, x):
    return _dot(jnp.concatenate([m01] * 3, axis=1).astype(BF16), jnp.concatenate(_split3(x), axis=0).astype(BF16))


def _fox_proj_kernel(x_ref, g_ref, wq_ref, wk_ref, wv_ref, wf_ref, bf_ref,
                     q_ref, k_ref, v_ref, kb_ref, vb_ref, lf_ref, *, transposed):
    hb = _rmsnorm(x_ref[...], g_ref[...]).astype(BF16)
    q = _dot(hb, wq_ref[...])
    q_ref[...] = (q * (FOX_HEAD_DIM ** -0.5)).astype(q_ref.dtype)
    if transposed:
        k = _dot_nt(wk_ref[...], hb)
        v = _dot_nt(wv_ref[...], hb)
        z = _dot_nt(wf_ref[...], hb) + bf_ref[...]
        k_ref[0] = k
        kb_ref[0] = k.astype(BF16)
        v_ref[0] = v
        vb_ref[0] = v.astype(BF16)
        lf_ref[0] = _log_sigmoid(z)[:FOX_HEADS, :]
    else:
        k = _dot(hb, wk_ref[...])
        v = _dot(hb, wv_ref[...])
        z = _dot(hb, wf_ref[...]) + bf_ref[...]
        k_ref[...] = k
        kb_ref[...] = k.astype(BF16)
        v_ref[...] = v
        vb_ref[...] = v.astype(BF16)
        lf_ref[...] = _log_sigmoid(z)[:, :FOX_HEADS]


def _fox_proj(x, g, wq, wk, wv, wf, bf, tm, q_dtype, seq_len=None):
    n = x.shape[0]
    transposed = seq_len is not None
    row = lambda i: (i, 0)
    const = lambda i: (0, 0)
    if transposed:
        nt = seq_len // tm
        b = n // seq_len
        tmap = lambda i: (i // nt, 0, i % nt)
        kv_spec = pl.BlockSpec((1, D_MODEL, tm), tmap)
        lf_spec = pl.BlockSpec((1, FOX_HEADS, tm), tmap)
        kv_shape = (b, D_MODEL, seq_len)
        lf_shape = (b, FOX_HEADS, seq_len)
    else:
        kv_spec = pl.BlockSpec((tm, D_MODEL), row)
        lf_spec = pl.BlockSpec((tm, FOX_HEADS), row)
        kv_shape = (n, D_MODEL)
        lf_shape = (n, FOX_HEADS)
    return pl.pallas_call(
        functools.partial(_fox_proj_kernel, transposed=transposed),
        grid=(n // tm,),
        in_specs=[
            pl.BlockSpec((tm, D_MODEL), row),
            pl.BlockSpec((1, D_MODEL), const),
            pl.BlockSpec(wq.shape, const),
            pl.BlockSpec(wk.shape, const),
            pl.BlockSpec(wv.shape, const),
            pl.BlockSpec(wf.shape, const),
            pl.BlockSpec(bf.shape, const),
        ],
        out_specs=[pl.BlockSpec((tm, D_MODEL), row), kv_spec, kv_spec, kv_spec, kv_spec, lf_spec],
        out_shape=[
            jax.ShapeDtypeStruct((n, D_MODEL), q_dtype),
            jax.ShapeDtypeStruct(kv_shape, F32),
            jax.ShapeDtypeStruct(kv_shape, F32),
            jax.ShapeDtypeStruct(kv_shape, BF16),
            jax.ShapeDtypeStruct(kv_shape, BF16),
            jax.ShapeDtypeStruct(lf_shape, F32),
        ],
        compiler_params=_params("parallel"),
        name="fox_proj",
    )(x, g, wq, wk, wv, wf, bf)


def _mlstm_proj_kernel(x_ref, y3_ref, g_ref, w_ref, wk_ref, wg_ref, bg_ref,
                       x1_ref, q_ref, k_ref, v_ref, o_ref, gate_ref, *, transposed):
    x1 = x_ref[...] + _from_slabs(y3_ref)
    x1_ref[...] = x1
    hb = _rmsnorm(x1, g_ref[...]).astype(BF16)
    q = _dot(hb, w_ref[:, 0:MLSTM_QK])
    q_ref[...] = (q * (MLSTM_DK ** -0.5)).astype(BF16)
    if transposed:
        k_ref[0] = _dot_nt(wk_ref[...], hb).astype(BF16)
    else:
        k_ref[...] = _dot(hb, wk_ref[...]).astype(BF16)
    v_ref[...] = _dot(hb, w_ref[:, MLSTM_QK:MLSTM_QK + D_MODEL]).astype(BF16)
    o_ref[...] = _dot(hb, w_ref[:, MLSTM_QK + D_MODEL:MLSTM_QK + 2 * D_MODEL]).astype(BF16)
    z = _dot(hb, wg_ref[...]) + bg_ref[...]
    lane = lax.broadcasted_iota(jnp.int32, z.shape, 1)
    gates = jnp.where(lane < MLSTM_HEADS, z, _log_sigmoid(z))
    gate_ref[...] = gates[:, :2 * MLSTM_HEADS]


def _mlstm_proj(x, y3, g, w_qvo, wk, w_g, b_g, tm, seq_len=None):
    n = x.shape[0]
    transposed = seq_len is not None
    row = lambda i: (i, 0)
    const = lambda i: (0, 0)
    if transposed:
        nt = seq_len // tm
        k_spec = pl.BlockSpec((1, MLSTM_QK, tm), lambda i: (i // nt, 0, i % nt))
        k_shape = (n // seq_len, MLSTM_QK, seq_len)
    else:
        k_spec = pl.BlockSpec((tm, MLSTM_QK), row)
        k_shape = (n, MLSTM_QK)
    return pl.pallas_call(
        functools.partial(_mlstm_proj_kernel, transposed=transposed),
        grid=(n // tm,),
        in_specs=[
            pl.BlockSpec((tm, D_MODEL), row),
            pl.BlockSpec((tm, D_MODEL // LANES, LANES), lambda i: (i, 0, 0)),
            pl.BlockSpec((1, D_MODEL), const),
            pl.BlockSpec(w_qvo.shape, const),
            pl.BlockSpec(wk.shape, const),
            pl.BlockSpec((D_MODEL, LANES), const),
            pl.BlockSpec((1, LANES), const),
        ],
        out_specs=[
            pl.BlockSpec((tm, D_MODEL), row),
            pl.BlockSpec((tm, MLSTM_QK), row),
            k_spec,
            pl.BlockSpec((tm, D_MODEL), row),
            pl.BlockSpec((tm, D_MODEL), row),
            pl.BlockSpec((tm, 2 * MLSTM_HEADS), row),
        ],
        out_shape=[
            jax.ShapeDtypeStruct((n, D_MODEL), F32),
            jax.ShapeDtypeStruct((n, MLSTM_QK), BF16),
            jax.ShapeDtypeStruct(k_shape, BF16),
            jax.ShapeDtypeStruct((n, D_MODEL), BF16),
            jax.ShapeDtypeStruct((n, D_MODEL), BF16),
            jax.ShapeDtypeStruct((n, 2 * MLSTM_HEADS), F32),
        ],
        compiler_params=_params("parallel"),
        name="mlstm_proj",
    )(x, y3, g, w_qvo, wk, w_g, b_g)


def _cumsum_lanes_kernel(x_ref, o_ref):
    rows, t = x_ref.shape
    tri = _upper_tri(LANES)
    carry = jnp.zeros((rows, 1), F32)
    for j in range(t // LANES):
        blk = _dot_f32_by_01(x_ref[:, j * LANES:(j + 1) * LANES], tri) + carry
        o_ref[:, j * LANES:(j + 1) * LANES] = blk
        carry = blk[:, LANES - 1:LANES]


def _cumsum_lanes(x):
    return pl.pallas_call(
        _cumsum_lanes_kernel,
        out_shape=jax.ShapeDtypeStruct(x.shape, F32),
        compiler_params=_params(),
        name="cumsum_lanes",
    )(x)


def _fox_prompt_kernel(q_ref, kt_ref, vt_ref, c_ref, o_ref, *, blk):
    qi = pl.program_id(2)
    q2 = q_ref[0]
    lane = lax.broadcasted_iota(jnp.int32, (1, LANES), 1)
    head0 = lane < FOX_HEAD_DIM
    zero = jnp.zeros_like(q2)
    qh = (jnp.where(head0, q2, zero), jnp.where(head0, zero, q2))
    row = lax.broadcasted_iota(jnp.int32, (blk, blk), 0)
    col = lax.broadcasted_iota(jnp.int32, (blk, blk), 1)
    causal = col <= row

    def block(j, carry, masked):
        start = pl.multiple_of(j * blk, blk)
        kt = kt_ref[0, :, pl.ds(start, blk)]
        vt = vt_ref[0, :, pl.ds(start, blk)]
        out = []
        for h in range(2):
            m, l, acc = carry[h]
            cs = c_ref[0, 0, h:h + 1, pl.ds(start, blk)]
            s = _dot(qh[h], kt) - cs
            if masked:
                s = jnp.where(causal, s, NEG_INF)
            m_new = jnp.maximum(m, jnp.max(s, axis=1, keepdims=True))
            alpha = jnp.exp(m - m_new)
            p = jnp.exp(s - m_new)
            l = alpha * l + jnp.sum(p, axis=1, keepdims=True)
            acc = alpha * acc + _dot_nt(p.astype(BF16), vt)
            out.append((m_new, l, acc))
        return tuple(out)

    one = (jnp.full((blk, 1), NEG_INF, F32), jnp.zeros((blk, 1), F32), jnp.zeros((blk, LANES), F32))
    carry = lax.fori_loop(0, qi, functools.partial(block, masked=False), (one, one))
    (_, l0, acc0), (_, l1, acc1) = block(qi, carry, True)
    o_ref[0] = jnp.where(head0, acc0 / l0, acc1 / l1).astype(o_ref.dtype)


def _fox_prompt_attention(q, kt, vt, c_t, blk):
    b, t, _ = q.shape
    pairs = FOX_HEADS // 2
    return pl.pallas_call(
        functools.partial(_fox_prompt_kernel, blk=blk),
        grid=(b, pairs, t // blk),
        in_specs=[
            pl.BlockSpec((1, blk, LANES), lambda bi, p, qi: (bi, qi, p)),
            pl.BlockSpec((1, LANES, t), lambda bi, p, qi: (bi, p, 0)),
            pl.BlockSpec((1, LANES, t), lambda bi, p, qi: (bi, p, 0)),
            pl.BlockSpec((1, 1, 2, t), lambda bi, p, qi: (bi, p, 0, 0)),
        ],
        out_specs=pl.BlockSpec((1, blk, LANES), lambda bi, p, qi: (bi, qi, p)),
        out_shape=jax.ShapeDtypeStruct((b, t, D_MODEL), BF16),
        compiler_params=_params("parallel", "parallel", "arbitrary"),
        name="fox_prompt_attention",
    )(q, kt, vt, c_t)


def _fox_sample_kernel(pt_ref, q_ref, *refs, n_pages, page, pps):
    (kc_hbm, vc_hbm, lfc_hbm, kn_ref, vn_ref, lfn_ref, o_ref, kbuf, vbuf, lfbuf, sem,
     qbd_ref, m_ref, l_ref, acc_ref, carry_ref, kpad_ref, vpad_ref, lfpad_ref) = refs
    n_steps = n_pages // pps
    b = pl.program_id(0)
    p = pl.program_id(1)

    def page_copies(bi, step, slot):
        out = []
        for j in range(pps):
            pg = pt_ref[bi, step * pps + j]
            out.append(pltpu.make_async_copy(kc_hbm.at[pg], kbuf.at[slot, j], sem.at[slot]))
            out.append(pltpu.make_async_copy(vc_hbm.at[pg], vbuf.at[slot, j], sem.at[slot]))
            out.append(pltpu.make_async_copy(lfc_hbm.at[pg], lfbuf.at[slot, j], sem.at[slot]))
        return out

    def start_pages(bi, step, slot):
        for i, cp in enumerate(page_copies(bi, step, slot)):
            cp.start(priority=i % 2)
    n_q = q_ref.shape[1]
    chunk = 4 * FOX_HEAD_DIM
    n_chunks = D_MODEL // chunk
    rows = FOX_HEADS * n_q
    crow = 4 * n_q
    lane_c = lax.broadcasted_iota(jnp.int32, (1, chunk), 1)

    @pl.when(p == 0)
    def _init():
        q = q_ref[0]
        for c in range(n_chunks):
            qc = q[:, c * chunk:(c + 1) * chunk]
            pieces = [jnp.where(lane_c // FOX_HEAD_DIM == hl, qc, 0.0) for hl in range(4)]
            qbd_ref[c] = jnp.concatenate(pieces, axis=0).astype(BF16)
        m_ref[...] = jnp.full(m_ref.shape, NEG_INF, F32)
        l_ref[...] = jnp.zeros(l_ref.shape, F32)
        acc_ref[...] = jnp.zeros(acc_ref.shape, F32)
        carry_ref[...] = jnp.zeros(carry_ref.shape, F32)

    def process(get_kt, get_vt, lf_ts, mask):
        nb = len(lf_ts)
        local = _dot_f32_by_01(jnp.concatenate(lf_ts, axis=0), _upper_tri(page))
        carry = carry_ref[...]
        biases = []
        for j in range(nb):
            cum = local[j * FOX_HEADS:(j + 1) * FOX_HEADS] + carry
            carry = cum[:, page - 1:page]
            biases.append(jnp.concatenate(
                [jnp.broadcast_to(cum[h:h + 1, :], (n_q, page)) for h in range(FOX_HEADS)], axis=0))
        carry_ref[...] = carry
        bias = jnp.concatenate(biases, axis=1)
        s = jnp.concatenate(
            [_dot(qbd_ref[c], jnp.concatenate([get_kt(j, c) for j in range(nb)], axis=1)) for c in range(n_chunks)],
            axis=0) - bias
        if mask is not None:
            s = jnp.where(mask, s, NEG_INF)
        m_old = m_ref[...]
        m_new = jnp.maximum(m_old, jnp.max(s, axis=1, keepdims=True))
        alpha = jnp.exp(m_old - m_new)
        pr = jnp.exp(s - m_new)
        l_ref[...] = alpha * l_ref[...] + jnp.sum(pr, axis=1, keepdims=True)
        m_ref[...] = m_new
        pb = pr.astype(BF16)
        for c in range(n_chunks):
            sl = slice(c * crow, (c + 1) * crow)
            vt = jnp.concatenate([get_vt(j, c) for j in range(nb)], axis=1)
            acc_ref[sl, :] = alpha[sl] * acc_ref[sl, :] + _dot_nt(pb[sl], vt)

    for s in range(n_steps):
        @pl.when(p == s)
        def _past(s=s):
            slot = s % 2
            if s == 0:
                @pl.when(b == 0)
                def _():
                    start_pages(b, 0, 0)
            if s + 1 < n_steps:
                start_pages(b, s + 1, (s + 1) % 2)
            else:
                @pl.when(b + 1 < pl.num_programs(0))
                def _():
                    start_pages(b + 1, 0, 0)
            for cp in page_copies(b, s, slot):
                cp.wait()
            process(lambda j, c: kbuf[slot, j, c * chunk:(c + 1) * chunk, :].astype(BF16),
                    lambda j, c: vbuf[slot, j, c * chunk:(c + 1) * chunk, :].astype(BF16),
                    [lfbuf[slot, j] for j in range(pps)], None)

    @pl.when(p == n_steps)
    def _new():
        kpad_ref[...] = jnp.zeros(kpad_ref.shape, F32)
        vpad_ref[...] = jnp.zeros(vpad_ref.shape, F32)
        lfpad_ref[...] = jnp.zeros(lfpad_ref.shape, F32)
        kpad_ref[:, 0:n_q] = kn_ref[0]
        vpad_ref[:, 0:n_q] = vn_ref[0]
        lfpad_ref[:, 0:n_q] = lfn_ref[0]
        key = lax.broadcasted_iota(jnp.int32, (rows, page), 1)
        qry = lax.broadcasted_iota(jnp.int32, (rows, page), 0) % n_q
        process(lambda j, c: kpad_ref[c * chunk:(c + 1) * chunk, :].astype(BF16),
                lambda j, c: vpad_ref[c * chunk:(c + 1) * chunk, :].astype(BF16),
                [lfpad_ref[...]], key <= qry)
        inv_l = 1.0 / l_ref[...]
        for c in range(n_chunks):
            blk = acc_ref[c * crow:(c + 1) * crow, :] * inv_l[c * crow:(c + 1) * crow]
            out = jnp.zeros((n_q, chunk), F32)
            for hl in range(4):
                out = out + jnp.where(lane_c // FOX_HEAD_DIM == hl, blk[hl * n_q:(hl + 1) * n_q, :], 0.0)
            o_ref[0, :, c * chunk:(c + 1) * chunk] = out


def _fox_sample_attention(page_table, q, cache_kt, cache_vt, cache_lft, kt_new, vt_new, lft_new):
    b, n_q, _ = q.shape
    n_pages = page_table.shape[1]
    page = cache_kt.shape[2]
    rows = FOX_HEADS * n_q
    chunk = 4 * FOX_HEAD_DIM
    pps = max(d for d in (1, 2, 4, 8) if n_pages % (2 * d) == 0)

    per_b = lambda bi, p, pt: (bi, 0, 0)
    grid_spec = pltpu.PrefetchScalarGridSpec(
        num_scalar_prefetch=1,
        grid=(b, n_pages // pps + 1),
        in_specs=[
            pl.BlockSpec((1, n_q, D_MODEL), per_b),
            pl.BlockSpec(memory_space=pl.ANY),
            pl.BlockSpec(memory_space=pl.ANY),
            pl.BlockSpec(memory_space=pl.ANY),
            pl.BlockSpec((1, D_MODEL, n_q), per_b),
            pl.BlockSpec((1, D_MODEL, n_q), per_b),
            pl.BlockSpec((1, FOX_HEADS, n_q), per_b),
        ],
        out_specs=pl.BlockSpec((1, n_q, D_MODEL), per_b),
        scratch_shapes=[
            pltpu.VMEM((2, pps, D_MODEL, page), F32),
            pltpu.VMEM((2, pps, D_MODEL, page), F32),
            pltpu.VMEM((2, pps, FOX_HEADS, page), F32),
            pltpu.SemaphoreType.DMA((2,)),
            pltpu.VMEM((D_MODEL // chunk, 4 * n_q, chunk), BF16),
            pltpu.VMEM((rows, 1), F32),
            pltpu.VMEM((rows, 1), F32),
            pltpu.VMEM((rows, chunk), F32),
            pltpu.VMEM((FOX_HEADS, 1), F32),
            pltpu.VMEM((D_MODEL, page), F32),
            pltpu.VMEM((D_MODEL, page), F32),
            pltpu.VMEM((FOX_HEADS, page), F32),
        ],
    )
    return pl.pallas_call(
        functools.partial(_fox_sample_kernel, n_pages=n_pages, page=page, pps=pps),
        grid_spec=grid_spec,
        out_shape=jax.ShapeDtypeStruct((b, n_q, D_MODEL), F32),
        compiler_params=_params("arbitrary", "arbitrary"),
        name="fox_sample_attention",
    )(page_table, q, cache_kt, cache_vt, cache_lft, kt_new, vt_new, lft_new)


def _to_slabs(ref3, x):
    for j in range(D_MODEL // LANES):
        ref3[:, j, :] = x[:, j * LANES:(j + 1) * LANES]


def _from_slabs(ref3):
    return jnp.concatenate([ref3[:, j, :] for j in range(D_MODEL // LANES)], axis=1)


def _outproj_router_kernel(x_ref, o_ref, w_ref, g_ref, wr_ref, br_ref, xn_ref, h3_ref, route_ref):
    x_new = x_ref[...] + _dot(o_ref[...].astype(BF16), w_ref[...])
    xn_ref[...] = x_new
    h = _rmsnorm(x_new, g_ref[...])
    _to_slabs(h3_ref, h)
    w_r = wr_ref[...]
    h_hi, w_hi = h.astype(BF16), w_r.astype(BF16)
    h_lo = (h - h_hi.astype(F32)).astype(BF16)
    w_lo = (w_r - w_hi.astype(F32)).astype(BF16)
    logits = _dot(h_hi, w_hi) + (_dot(h_lo, w_hi) + _dot(h_hi, w_lo)) + br_ref[...]
    lane = lax.broadcasted_iota(jnp.int32, logits.shape, 1)
    big = jnp.int32(LANES)
    is_group = (lane >= MOE_EXPERTS) & (lane < MOE_EXPERTS + MOE_GROUPS)
    gl = jnp.where(is_group, logits, NEG_INF)
    g_max = jnp.max(gl, axis=1, keepdims=True)
    g_sel = jnp.min(jnp.where(gl == g_max, lane - MOE_EXPERTS, big), axis=1, keepdims=True)
    p_g = 1.0 / jnp.sum(jnp.where(is_group, jnp.exp(gl - g_max), 0.0), axis=1, keepdims=True)
    in_group = (lane < MOE_EXPERTS) & (lane // MOE_EXPERTS_PER_GROUP == g_sel)
    el = jnp.where(in_group, logits, NEG_INF)
    v1 = jnp.max(el, axis=1, keepdims=True)
    i1 = jnp.min(jnp.where(el == v1, lane, big), axis=1, keepdims=True)
    el2 = jnp.where(lane == i1, NEG_INF, el)
    v2 = jnp.max(el2, axis=1, keepdims=True)
    i2 = jnp.min(jnp.where(el2 == v2, lane, big), axis=1, keepdims=True)
    e2 = jnp.exp(v2 - v1)
    w1 = p_g / (1.0 + e2)
    w2 = p_g * e2 / (1.0 + e2)
    route_ref[...] = jnp.where(lane == 0, i1.astype(F32),
                               jnp.where(lane == 1, i2.astype(F32),
                                         jnp.where(lane == 2, w1, jnp.where(lane == 3, w2, 0.0))))


def _outproj_router(x, o, w_out, g_ffn, w_r, b_r, tm):
    n = x.shape[0]
    row = lambda i: (i, 0)
    const = lambda i: (0, 0)
    return pl.pallas_call(
        _outproj_router_kernel,
        grid=(n // tm,),
        in_specs=[
            pl.BlockSpec((tm, D_MODEL), row),
            pl.BlockSpec((tm, D_MODEL), row),
            pl.BlockSpec((D_MODEL, D_MODEL), const),
            pl.BlockSpec((1, D_MODEL), const),
            pl.BlockSpec((D_MODEL, LANES), const),
            pl.BlockSpec((1, LANES), const),
        ],
        out_specs=[
            pl.BlockSpec((tm, D_MODEL), row),
            pl.BlockSpec((tm, D_MODEL // LANES, LANES), lambda i: (i, 0, 0)),
            pl.BlockSpec((tm, LANES), row),
        ],
        out_shape=[
            jax.ShapeDtypeStruct((n, D_MODEL), F32),
            jax.ShapeDtypeStruct((n, D_MODEL // LANES, LANES), F32),
            jax.ShapeDtypeStruct((n, LANES), F32),
        ],
        compiler_params=_params("parallel"),
        name="outproj_router",
    )(x, o, w_out, g_ffn, w_r, b_r)


MOE_TILE = 256
MOE_TILE_PITCH = MOE_TILE + 1
MOE_UNROLL = 8


def _moe_kernel(off_ref, tok_ref, wt_ref, h3_ref, wg_ref, wu_ref, wd_ref, y3_ref, xg_ref, yg_ref):
    c = pl.program_id(0)
    e = pl.program_id(1)
    n_col = D_MODEL // LANES
    pitch = MOE_TILE_PITCH

    @pl.when((c == 0) & (e == 0))
    def _():
        xg_ref[...] = jnp.zeros(xg_ref.shape, F32)

    @pl.when(e == 0)
    def _():
        y3_ref[...] = jnp.zeros(y3_ref.shape, F32)

    base = off_ref[c * (MOE_EXPERTS + 1) + e]
    n = off_ref[c * (MOE_EXPERTS + 1) + e + 1] - base

    def gather_row(i, start):
        xg_ref[pl.ds(i, n_col, stride=pitch), :] = h3_ref[tok_ref[start + i]]

    def sub_tile(t, carry):
        start = base + t * MOE_TILE
        cnt = jnp.minimum(MOE_TILE, n - t * MOE_TILE)
        full = cnt // MOE_UNROLL

        def gather_group(g, _):
            for u in range(MOE_UNROLL):
                gather_row(g * MOE_UNROLL + u, start)
            return 0

        def gather_one(i, _):
            gather_row(i, start)
            return 0

        lax.fori_loop(0, full, gather_group, 0)
        lax.fori_loop(full * MOE_UNROLL, cnt, gather_one, 0)
        x = jnp.concatenate([xg_ref[j * pitch:j * pitch + MOE_TILE, :] for j in range(n_col)], axis=1).astype(BF16)
        hg = _dot(x, wg_ref[0, 0].astype(BF16))
        hu = _dot(x, wu_ref[0, 0].astype(BF16))
        y = _dot((hg * _sigmoid(hg) * hu).astype(BF16), wd_ref[0, 0].astype(BF16))
        for j in range(n_col):
            yg_ref[j * pitch:j * pitch + MOE_TILE, :] = y[:, j * LANES:(j + 1) * LANES]

        def scatter_group(g, _):
            rows, vals = [], []
            for u in range(MOE_UNROLL):
                i = g * MOE_UNROLL + u
                r = tok_ref[start + i]
                rows.append(r)
                vals.append(y3_ref[r] + wt_ref[start + i] * yg_ref[pl.ds(i, n_col, stride=pitch), :])
            for r, v in zip(rows, vals):
                y3_ref[r] = v
            return 0

        def scatter_one(i, _):
            r = tok_ref[start + i]
            y3_ref[r] = y3_ref[r] + wt_ref[start + i] * yg_ref[pl.ds(i, n_col, stride=pitch), :]
            return 0

        lax.fori_loop(0, full, scatter_group, 0)
        lax.fori_loop(full * MOE_UNROLL, cnt, scatter_one, 0)
        return carry

    lax.fori_loop(0, (n + MOE_TILE - 1) // MOE_TILE, sub_tile, 0)


def _moe(h3, route, w_gate, w_up, w_down, layer, chunk):
    n = h3.shape[0]
    n_chunks = n // chunk
    pairs = 2 * chunk
    expert = route[:, 0:2].astype(jnp.int32).reshape(n_chunks, pairs)
    weight = route[:, 2:4].reshape(n_chunks, pairs)
    order = jnp.argsort(expert, axis=1, stable=True)
    expert_sorted = jnp.take_along_axis(expert, order, axis=1)
    tok = (order // 2).astype(jnp.int32)
    wt = jnp.take_along_axis(weight, order, axis=1)
    bounds = jnp.arange(MOE_EXPERTS + 1, dtype=jnp.int32)
    off = jnp.sum(expert_sorted[:, None, :] < bounds[None, :, None], axis=2).astype(jnp.int32)
    off = (off + pairs * jnp.arange(n_chunks, dtype=jnp.int32)[:, None]).reshape(-1)
    tok = tok.reshape(-1)
    wt = wt.reshape(-1)

    n_col = D_MODEL // LANES
    tok_map = lambda c, e, *_: (c, 0, 0)
    grid_spec = pltpu.PrefetchScalarGridSpec(
        num_scalar_prefetch=3,
        grid=(n_chunks, MOE_EXPERTS),
        in_specs=[
            pl.BlockSpec((chunk, n_col, LANES), tok_map, pipeline_mode=pl.Buffered(1)),
            pl.BlockSpec((1, 1, D_MODEL, MOE_HIDDEN), lambda c, e, *_: (layer, e, 0, 0)),
            pl.BlockSpec((1, 1, D_MODEL, MOE_HIDDEN), lambda c, e, *_: (layer, e, 0, 0)),
            pl.BlockSpec((1, 1, MOE_HIDDEN, D_MODEL), lambda c, e, *_: (layer, e, 0, 0)),
        ],
        out_specs=pl.BlockSpec((chunk, n_col, LANES), tok_map, pipeline_mode=pl.Buffered(1)),
        scratch_shapes=[
            pltpu.VMEM((n_col * MOE_TILE_PITCH, LANES), F32),
            pltpu.VMEM((n_col * MOE_TILE_PITCH, LANES), F32),
        ],
    )
    return pl.pallas_call(
        _moe_kernel,
        grid_spec=grid_spec,
        out_shape=jax.ShapeDtypeStruct((n, n_col, LANES), F32),
        compiler_params=_params("arbitrary", "arbitrary"),
        name="moe_experts",
    )(off, tok, wt, h3, w_gate, w_up, w_down)


def _residual_norm_kernel(x_ref, y3_ref, g_ref, o_ref):
    o_ref[...] = _rmsnorm(x_ref[...] + _from_slabs(y3_ref), g_ref[...])


def _residual_norm(x, y3, g, tm):
    n = x.shape[0]
    row = lambda i: (i, 0)
    return pl.pallas_call(
        _residual_norm_kernel,
        grid=(n // tm,),
        in_specs=[
            pl.BlockSpec((tm, D_MODEL), row),
            pl.BlockSpec((tm, D_MODEL // LANES, LANES), lambda i: (i, 0, 0)),
            pl.BlockSpec((1, D_MODEL), lambda i: (0, 0)),
        ],
        out_specs=pl.BlockSpec((tm, D_MODEL), row),
        out_shape=jax.ShapeDtypeStruct((n, D_MODEL), F32),
        compiler_params=_params("parallel"),
        name="residual_norm",
    )(x, y3, g)


def _mlstm_kernel(q_ref, kt_ref, v_ref, o_ref, gc_ref, gr_ref, hn_ref, c0_ref, n0_ref, m0_ref,
                  h_ref, c_out_ref, n_out_ref, m_out_ref, c_s, n_s, m_s, *, chunk, pps, bb):
    ci = pl.program_id(2)

    @pl.when(ci == 0)
    def _():
        c_s[...] = c0_ref[...]
        n_s[...] = n0_ref[...]
        m_s[...] = m0_ref[...]

    lane = lax.broadcasted_iota(jnp.int32, (1, LANES), 1)
    srow = lax.broadcasted_iota(jnp.int32, (LANES, 1), 0)
    row = lax.broadcasted_iota(jnp.int32, (chunk, chunk), 0)
    col = lax.broadcasted_iota(jnp.int32, (chunk, chunk), 1)
    causal = col <= row
    lower = causal.astype(F32)
    upper = (row <= col).astype(F32)
    units = [(bi, pp) for bi in range(bb) for pp in range(pps)]
    old_state = {u: (c_s[u], n_s[u], m_s[u]) for u in units}
    new_state = {}
    for bi, pp in units:
        q2 = q_ref[bi, :, pp * LANES:(pp + 1) * LANES]
        kt2 = kt_ref[bi, pp * LANES:(pp + 1) * LANES, :]
        gc = gc_ref[bi, pp]
        gr = gr_ref[bi, pp]
        c_prev, n_prev, m_prev_row = old_state[(bi, pp)]
        b_cols = _dot_01_by_f32(lower, gc)
        b_rows = _dot_f32_by_01(gr, upper)
        c_prev_b = c_prev.astype(BF16)
        kt2f = kt2.astype(F32)
        c_new = jnp.zeros((LANES, MLSTM_DV), F32)
        n_new = jnp.zeros((1, LANES), F32)
        scale_rows = jnp.zeros((LANES, 1), F32)
        scale_lanes = jnp.zeros((1, LANES), F32)
        m_lanes = jnp.zeros((1, LANES), F32)
        for hl in range(2):
            in_head = lane // MLSTM_DK == hl
            in_head_rows = srow // MLSTM_DK == hl
            vsl = slice((2 * pp + hl) * MLSTM_DV, (2 * pp + hl + 1) * MLSTM_DV)
            b_col = b_cols[:, 2 + hl:3 + hl]
            b_row = b_rows[2 + hl:3 + hl, :]
            i_row = gr[hl:hl + 1, :]
            m_prev = m_prev_row[:, hl * MLSTM_DK:hl * MLSTM_DK + 1]
            d = jnp.where(causal, b_col - b_row + i_row, NEG_INF)
            inter = b_col + m_prev
            m_t = jnp.maximum(inter, jnp.max(d, axis=1, keepdims=True))
            w_inter = jnp.exp(inter - m_t)
            qh = jnp.where(in_head, q2, jnp.zeros_like(q2))
            s = jnp.exp(d - m_t) * _dot(qh, kt2)
            vh = v_ref[bi, :, vsl]
            num = _dot(s.astype(BF16), vh) + w_inter * _dot(qh, c_prev_b)
            qn = jnp.sum(qh.astype(F32) * n_prev, axis=1, keepdims=True)
            den = jnp.sum(s, axis=1, keepdims=True) + w_inter * qn
            hval = num / jnp.maximum(jnp.abs(den), jnp.exp(-m_t))
            hnorm = hval * lax.rsqrt(jnp.mean(hval * hval, axis=1, keepdims=True) + RMS_EPS)
            hnorm = hnorm * hn_ref[:, vsl]
            og = o_ref[bi, :, vsl].astype(F32)
            h_ref[bi, :, vsl] = (_sigmoid(og) * hnorm).astype(h_ref.dtype)
            b_last = b_row[:, chunk - 1:chunk]
            decay = b_last - b_row + i_row
            m_new = jnp.maximum(b_last + m_prev, jnp.max(decay, axis=1, keepdims=True))
            w_k = jnp.exp(decay - m_new)
            carry_scale = jnp.exp(b_last + m_prev - m_new)
            kth = jnp.where(in_head_rows, kt2f, 0.0)
            c_new = c_new + _dot((kth * w_k).astype(BF16), vh)
            n_new = n_new + _dot_nt(jnp.concatenate(_split3(w_k), axis=1).astype(BF16),
                                    jnp.concatenate([kth] * 3, axis=1).astype(BF16))
            scale_rows = jnp.where(in_head_rows, carry_scale, scale_rows)
            scale_lanes = jnp.where(in_head, carry_scale, scale_lanes)
            m_lanes = jnp.where(in_head, m_new, m_lanes)
        new_state[(bi, pp)] = (scale_rows * c_prev + c_new, scale_lanes * n_prev + n_new, m_lanes)
    for u in units:
        c_s[u], n_s[u], m_s[u] = new_state[u]

    @pl.when(ci == pl.num_programs(2) - 1)
    def _():
        c_out_ref[...] = c_s[...]
        n_out_ref[...] = n_s[...]
        m_out_ref[...] = m_s[...]


def _mlstm(q, kt, v, o, gcol, grow, head_norm, c0, n0, m0, chunk, pps, bb):
    b, t, _ = q.shape
    pairs = MLSTM_HEADS // 2
    state = lambda bi, p, ci: (bi, p, 0, 0)
    seq = lambda bi, p, ci: (bi, ci, p)
    return pl.pallas_call(
        functools.partial(_mlstm_kernel, chunk=chunk, pps=pps, bb=bb),
        grid=(b // bb, pairs // pps, t // chunk),
        in_specs=[
            pl.BlockSpec((bb, chunk, pps * LANES), seq),
            pl.BlockSpec((bb, pps * LANES, chunk), lambda bi, p, ci: (bi, p, ci)),
            pl.BlockSpec((bb, chunk, pps * 2 * MLSTM_DV), seq),
            pl.BlockSpec((bb, chunk, pps * 2 * MLSTM_DV), seq),
            pl.BlockSpec((bb, pps, chunk, 4), lambda bi, p, ci: (bi, p, ci, 0)),
            pl.BlockSpec((bb, pps, 4, chunk), lambda bi, p, ci: (bi, p, 0, ci)),
            pl.BlockSpec((1, pps * 2 * MLSTM_DV), lambda bi, p, ci: (0, p)),
            pl.BlockSpec((bb, pps, 2 * MLSTM_DK, MLSTM_DV), state),
            pl.BlockSpec((bb, pps, 1, 2 * MLSTM_DK), state),
            pl.BlockSpec((bb, pps, 1, 2 * MLSTM_DK), state),
        ],
        out_specs=[
            pl.BlockSpec((bb, chunk, pps * 2 * MLSTM_DV), seq),
            pl.BlockSpec((bb, pps, 2 * MLSTM_DK, MLSTM_DV), state),
            pl.BlockSpec((bb, pps, 1, 2 * MLSTM_DK), state),
            pl.BlockSpec((bb, pps, 1, 2 * MLSTM_DK), state),
        ],
        out_shape=[
            jax.ShapeDtypeStruct((b, t, D_MODEL), BF16),
            jax.ShapeDtypeStruct((b, pairs, 2 * MLSTM_DK, MLSTM_DV), F32),
            jax.ShapeDtypeStruct((b, pairs, 1, 2 * MLSTM_DK), F32),
            jax.ShapeDtypeStruct((b, pairs, 1, 2 * MLSTM_DK), F32),
        ],
        scratch_shapes=[
            pltpu.VMEM((bb, pps, 2 * MLSTM_DK, MLSTM_DV), F32),
            pltpu.VMEM((bb, pps, 1, 2 * MLSTM_DK), F32),
            pltpu.VMEM((bb, pps, 1, 2 * MLSTM_DK), F32),
        ],
        compiler_params=_params("parallel", "parallel", "arbitrary"),
        name="mlstm_chunkwise",
    )(q, kt, v, o, gcol, grow, head_norm, c0, n0, m0)


def _mlstm_short_kernel(q_ref, kt_ref, v_ref, o_ref, gc_ref, gr_ref, hn_ref, c0_ref, n0_ref, mc_ref, mr_ref,
                        h_ref, c_out_ref, n_out_ref, m_out_ref, *, bb):
    seq = q_ref.shape[1]
    heads = MLSTM_HEADS
    rows = heads * seq
    feat = MLSTM_QK
    ri = lax.broadcasted_iota(jnp.int32, (rows, rows), 0)
    ci = lax.broadcasted_iota(jnp.int32, (rows, rows), 1)
    same = (ri // seq) == (ci // seq)
    causal = same & ((ci % seq) <= (ri % seq))
    same_f = same.astype(F32)
    col_mats = jnp.concatenate([causal.astype(F32), same_f], axis=0)
    row_mats = jnp.concatenate([(same & ((ri % seq) <= (ci % seq))).astype(F32), same_f], axis=1)
    fr = lax.broadcasted_iota(jnp.int32, (feat, rows), 0)
    fc = lax.broadcasted_iota(jnp.int32, (feat, rows), 1)
    feat_same = (fr // MLSTM_DK) == (fc // seq)
    spread_rows = (feat_same & (fc % seq == 0)).astype(F32)
    er = lax.broadcasted_iota(jnp.int32, (rows, feat), 0)
    ec = lax.broadcasted_iota(jnp.int32, (rows, feat), 1)
    spread_lanes = (((er // seq) == (ec // MLSTM_DK)) & (er % seq == 0)).astype(F32)
    q_same = (er // seq) == (ec // MLSTM_DK)
    for bi in range(bb):
        qf = q_ref[bi].astype(F32)
        qbd = jnp.where(q_same, jnp.concatenate([qf] * heads, axis=0), 0.0)
        qbd_b = qbd.astype(BF16)
        kt = kt_ref[bi]
        km = jnp.where(feat_same, jnp.concatenate([kt.astype(F32)] * heads, axis=1), 0.0)
        stack = lambda ref: jnp.concatenate(
            [ref[bi, :, h * MLSTM_DV:(h + 1) * MLSTM_DV].astype(F32) for h in range(heads)], axis=0)
        v_st = stack(v_ref).astype(BF16)
        o_st = stack(o_ref)
        hn_st = jnp.concatenate([jnp.broadcast_to(hn_ref[:, h * MLSTM_DV:(h + 1) * MLSTM_DV], (seq, MLSTM_DV))
                                 for h in range(heads)], axis=0)
        gc = gc_ref[bi]
        gr = gr_ref[bi]
        i_col, i_row = gc[:, 0:1], gr[0:1, :]
        cols = _dot_01_by_f32(col_mats, gc[:, 1:2])
        b_col, b_last_col = cols[:rows], cols[rows:]
        rws = _dot_f32_by_01(gr[1:2, :], row_mats)
        b_row, b_last_row = rws[:, :rows], rws[:, rows:]
        c_prev = c0_ref[bi]
        n_prev = n0_ref[bi]
        m_prev_col = mc_ref[bi]
        m_prev_row = mr_ref[bi]
        d = jnp.where(causal, b_col - b_row + i_row, NEG_INF)
        inter = b_col + m_prev_col
        m_t = jnp.maximum(inter, jnp.max(d, axis=1, keepdims=True))
        w_inter = jnp.exp(inter - m_t)
        qk = _dot(qbd_b, kt)
        s = jnp.exp(d - m_t) * jnp.concatenate([qk] * heads, axis=1)
        num = _dot(s.astype(BF16), v_st) + w_inter * _dot(qbd_b, c_prev.astype(BF16))
        qn = jnp.sum(qbd * n_prev, axis=1, keepdims=True)
        den = jnp.sum(s, axis=1, keepdims=True) + w_inter * qn
        hval = num / jnp.maximum(jnp.abs(den), jnp.exp(-m_t))
        hnorm = hval * lax.rsqrt(jnp.mean(hval * hval, axis=1, keepdims=True) + RMS_EPS) * hn_st
        out = (_sigmoid(o_st) * hnorm).astype(h_ref.dtype)
        for h in range(heads):
            h_ref[bi, :, h * MLSTM_DV:(h + 1) * MLSTM_DV] = out[h * seq:(h + 1) * seq, :]
        decay_row = b_last_row - b_row + i_row
        decay_col = b_last_col - b_col + i_col
        m_new_row = jnp.maximum(b_last_row + m_prev_row,
                                jnp.max(jnp.where(same, decay_col, NEG_INF), axis=0, keepdims=True))
        m_new_col = jnp.maximum(b_last_col + m_prev_col,
                                jnp.max(jnp.where(same, decay_row, NEG_INF), axis=1, keepdims=True))
        w_k = jnp.exp(decay_row - m_new_row)
        scale_row = jnp.exp(b_last_row + m_prev_row - m_new_row)
        scale_col = jnp.exp(b_last_col + m_prev_col - m_new_col)
        c_scale = _dot_01_by_f32(spread_rows, scale_col)
        n_scale = _dot_f32_by_01(scale_row, spread_lanes)
        c_out_ref[bi] = c_scale * c_prev + _dot((km * w_k).astype(BF16), v_st)
        n_out_ref[bi] = n_scale * n_prev + _dot_nt(jnp.concatenate(_split3(w_k), axis=1).astype(BF16),
                                                   jnp.concatenate([km] * 3, axis=1).astype(BF16))
        m_out_ref[bi] = m_new_row


def _mlstm_short(q, kt, v, o, gcol, grow, head_norm, c0, n0, m_col, m_row, bb):
    b, seq, _ = q.shape
    rows = MLSTM_HEADS * seq
    blk = lambda *shape: pl.BlockSpec((bb,) + shape, lambda i: (i,) + (0,) * len(shape))
    return pl.pallas_call(
        functools.partial(_mlstm_short_kernel, bb=bb),
        grid=(b // bb,),
        in_specs=[
            blk(seq, MLSTM_QK), blk(MLSTM_QK, seq), blk(seq, D_MODEL), blk(seq, D_MODEL),
            blk(rows, 2), blk(2, rows),
            pl.BlockSpec((1, D_MODEL), lambda i: (0, 0)),
            blk(MLSTM_QK, MLSTM_DV), blk(1, MLSTM_QK), blk(rows, 1), blk(1, rows),
        ],
        out_specs=[blk(seq, D_MODEL), blk(MLSTM_QK, MLSTM_DV), blk(1, MLSTM_QK), blk(1, rows)],
        out_shape=[
            jax.ShapeDtypeStruct((b, seq, D_MODEL), BF16),
            jax.ShapeDtypeStruct((b, MLSTM_QK, MLSTM_DV), F32),
            jax.ShapeDtypeStruct((b, 1, MLSTM_QK), F32),
            jax.ShapeDtypeStruct((b, 1, rows), F32),
        ],
        compiler_params=_params("parallel"),
        name="mlstm_short",
    )(q, kt, v, o, gcol, grow, head_norm, c0, n0, m_col, m_row)


def _pad_lanes(w, width=LANES):
    return jnp.pad(w, ((0, 0), (0, width - w.shape[1])))


def _pad_rows(w, height=LANES):
    return jnp.pad(w, ((0, height - w.shape[0]), (0, 0)))


def _router_weights(w_group, b_group, w_router, b_router):
    w = _pad_lanes(jnp.concatenate([w_router, w_group], axis=1).astype(F32))
    b = _pad_lanes(jnp.concatenate([b_router, b_group])[None, :].astype(F32))
    return w, b


def _mlstm_mixer(x2, y3, b, t, g, w_in, b_i, b_f, head_norm, c0, n0, m0, tm, chunk, pps, bb, transposed):
    pairs = MLSTM_HEADS // 2
    v_end = 2 * MLSTM_QK + 2 * D_MODEL
    w_qvo = jnp.concatenate([w_in[:, :MLSTM_QK], w_in[:, 2 * MLSTM_QK:v_end]], axis=1).astype(BF16)
    w_k = w_in[:, MLSTM_QK:2 * MLSTM_QK]
    w_g = _pad_lanes(w_in[:, v_end:]).astype(BF16)
    b_g = _pad_lanes(jnp.concatenate([b_i, b_f])[None, :].astype(F32))
    if transposed:
        x1, q, kt, v, o, gates = _mlstm_proj(x2, y3, g, w_qvo, w_k.T.astype(BF16), w_g, b_g, tm, seq_len=t)
    else:
        x1, q, k, v, o, gates = _mlstm_proj(x2, y3, g, w_qvo, w_k.astype(BF16), w_g, b_g, tm)
        kt = jnp.swapaxes(k.reshape(b, t, MLSTM_QK), 1, 2)
    q = q.reshape(b, t, MLSTM_QK)
    v = v.reshape(b, t, D_MODEL)
    o = o.reshape(b, t, D_MODEL)
    if not transposed:
        gcol = gates.reshape(b, t, 2, MLSTM_HEADS).transpose(0, 3, 1, 2).reshape(b, MLSTM_HEADS * t, 2)
        m_rep = jnp.repeat(m0, t, axis=1)
        h, c_f, n_f, m_f = _mlstm_short(q, kt, v, o, gcol, jnp.swapaxes(gcol, 1, 2), head_norm.reshape(1, D_MODEL),
                                        c0.reshape(b, MLSTM_QK, MLSTM_DV), n0.reshape(b, 1, MLSTM_QK),
                                        m_rep[:, :, None], m_rep[:, None, :], bb)
        return (x1, h.reshape(b * t, D_MODEL), c_f.reshape(b, MLSTM_HEADS, MLSTM_DK, MLSTM_DV),
                n_f.reshape(b, MLSTM_HEADS, MLSTM_DK), m_f[:, 0, ::t])
    gates = gates.reshape(b, t, 2, pairs, 2)
    gcol = gates.transpose(0, 3, 1, 2, 4).reshape(b, pairs, t, 4)
    grow = jnp.swapaxes(gcol, 2, 3)
    c0 = c0.reshape(b, pairs, 2 * MLSTM_DK, MLSTM_DV)
    n0 = n0.reshape(b, pairs, 1, 2 * MLSTM_DK)
    m0 = jnp.repeat(m0.reshape(b, pairs, 1, 2), MLSTM_DK, axis=3)
    h, c_f, n_f, m_f = _mlstm(q, kt, v, o, gcol, grow, head_norm.reshape(1, D_MODEL), c0, n0, m0, chunk, pps, bb)
    c_f = c_f.reshape(b, MLSTM_HEADS, MLSTM_DK, MLSTM_DV)
    n_f = n_f.reshape(b, MLSTM_HEADS, MLSTM_DK)
    m_f = m_f.reshape(b, MLSTM_HEADS, MLSTM_DK)[:, :, 0]
    return x1, h.reshape(b * t, D_MODEL), c_f, n_f, m_f


def kernel(x_prompt, x_sample, cache_k, cache_v, cache_logf, page_table, state_C, state_n, state_m, norm_mix, norm_ffn, norm_final, fox_w_in, fox_b_f, fox_w_out, mlstm_w_in, mlstm_b_i, mlstm_b_f, mlstm_head_norm, mlstm_w_out, moe_w_group, moe_b_group, moe_w_router, moe_b_router, moe_w_gate, moe_w_up, moe_w_down):
    bp, tp, _ = x_prompt.shape
    bs, ts, _ = x_sample.shape
    n_p, n_s = bp * tp, bs * ts
    tm_p, tm_s = min(512, tp), min(512, n_s)
    xp = x_prompt.reshape(n_p, D_MODEL)
    xs = x_sample.reshape(n_s, D_MODEL)
    g_final = norm_final[None, :]
    chunk_p, chunk_s = min(4096, n_p), min(4096, n_s)

    def ffn(x, o, w_out, layer, tm, chunk):
        w_r, b_r = _router_weights(moe_w_group[layer], moe_b_group[layer], moe_w_router[layer], moe_b_router[layer])
        x_new, h3, route = _outproj_router(x, o, w_out.astype(BF16), norm_ffn[layer][None, :], w_r, b_r, tm)
        return x_new, _moe(h3, route, moe_w_gate, moe_w_up, moe_w_down, layer, chunk)

    g0 = norm_mix[0][None, :]
    w_in = fox_w_in[0]
    wq, wk, wv, wf = (w_in[:, :D_MODEL], w_in[:, D_MODEL:2 * D_MODEL], w_in[:, 2 * D_MODEL:3 * D_MODEL],
                      w_in[:, 3 * D_MODEL:])
    b_f = fox_b_f[0].astype(F32)
    wq_b = wq.astype(BF16)
    qp, kpt, vpt, kptb, vptb, lfpt = _fox_proj(
        xp, g0, wq_b, wk.T.astype(BF16), wv.T.astype(BF16), _pad_rows(wf.T).astype(BF16),
        _pad_rows(b_f[:, None]), tm_p, BF16, seq_len=tp)
    qs, ks, vs, _, _, lfs = _fox_proj(
        xs, g0, wq_b, wk.astype(BF16), wv.astype(BF16), _pad_lanes(wf).astype(BF16),
        _pad_lanes(b_f[None, :]), tm_s, F32)

    c_t = _cumsum_lanes(lfpt.reshape(bp * FOX_HEADS, tp)).reshape(bp, FOX_HEADS // 2, 2, tp)
    op = _fox_prompt_attention(qp.reshape(bp, tp, D_MODEL), kptb, vptb, c_t, min(512, tp))
    n_pool, page = cache_k.shape[1], cache_k.shape[2]
    feat_major = lambda c: c.transpose(0, 2, 3, 1).reshape(n_pool, D_MODEL, page)
    seq_last = lambda a: jnp.swapaxes(a.reshape(bs, ts, -1), 1, 2)
    os_ = _fox_sample_attention(page_table, qs.reshape(bs, ts, D_MODEL), feat_major(cache_k[0]), feat_major(cache_v[0]),
                                jnp.swapaxes(cache_logf[0], 1, 2), seq_last(ks), seq_last(vs), seq_last(lfs))
    xp, yp3 = ffn(xp, op.reshape(n_p, D_MODEL), fox_w_out[0], 0, tm_p, chunk_p)
    xs, ys3 = ffn(xs, os_.reshape(n_s, D_MODEL), fox_w_out[0], 0, tm_s, chunk_s)

    g1 = norm_mix[1][None, :]
    zc = jnp.zeros((bp, MLSTM_HEADS, MLSTM_DK, MLSTM_DV), F32)
    zn = jnp.zeros((bp, MLSTM_HEADS, MLSTM_DK), F32)
    zm = jnp.zeros((bp, MLSTM_HEADS), F32)
    xp, hp, cpf, npf, mpf = _mlstm_mixer(xp, yp3, bp, tp, g1, mlstm_w_in[0], mlstm_b_i[0], mlstm_b_f[0],
                                         mlstm_head_norm[0], zc, zn, zm, tm_p, min(256, tp), 2, 1, True)
    xs, hs, csf, nsf, msf = _mlstm_mixer(xs, ys3, bs, ts, g1, mlstm_w_in[0], mlstm_b_i[0], mlstm_b_f[0],
                                         mlstm_head_norm[0], state_C[0], state_n[0], state_m[0], tm_s, ts,
                                         MLSTM_HEADS // 2, 4 if bs % 4 == 0 else 1, False)
    xp, yp3 = ffn(xp, hp, mlstm_w_out[0], 1, tm_p, chunk_p)
    xs, ys3 = ffn(xs, hs, mlstm_w_out[0], 1, tm_s, chunk_s)
    yp = _residual_norm(xp, yp3, g_final, tm_p)
    ys = _residual_norm(xs, ys3, g_final, tm_s)

    hd = (FOX_HEADS, FOX_HEAD_DIM)
    time_major = lambda a: a.reshape(bp, FOX_HEADS, -1, tp).transpose(0, 3, 1, 2)
    return (yp.reshape(bp, tp, D_MODEL), ys.reshape(bs, ts, D_MODEL),
            time_major(kpt)[None], time_major(vpt)[None], jnp.swapaxes(lfpt, 1, 2)[None],
            ks.reshape(1, bs, ts, *hd), vs.reshape(1, bs, ts, *hd), lfs.reshape(1, bs, ts, FOX_HEADS),
            cpf[None], npf[None], mpf[None], csf[None], nsf[None], msf[None])
```

```python
import functools

import jax
import jax.numpy as jnp
from jax import lax
from jax.experimental import pallas as pl
from jax.experimental.pallas import tpu as pltpu

D_MODEL = 1024
FOX_HEADS = 16
FOX_HEAD_DIM = 64
MLSTM_HEADS = 8
MLSTM_DK = 64
MLSTM_DV = 128
MLSTM_QK = MLSTM_HEADS * MLSTM_DK
MOE_GROUPS = 4
MOE_EXPERTS_PER_GROUP = 8
MOE_EXPERTS = 32
MOE_HIDDEN = 256
RMS_EPS = 1e-6

LANES = 128
VMEM_LIMIT_BYTES = 56 * 1024 * 1024

F32 = jnp.float32
BF16 = jnp.bfloat16
NEG_INF = float("-inf")


def _params(*sem):
    return pltpu.CompilerParams(dimension_semantics=sem, vmem_limit_bytes=VMEM_LIMIT_BYTES)


def _dot(a, b):
    return jnp.dot(a, b, preferred_element_type=F32)


def _dot_nt(a, b):
    return lax.dot_general(a, b, (((1,), (1,)), ((), ())), preferred_element_type=F32)


def _rmsnorm(x, g):
    return x * lax.rsqrt(jnp.mean(x * x, axis=-1, keepdims=True) + RMS_EPS) * g


def _log_sigmoid(z):
    return jnp.minimum(z, 0.0) - jnp.log1p(jnp.exp(-jnp.abs(z)))


def _sigmoid(z):
    return 1.0 / (1.0 + jnp.exp(-z))


def _upper_tri(n):
    r = lax.broadcasted_iota(jnp.int32, (n, n), 0)
    c = lax.broadcasted_iota(jnp.int32, (n, n), 1)
    return (r <= c).astype(F32)


def _split3(x):
    hi = x.astype(BF16).astype(F32)
    mid = (x - hi).astype(BF16).astype(F32)
    lo = (x - hi - mid).astype(BF16).astype(F32)
    return hi, mid, lo


def _dot_f32_by_01(x, m01):
    return _dot(jnp.concatenate(_split3(x), axis=1).astype(BF16), jnp.concatenate([m01] * 3, axis=0).astype(BF16))


def _dot_01_by_f32(m01, x):
    return _dot(jnp.concatenate([m01] * 3, axis=1).astype(BF16), jnp.concatenate(_split3(x), axis=0).astype(BF16))


def _fox_proj_kernel(x_ref, g_ref, wq_ref, wk_ref, wv_ref, wf_ref, bf_ref,
                     q_ref, k_ref, v_ref, kb_ref, vb_ref, lf_ref, *, transposed):
    hb = _rmsnorm(x_ref[...], g_ref[...]).astype(BF16)
    q = _dot(hb, wq_ref[...])
    q_ref[...] = (q * (FOX_HEAD_DIM ** -0.5)).astype(q_ref.dtype)
    if transposed:
        k = _dot_nt(wk_ref[...], hb)
        v = _dot_nt(wv_ref[...], hb)
        z = _dot_nt(wf_ref[...], hb) + bf_ref[...]
        k_ref[0] = k
        kb_ref[0] = k.astype(BF16)
        v_ref[0] = v
        vb_ref[0] = v.astype(BF16)
        lf_ref[0] = _log_sigmoid(z)[:FOX_HEADS, :]
    else:
        k = _dot(hb, wk_ref[...])
        v = _dot(hb, wv_ref[...])
        z = _dot(hb, wf_ref[...]) + bf_ref[...]
        k_ref[...] = k
        kb_ref[...] = k.astype(BF16)
        v_ref[...] = v
        vb_ref[...] = v.astype(BF16)
        lf_ref[...] = _log_sigmoid(z)[:, :FOX_HEADS]


def _fox_proj(x, g, wq, wk, wv, wf, bf, tm, q_dtype, seq_len=None):
    n = x.shape[0]
    transposed = seq_len is not None
    row = lambda i: (i, 0)
    const = lambda i: (0, 0)
    if transposed:
        nt = seq_len // tm
        b = n // seq_len
        tmap = lambda i: (i // nt, 0, i % nt)
        kv_spec = pl.BlockSpec((1, D_MODEL, tm), tmap)
        lf_spec = pl.BlockSpec((1, FOX_HEADS, tm), tmap)
        kv_shape = (b, D_MODEL, seq_len)
        lf_shape = (b, FOX_HEADS, seq_len)
    else:
        kv_spec = pl.BlockSpec((tm, D_MODEL), row)
        lf_spec = pl.BlockSpec((tm, FOX_HEADS), row)
        kv_shape = (n, D_MODEL)
        lf_shape = (n, FOX_HEADS)
    return pl.pallas_call(
        functools.partial(_fox_proj_kernel, transposed=transposed),
        grid=(n // tm,),
        in_specs=[
            pl.BlockSpec((tm, D_MODEL), row),
            pl.BlockSpec((1, D_MODEL), const),
            pl.BlockSpec(wq.shape, const),
            pl.BlockSpec(wk.shape, const),
            pl.BlockSpec(wv.shape, const),
            pl.BlockSpec(wf.shape, const),
            pl.BlockSpec(bf.shape, const),
        ],
        out_specs=[pl.BlockSpec((tm, D_MODEL), row), kv_spec, kv_spec, kv_spec, kv_spec, lf_spec],
        out_shape=[
            jax.ShapeDtypeStruct((n, D_MODEL), q_dtype),
            jax.ShapeDtypeStruct(kv_shape, F32),
            jax.ShapeDtypeStruct(kv_shape, F32),
            jax.ShapeDtypeStruct(kv_shape, BF16),
            jax.ShapeDtypeStruct(kv_shape, BF16),
            jax.ShapeDtypeStruct(lf_shape, F32),
        ],
        compiler_params=_params("parallel"),
        name="fox_proj",
    )(x, g, wq, wk, wv, wf, bf)


def _mlstm_proj_kernel(x_ref, y3_ref, g_ref, w_ref, wk_ref, wg_ref, bg_ref,
                       x1_ref, q_ref, k_ref, v_ref, o_ref, gate_ref, *, transposed):
    x1 = x_ref[...] + _from_slabs(y3_ref)
    x1_ref[...] = x1
    hb = _rmsnorm(x1, g_ref[...]).astype(BF16)
    q = _dot(hb, w_ref[:, 0:MLSTM_QK])
    q_ref[...] = (q * (MLSTM_DK ** -0.5)).astype(BF16)
    if transposed:
        k_ref[0] = _dot_nt(wk_ref[...], hb).astype(BF16)
    else:
        k_ref[...] = _dot(hb, wk_ref[...]).astype(BF16)
    v_ref[...] = _dot(hb, w_ref[:, MLSTM_QK:MLSTM_QK + D_MODEL]).astype(BF16)
    o_ref[...] = _dot(hb, w_ref[:, MLSTM_QK + D_MODEL:MLSTM_QK + 2 * D_MODEL]).astype(BF16)
    z = _dot(hb, wg_ref[...]) + bg_ref[...]
    lane = lax.broadcasted_iota(jnp.int32, z.shape, 1)
    gates = jnp.where(lane < MLSTM_HEADS, z, _log_sigmoid(z))
    gate_ref[...] = gates[:, :2 * MLSTM_HEADS]


def _mlstm_proj(x, y3, g, w_qvo, wk, w_g, b_g, tm, seq_len=None):
    n = x.shape[0]
    transposed = seq_len is not None
    row = lambda i: (i, 0)
    const = lambda i: (0, 0)
    if transposed:
        nt = seq_len // tm
        k_spec = pl.BlockSpec((1, MLSTM_QK, tm), lambda i: (i // nt, 0, i % nt))
        k_shape = (n // seq_len, MLSTM_QK, seq_len)
    else:
        k_spec = pl.BlockSpec((tm, MLSTM_QK), row)
        k_shape = (n, MLSTM_QK)
    return pl.pallas_call(
        functools.partial(_mlstm_proj_kernel, transposed=transposed),
        grid=(n // tm,),
        in_specs=[
            pl.BlockSpec((tm, D_MODEL), row),
            pl.BlockSpec((tm, D_MODEL // LANES, LANES), lambda i: (i, 0, 0)),
            pl.BlockSpec((1, D_MODEL), const),
            pl.BlockSpec(w_qvo.shape, const),
            pl.BlockSpec(wk.shape, const),
            pl.BlockSpec((D_MODEL, LANES), const),
            pl.BlockSpec((1, LANES), const),
        ],
        out_specs=[
            pl.BlockSpec((tm, D_MODEL), row),
            pl.BlockSpec((tm, MLSTM_QK), row),
            k_spec,
            pl.BlockSpec((tm, D_MODEL), row),
            pl.BlockSpec((tm, D_MODEL), row),
            pl.BlockSpec((tm, 2 * MLSTM_HEADS), row),
        ],
        out_shape=[
            jax.ShapeDtypeStruct((n, D_MODEL), F32),
            jax.ShapeDtypeStruct((n, MLSTM_QK), BF16),
            jax.ShapeDtypeStruct(k_shape, BF16),
            jax.ShapeDtypeStruct((n, D_MODEL), BF16),
            jax.ShapeDtypeStruct((n, D_MODEL), BF16),
            jax.ShapeDtypeStruct((n, 2 * MLSTM_HEADS), F32),
        ],
        compiler_params=_params("parallel"),
        name="mlstm_proj",
    )(x, y3, g, w_qvo, wk, w_g, b_g)


def _cumsum_lanes_kernel(x_ref, o_ref):
    rows, t = x_ref.shape
    tri = _upper_tri(LANES)
    carry = jnp.zeros((rows, 1), F32)
    for j in range(t // LANES):
        blk = _dot_f32_by_01(x_ref[:, j * LANES:(j + 1) * LANES], tri) + carry
        o_ref[:, j * LANES:(j + 1) * LANES] = blk
        carry = blk[:, LANES - 1:LANES]


def _cumsum_lanes(x):
    return pl.pallas_call(
        _cumsum_lanes_kernel,
        out_shape=jax.ShapeDtypeStruct(x.shape, F32),
        compiler_params=_params(),
        name="cumsum_lanes",
    )(x)


def _fox_prompt_kernel(q_ref, kt_ref, vt_ref, c_ref, o_ref, *, blk):
    qi = pl.program_id(2)
    q2 = q_ref[0]
    lane = lax.broadcasted_iota(jnp.int32, (1, LANES), 1)
    head0 = lane < FOX_HEAD_DIM
    zero = jnp.zeros_like(q2)
    qh = (jnp.where(head0, q2, zero), jnp.where(head0, zero, q2))
    row = lax.broadcasted_iota(jnp.int32, (blk, blk), 0)
    col = lax.broadcasted_iota(jnp.int32, (blk, blk), 1)
    causal = col <= row

    def block(j, carry, masked):
        start = pl.multiple_of(j * blk, blk)
        kt = kt_ref[0, :, pl.ds(start, blk)]
        vt = vt_ref[0, :, pl.ds(start, blk)]
        out = []
        for h in range(2):
            m, l, acc = carry[h]
            cs = c_ref[0, 0, h:h + 1, pl.ds(start, blk)]
            s = _dot(qh[h], kt) - cs
            if masked:
                s = jnp.where(causal, s, NEG_INF)
            m_new = jnp.maximum(m, jnp.max(s, axis=1, keepdims=True))
            alpha = jnp.exp(m - m_new)
            p = jnp.exp(s - m_new)
            l = alpha * l + jnp.sum(p, axis=1, keepdims=True)
            acc = alpha * acc + _dot_nt(p.astype(BF16), vt)
            out.append((m_new, l, acc))
        return tuple(out)

    one = (jnp.full((blk, 1), NEG_INF, F32), jnp.zeros((blk, 1), F32), jnp.zeros((blk, LANES), F32))
    carry = lax.fori_loop(0, qi, functools.partial(block, masked=False), (one, one))
    (_, l0, acc0), (_, l1, acc1) = block(qi, carry, True)
    o_ref[0] = jnp.where(head0, acc0 / l0, acc1 / l1).astype(o_ref.dtype)


def _fox_prompt_attention(q, kt, vt, c_t, blk):
    b, t, _ = q.shape
    pairs = FOX_HEADS // 2
    return pl.pallas_call(
        functools.partial(_fox_prompt_kernel, blk=blk),
        grid=(b, pairs, t // blk),
        in_specs=[
            pl.BlockSpec((1, blk, LANES), lambda bi, p, qi: (bi, qi, p)),
            pl.BlockSpec((1, LANES, t), lambda bi, p, qi: (bi, p, 0)),
            pl.BlockSpec((1, LANES, t), lambda bi, p, qi: (bi, p, 0)),
            pl.BlockSpec((1, 1, 2, t), lambda bi, p, qi: (bi, p, 0, 0)),
        ],
        out_specs=pl.BlockSpec((1, blk, LANES), lambda bi, p, qi: (bi, qi, p)),
        out_shape=jax.ShapeDtypeStruct((b, t, D_MODEL), F32),
        compiler_params=_params("parallel", "parallel", "arbitrary"),
        name="fox_prompt_attention",
    )(q, kt, vt, c_t)


def _fox_sample_kernel(pt_ref, q_ref, *refs, n_pages, page, pps):
    (kc_hbm, vc_hbm, lfc_hbm, kn_ref, vn_ref, lfn_ref, o_ref, kbuf, vbuf, lfbuf, sem,
     qbd_ref, m_ref, l_ref, acc_ref, carry_ref, kpad_ref, vpad_ref, lfpad_ref) = refs
    n_steps = n_pages // pps
    b = pl.program_id(0)
    p = pl.program_id(1)

    def page_copies(bi, step, slot):
        out = []
        for j in range(pps):
            pg = pt_ref[bi, step * pps + j]
            out.append(pltpu.make_async_copy(kc_hbm.at[pg], kbuf.at[slot, j], sem.at[slot]))
            out.append(pltpu.make_async_copy(vc_hbm.at[pg], vbuf.at[slot, j], sem.at[slot]))
            out.append(pltpu.make_async_copy(lfc_hbm.at[pg], lfbuf.at[slot, j], sem.at[slot]))
        return out

    def start_pages(bi, step, slot):
        for i, cp in enumerate(page_copies(bi, step, slot)):
            cp.start(priority=i % 2)
    n_q = q_ref.shape[1]
    chunk = 4 * FOX_HEAD_DIM
    n_chunks = D_MODEL // chunk
    rows = FOX_HEADS * n_q
    crow = 4 * n_q
    lane_c = lax.broadcasted_iota(jnp.int32, (1, chunk), 1)

    @pl.when(p == 0)
    def _init():
        q = q_ref[0]
        for c in range(n_chunks):
            qc = q[:, c * chunk:(c + 1) * chunk]
            pieces = [jnp.where(lane_c // FOX_HEAD_DIM == hl, qc, 0.0) for hl in range(4)]
            qbd_ref[c] = jnp.concatenate(pieces, axis=0).astype(BF16)
        m_ref[...] = jnp.full(m_ref.shape, NEG_INF, F32)
        l_ref[...] = jnp.zeros(l_ref.shape, F32)
        acc_ref[...] = jnp.zeros(acc_ref.shape, F32)
        carry_ref[...] = jnp.zeros(carry_ref.shape, F32)

    def process(get_kt, get_vt, lf_ts, mask):
        nb = len(lf_ts)
        local = _dot_f32_by_01(jnp.concatenate(lf_ts, axis=0), _upper_tri(page))
        carry = carry_ref[...]
        biases = []
        for j in range(nb):
            cum = local[j * FOX_HEADS:(j + 1) * FOX_HEADS] + carry
            carry = cum[:, page - 1:page]
            biases.append(jnp.concatenate(
                [jnp.broadcast_to(cum[h:h + 1, :], (n_q, page)) for h in range(FOX_HEADS)], axis=0))
        carry_ref[...] = carry
        bias = jnp.concatenate(biases, axis=1)
        s = jnp.concatenate(
            [_dot(qbd_ref[c], jnp.concatenate([get_kt(j, c) for j in range(nb)], axis=1)) for c in range(n_chunks)],
            axis=0) - bias
        if mask is not None:
            s = jnp.where(mask, s, NEG_INF)
        m_old = m_ref[...]
        m_new = jnp.maximum(m_old, jnp.max(s, axis=1, keepdims=True))
        alpha = jnp.exp(m_old - m_new)
        pr = jnp.exp(s - m_new)
        l_ref[...] = alpha * l_ref[...] + jnp.sum(pr, axis=1, keepdims=True)
        m_ref[...] = m_new
        pb = pr.astype(BF16)
        for c in range(n_chunks):
            sl = slice(c * crow, (c + 1) * crow)
            vt = jnp.concatenate([get_vt(j, c) for j in range(nb)], axis=1)
            acc_ref[sl, :] = alpha[sl] * acc_ref[sl, :] + _dot_nt(pb[sl], vt)

    for s in range(n_steps):
        @pl.when(p == s)
        def _past(s=s):
            slot = s % 2
            if s == 0:
                @pl.when(b == 0)
                def _():
                    start_pages(b, 0, 0)
            if s + 1 < n_steps:
                start_pages(b, s + 1, (s + 1) % 2)
            else:
                @pl.when(b + 1 < pl.num_programs(0))
                def _():
                    start_pages(b + 1, 0, 0)
            for cp in page_copies(b, s, slot):
                cp.wait()
            process(lambda j, c: kbuf[slot, j, c * chunk:(c + 1) * chunk, :].astype(BF16),
                    lambda j, c: vbuf[slot, j, c * chunk:(c + 1) * chunk, :].astype(BF16),
                    [lfbuf[slot, j] for j in range(pps)], None)

    @pl.when(p == n_steps)
    def _new():
        kpad_ref[...] = jnp.zeros(kpad_ref.shape, F32)
        vpad_ref[...] = jnp.zeros(vpad_ref.shape, F32)
        lfpad_ref[...] = jnp.zeros(lfpad_ref.shape, F32)
        kpad_ref[:, 0:n_q] = kn_ref[0]
        vpad_ref[:, 0:n_q] = vn_ref[0]
        lfpad_ref[:, 0:n_q] = lfn_ref[0]
        key = lax.broadcasted_iota(jnp.int32, (rows, page), 1)
        qry = lax.broadcasted_iota(jnp.int32, (rows, page), 0) % n_q
        process(lambda j, c: kpad_ref[c * chunk:(c + 1) * chunk, :].astype(BF16),
                lambda j, c: vpad_ref[c * chunk:(c + 1) * chunk, :].astype(BF16),
                [lfpad_ref[...]], key <= qry)
        inv_l = 1.0 / l_ref[...]
        for c in range(n_chunks):
            blk = acc_ref[c * crow:(c + 1) * crow, :] * inv_l[c * crow:(c + 1) * crow]
            out = jnp.zeros((n_q, chunk), F32)
            for hl in range(4):
                out = out + jnp.where(lane_c // FOX_HEAD_DIM == hl, blk[hl * n_q:(hl + 1) * n_q, :], 0.0)
            o_ref[0, :, c * chunk:(c + 1) * chunk] = out


def _fox_sample_attention(page_table, q, cache_kt, cache_vt, cache_lft, kt_new, vt_new, lft_new):
    b, n_q, _ = q.shape
    n_pages = page_table.shape[1]
    page = cache_kt.shape[2]
    rows = FOX_HEADS * n_q
    chunk = 4 * FOX_HEAD_DIM
    pps = max(d for d in (1, 2, 4, 8) if n_pages % (2 * d) == 0)

    per_b = lambda bi, p, pt: (bi, 0, 0)
    grid_spec = pltpu.PrefetchScalarGridSpec(
        num_scalar_prefetch=1,
        grid=(b, n_pages // pps + 1),
        in_specs=[
            pl.BlockSpec((1, n_q, D_MODEL), per_b),
            pl.BlockSpec(memory_space=pl.ANY),
            pl.BlockSpec(memory_space=pl.ANY),
            pl.BlockSpec(memory_space=pl.ANY),
            pl.BlockSpec((1, D_MODEL, n_q), per_b),
            pl.BlockSpec((1, D_MODEL, n_q), per_b),
            pl.BlockSpec((1, FOX_HEADS, n_q), per_b),
        ],
        out_specs=pl.BlockSpec((1, n_q, D_MODEL), per_b),
        scratch_shapes=[
            pltpu.VMEM((2, pps, D_MODEL, page), F32),
            pltpu.VMEM((2, pps, D_MODEL, page), F32),
            pltpu.VMEM((2, pps, FOX_HEADS, page), F32),
            pltpu.SemaphoreType.DMA((2,)),
            pltpu.VMEM((D_MODEL // chunk, 4 * n_q, chunk), BF16),
            pltpu.VMEM((rows, 1), F32),
            pltpu.VMEM((rows, 1), F32),
            pltpu.VMEM((rows, chunk), F32),
            pltpu.VMEM((FOX_HEADS, 1), F32),
            pltpu.VMEM((D_MODEL, page), F32),
            pltpu.VMEM((D_MODEL, page), F32),
            pltpu.VMEM((FOX_HEADS, page), F32),
        ],
    )
    return pl.pallas_call(
        functools.partial(_fox_sample_kernel, n_pages=n_pages, page=page, pps=pps),
        grid_spec=grid_spec,
        out_shape=jax.ShapeDtypeStruct((b, n_q, D_MODEL), F32),
        compiler_params=_params("arbitrary", "arbitrary"),
        name="fox_sample_attention",
    )(page_table, q, cache_kt, cache_vt, cache_lft, kt_new, vt_new, lft_new)


def _to_slabs(ref3, x):
    for j in range(D_MODEL // LANES):
        ref3[:, j, :] = x[:, j * LANES:(j + 1) * LANES]


def _from_slabs(ref3):
    return jnp.concatenate([ref3[:, j, :] for j in range(D_MODEL // LANES)], axis=1)


def _outproj_router_kernel(x_ref, o_ref, w_ref, g_ref, wr_ref, br_ref, xn_ref, h3_ref, route_ref):
    o = o_ref[...].astype(F32)
    w = w_ref[...]
    o_hi, w_hi = o.astype(BF16), w.astype(BF16)
    o_lo = (o - o_hi.astype(F32)).astype(BF16)
    w_lo = (w - w_hi.astype(F32)).astype(BF16)
    x_new = x_ref[...] + (_dot(o_hi, w_hi) + (_dot(o_lo, w_hi) + _dot(o_hi, w_lo)))
    xn_ref[...] = x_new
    h = _rmsnorm(x_new, g_ref[...])
    _to_slabs(h3_ref, h)
    w_r = wr_ref[...]
    h_hi, w_hi = h.astype(BF16), w_r.astype(BF16)
    h_lo = (h - h_hi.astype(F32)).astype(BF16)
    w_lo = (w_r - w_hi.astype(F32)).astype(BF16)
    logits = _dot(h_hi, w_hi) + (_dot(h_lo, w_hi) + _dot(h_hi, w_lo)) + br_ref[...]
    lane = lax.broadcasted_iota(jnp.int32, logits.shape, 1)
    big = jnp.int32(LANES)
    is_group = (lane >= MOE_EXPERTS) & (lane < MOE_EXPERTS + MOE_GROUPS)
    gl = jnp.where(is_group, logits, NEG_INF)
    g_max = jnp.max(gl, axis=1, keepdims=True)
    g_sel = jnp.min(jnp.where(gl == g_max, lane - MOE_EXPERTS, big), axis=1, keepdims=True)
    p_g = 1.0 / jnp.sum(jnp.where(is_group, jnp.exp(gl - g_max), 0.0), axis=1, keepdims=True)
    in_group = (lane < MOE_EXPERTS) & (lane // MOE_EXPERTS_PER_GROUP == g_sel)
    el = jnp.where(in_group, logits, NEG_INF)
    v1 = jnp.max(el, axis=1, keepdims=True)
    i1 = jnp.min(jnp.where(el == v1, lane, big), axis=1, keepdims=True)
    el2 = jnp.where(lane == i1, NEG_INF, el)
    v2 = jnp.max(el2, axis=1, keepdims=True)
    i2 = jnp.min(jnp.where(el2 == v2, lane, big), axis=1, keepdims=True)
    e2 = jnp.exp(v2 - v1)
    w1 = p_g / (1.0 + e2)
    w2 = p_g * e2 / (1.0 + e2)
    route_ref[...] = jnp.where(lane == 0, i1.astype(F32),
                               jnp.where(lane == 1, i2.astype(F32),
                                         jnp.where(lane == 2, w1, jnp.where(lane == 3, w2, 0.0))))


def _outproj_router(x, o, w_out, g_ffn, w_r, b_r, tm):
    n = x.shape[0]
    row = lambda i: (i, 0)
    const = lambda i: (0, 0)
    return pl.pallas_call(
        _outproj_router_kernel,
        grid=(n // tm,),
        in_specs=[
            pl.BlockSpec((tm, D_MODEL), row),
            pl.BlockSpec((tm, D_MODEL), row),
            pl.BlockSpec((D_MODEL, D_MODEL), const),
            pl.BlockSpec((1, D_MODEL), const),
            pl.BlockSpec((D_MODEL, LANES), const),
            pl.BlockSpec((1, LANES), const),
        ],
        out_specs=[
            pl.BlockSpec((tm, D_MODEL), row),
            pl.BlockSpec((tm, D_MODEL // LANES, LANES), lambda i: (i, 0, 0)),
            pl.BlockSpec((tm, LANES), row),
        ],
        out_shape=[
            jax.ShapeDtypeStruct((n, D_MODEL), F32),
            jax.ShapeDtypeStruct((n, D_MODEL // LANES, LANES), F32),
            jax.ShapeDtypeStruct((n, LANES), F32),
        ],
        compiler_params=_params("parallel"),
        name="outproj_router",
    )(x, o, w_out, g_ffn, w_r, b_r)


MOE_TILE = 256
MOE_TILE_PITCH = MOE_TILE + 1
MOE_UNROLL = 8


def _moe_kernel(off_ref, tok_ref, wt_ref, h3_ref, wg_ref, wu_ref, wd_ref, y3_ref, xg_ref, yg_ref):
    c = pl.program_id(0)
    e = pl.program_id(1)
    n_col = D_MODEL // LANES
    pitch = MOE_TILE_PITCH

    @pl.when((c == 0) & (e == 0))
    def _():
        xg_ref[...] = jnp.zeros(xg_ref.shape, F32)

    @pl.when(e == 0)
    def _():
        y3_ref[...] = jnp.zeros(y3_ref.shape, F32)

    base = off_ref[c * (MOE_EXPERTS + 1) + e]
    n = off_ref[c * (MOE_EXPERTS + 1) + e + 1] - base

    def gather_row(i, start):
        xg_ref[pl.ds(i, n_col, stride=pitch), :] = h3_ref[tok_ref[start + i]]

    def sub_tile(t, carry):
        start = base + t * MOE_TILE
        cnt = jnp.minimum(MOE_TILE, n - t * MOE_TILE)
        full = cnt // MOE_UNROLL

        def gather_group(g, _):
            for u in range(MOE_UNROLL):
                gather_row(g * MOE_UNROLL + u, start)
            return 0

        def gather_one(i, _):
            gather_row(i, start)
            return 0

        lax.fori_loop(0, full, gather_group, 0)
        lax.fori_loop(full * MOE_UNROLL, cnt, gather_one, 0)
        x = jnp.concatenate([xg_ref[j * pitch:j * pitch + MOE_TILE, :] for j in range(n_col)], axis=1).astype(BF16)
        hg = _dot(x, wg_ref[0, 0].astype(BF16))
        hu = _dot(x, wu_ref[0, 0].astype(BF16))
        y = _dot((hg * _sigmoid(hg) * hu).astype(BF16), wd_ref[0, 0].astype(BF16))
        for j in range(n_col):
            yg_ref[j * pitch:j * pitch + MOE_TILE, :] = y[:, j * LANES:(j + 1) * LANES]

        def scatter_group(g, _):
            rows, vals = [], []
            for u in range(MOE_UNROLL):
                i = g * MOE_UNROLL + u
                r = tok_ref[start + i]
                rows.append(r)
                vals.append(y3_ref[r] + wt_ref[start + i] * yg_ref[pl.ds(i, n_col, stride=pitch), :])
            for r, v in zip(rows, vals):
                y3_ref[r] = v
            return 0

        def scatter_one(i, _):
            r = tok_ref[start + i]
            y3_ref[r] = y3_ref[r] + wt_ref[start + i] * yg_ref[pl.ds(i, n_col, stride=pitch), :]
            return 0

        lax.fori_loop(0, full, scatter_group, 0)
        lax.fori_loop(full * MOE_UNROLL, cnt, scatter_one, 0)
        return carry

    lax.fori_loop(0, (n + MOE_TILE - 1) // MOE_TILE, sub_tile, 0)


def _moe(h3, route, w_gate, w_up, w_down, layer, chunk):
    n = h3.shape[0]
    n_chunks = n // chunk
    pairs = 2 * chunk
    expert = route[:, 0:2].astype(jnp.int32).reshape(n_chunks, pairs)
    weight = route[:, 2:4].reshape(n_chunks, pairs)
    order = jnp.argsort(expert, axis=1, stable=True)
    expert_sorted = jnp.take_along_axis(expert, order, axis=1)
    tok = (order // 2).astype(jnp.int32)
    wt = jnp.take_along_axis(weight, order, axis=1)
    bounds = jnp.arange(MOE_EXPERTS + 1, dtype=jnp.int32)
    off = jnp.sum(expert_sorted[:, None, :] < bounds[None, :, None], axis=2).astype(jnp.int32)
    off = (off + pairs * jnp.arange(n_chunks, dtype=jnp.int32)[:, None]).reshape(-1)
    tok = tok.reshape(-1)
    wt = wt.reshape(-1)

    n_col = D_MODEL // LANES
    tok_map = lambda c, e, *_: (c, 0, 0)
    grid_spec = pltpu.PrefetchScalarGridSpec(
        num_scalar_prefetch=3,
        grid=(n_chunks, MOE_EXPERTS),
        in_specs=[
            pl.BlockSpec((chunk, n_col, LANES), tok_map, pipeline_mode=pl.Buffered(1)),
            pl.BlockSpec((1, 1, D_MODEL, MOE_HIDDEN), lambda c, e, *_: (layer, e, 0, 0)),
            pl.BlockSpec((1, 1, D_MODEL, MOE_HIDDEN), lambda c, e, *_: (layer, e, 0, 0)),
            pl.BlockSpec((1, 1, MOE_HIDDEN, D_MODEL), lambda c, e, *_: (layer, e, 0, 0)),
        ],
        out_specs=pl.BlockSpec((chunk, n_col, LANES), tok_map, pipeline_mode=pl.Buffered(1)),
        scratch_shapes=[
            pltpu.VMEM((n_col * MOE_TILE_PITCH, LANES), F32),
            pltpu.VMEM((n_col * MOE_TILE_PITCH, LANES), F32),
        ],
    )
    return pl.pallas_call(
        _moe_kernel,
        grid_spec=grid_spec,
        out_shape=jax.ShapeDtypeStruct((n, n_col, LANES), F32),
        compiler_params=_params("arbitrary", "arbitrary"),
        name="moe_experts",
    )(off, tok, wt, h3, w_gate, w_up, w_down)


def _residual_norm_kernel(x_ref, y3_ref, g_ref, o_ref):
    o_ref[...] = _rmsnorm(x_ref[...] + _from_slabs(y3_ref), g_ref[...])


def _residual_norm(x, y3, g, tm):
    n = x.shape[0]
    row = lambda i: (i, 0)
    return pl.pallas_call(
        _residual_norm_kernel,
        grid=(n // tm,),
        in_specs=[
            pl.BlockSpec((tm, D_MODEL), row),
            pl.BlockSpec((tm, D_MODEL // LANES, LANES), lambda i: (i, 0, 0)),
            pl.BlockSpec((1, D_MODEL), lambda i: (0, 0)),
        ],
        out_specs=pl.BlockSpec((tm, D_MODEL), row),
        out_shape=jax.ShapeDtypeStruct((n, D_MODEL), F32),
        compiler_params=_params("parallel"),
        name="residual_norm",
    )(x, y3, g)


def _mlstm_kernel(q_ref, kt_ref, v_ref, o_ref, gc_ref, gr_ref, hn_ref, c0_ref, n0_ref, m0_ref,
                  h_ref, c_out_ref, n_out_ref, m_out_ref, c_s, n_s, m_s, *, chunk, pps, bb):
    ci = pl.program_id(2)

    @pl.when(ci == 0)
    def _():
        c_s[...] = c0_ref[...]
        n_s[...] = n0_ref[...]
        m_s[...] = m0_ref[...]

    lane = lax.broadcasted_iota(jnp.int32, (1, LANES), 1)
    srow = lax.broadcasted_iota(jnp.int32, (LANES, 1), 0)
    row = lax.broadcasted_iota(jnp.int32, (chunk, chunk), 0)
    col = lax.broadcasted_iota(jnp.int32, (chunk, chunk), 1)
    causal = col <= row
    lower = causal.astype(F32)
    upper = (row <= col).astype(F32)
    units = [(bi, pp) for bi in range(bb) for pp in range(pps)]
    old_state = {u: (c_s[u], n_s[u], m_s[u]) for u in units}
    new_state = {}
    for bi, pp in units:
        q2 = q_ref[bi, :, pp * LANES:(pp + 1) * LANES]
        kt2 = kt_ref[bi, pp * LANES:(pp + 1) * LANES, :]
        gc = gc_ref[bi, pp]
        gr = gr_ref[bi, pp]
        c_prev, n_prev, m_prev_row = old_state[(bi, pp)]
        b_cols = _dot_01_by_f32(lower, gc)
        b_rows = _dot_f32_by_01(gr, upper)
        c_prev_b = c_prev.astype(BF16)
        kt2f = kt2.astype(F32)
        c_new = jnp.zeros((LANES, MLSTM_DV), F32)
        n_new = jnp.zeros((1, LANES), F32)
        scale_rows = jnp.zeros((LANES, 1), F32)
        scale_lanes = jnp.zeros((1, LANES), F32)
        m_lanes = jnp.zeros((1, LANES), F32)
        for hl in range(2):
            in_head = lane // MLSTM_DK == hl
            in_head_rows = srow // MLSTM_DK == hl
            vsl = slice((2 * pp + hl) * MLSTM_DV, (2 * pp + hl + 1) * MLSTM_DV)
            b_col = b_cols[:, 2 + hl:3 + hl]
            b_row = b_rows[2 + hl:3 + hl, :]
            i_row = gr[hl:hl + 1, :]
            m_prev = m_prev_row[:, hl * MLSTM_DK:hl * MLSTM_DK + 1]
            d = jnp.where(causal, b_col - b_row + i_row, NEG_INF)
            inter = b_col + m_prev
            m_t = jnp.maximum(inter, jnp.max(d, axis=1, keepdims=True))
            w_inter = jnp.exp(inter - m_t)
            qh = jnp.where(in_head, q2, jnp.zeros_like(q2))
            s = jnp.exp(d - m_t) * _dot(qh, kt2)
            vh = v_ref[bi, :, vsl]
            num = _dot(s.astype(BF16), vh) + w_inter * _dot(qh, c_prev_b)
            qn = jnp.sum(qh.astype(F32) * n_prev, axis=1, keepdims=True)
            den = jnp.sum(s, axis=1, keepdims=True) + w_inter * qn
            hval = num / jnp.maximum(jnp.abs(den), jnp.exp(-m_t))
            hnorm = hval * lax.rsqrt(jnp.mean(hval * hval, axis=1, keepdims=True) + RMS_EPS)
            hnorm = hnorm * hn_ref[:, vsl]
            og = o_ref[bi, :, vsl].astype(F32)
            h_ref[bi, :, vsl] = (_sigmoid(og) * hnorm).astype(h_ref.dtype)
            b_last = b_row[:, chunk - 1:chunk]
            decay = b_last - b_row + i_row
            m_new = jnp.maximum(b_last + m_prev, jnp.max(decay, axis=1, keepdims=True))
            w_k = jnp.exp(decay - m_new)
            carry_scale = jnp.exp(b_last + m_prev - m_new)
            kth = jnp.where(in_head_rows, kt2f, 0.0)
            c_new = c_new + _dot((kth * w_k).astype(BF16), vh)
            n_new = n_new + _dot_nt(jnp.concatenate(_split3(w_k), axis=1).astype(BF16),
                                    jnp.concatenate([kth] * 3, axis=1).astype(BF16))
            scale_rows = jnp.where(in_head_rows, carry_scale, scale_rows)
            scale_lanes = jnp.where(in_head, carry_scale, scale_lanes)
            m_lanes = jnp.where(in_head, m_new, m_lanes)
        new_state[(bi, pp)] = (scale_rows * c_prev + c_new, scale_lanes * n_prev + n_new, m_lanes)
    for u in units:
        c_s[u], n_s[u], m_s[u] = new_state[u]

    @pl.when(ci == pl.num_programs(2) - 1)
    def _():
        c_out_ref[...] = c_s[...]
        n_out_ref[...] = n_s[...]
        m_out_ref[...] = m_s[...]


def _mlstm(q, kt, v, o, gcol, grow, head_norm, c0, n0, m0, chunk, pps, bb):
    b, t, _ = q.shape
    pairs = MLSTM_HEADS // 2
    state = lambda bi, p, ci: (bi, p, 0, 0)
    seq = lambda bi, p, ci: (bi, ci, p)
    return pl.pallas_call(
        functools.partial(_mlstm_kernel, chunk=chunk, pps=pps, bb=bb),
        grid=(b // bb, pairs // pps, t // chunk),
        in_specs=[
            pl.BlockSpec((bb, chunk, pps * LANES), seq),
            pl.BlockSpec((bb, pps * LANES, chunk), lambda bi, p, ci: (bi, p, ci)),
            pl.BlockSpec((bb, chunk, pps * 2 * MLSTM_DV), seq),
            pl.BlockSpec((bb, chunk, pps * 2 * MLSTM_DV), seq),
            pl.BlockSpec((bb, pps, chunk, 4), lambda bi, p, ci: (bi, p, ci, 0)),
            pl.BlockSpec((bb, pps, 4, chunk), lambda bi, p, ci: (bi, p, 0, ci)),
            pl.BlockSpec((1, pps * 2 * MLSTM_DV), lambda bi, p, ci: (0, p)),
            pl.BlockSpec((bb, pps, 2 * MLSTM_DK, MLSTM_DV), state),
            pl.BlockSpec((bb, pps, 1, 2 * MLSTM_DK), state),
            pl.BlockSpec((bb, pps, 1, 2 * MLSTM_DK), state),
        ],
        out_specs=[
            pl.BlockSpec((bb, chunk, pps * 2 * MLSTM_DV), seq),
            pl.BlockSpec((bb, pps, 2 * MLSTM_DK, MLSTM_DV), state),
            pl.BlockSpec((bb, pps, 1, 2 * MLSTM_DK), state),
            pl.BlockSpec((bb, pps, 1, 2 * MLSTM_DK), state),
        ],
        out_shape=[
            jax.ShapeDtypeStruct((b, t, D_MODEL), BF16),
            jax.ShapeDtypeStruct((b, pairs, 2 * MLSTM_DK, MLSTM_DV), F32),
            jax.ShapeDtypeStruct((b, pairs, 1, 2 * MLSTM_DK), F32),
            jax.ShapeDtypeStruct((b, pairs, 1, 2 * MLSTM_DK), F32),
        ],
        scratch_shapes=[
            pltpu.VMEM((bb, pps, 2 * MLSTM_DK, MLSTM_DV), F32),
            pltpu.VMEM((bb, pps, 1, 2 * MLSTM_DK), F32),
            pltpu.VMEM((bb, pps, 1, 2 * MLSTM_DK), F32),
        ],
        compiler_params=_params("parallel", "parallel", "arbitrary"),
        name="mlstm_chunkwise",
    )(q, kt, v, o, gcol, grow, head_norm, c0, n0, m0)


def _mlstm_short_kernel(q_ref, kt_ref, v_ref, o_ref, gc_ref, gr_ref, hn_ref, c0_ref, n0_ref, mc_ref, mr_ref,
                        h_ref, c_out_ref, n_out_ref, m_out_ref, *, bb):
    seq = q_ref.shape[1]
    heads = MLSTM_HEADS
    rows = heads * seq
    feat = MLSTM_QK
    ri = lax.broadcasted_iota(jnp.int32, (rows, rows), 0)
    ci = lax.broadcasted_iota(jnp.int32, (rows, rows), 1)
    same = (ri // seq) == (ci // seq)
    causal = same & ((ci % seq) <= (ri % seq))
    same_f = same.astype(F32)
    col_mats = jnp.concatenate([causal.astype(F32), same_f], axis=0)
    row_mats = jnp.concatenate([(same & ((ri % seq) <= (ci % seq))).astype(F32), same_f], axis=1)
    fr = lax.broadcasted_iota(jnp.int32, (feat, rows), 0)
    fc = lax.broadcasted_iota(jnp.int32, (feat, rows), 1)
    feat_same = (fr // MLSTM_DK) == (fc // seq)
    spread_rows = (feat_same & (fc % seq == 0)).astype(F32)
    er = lax.broadcasted_iota(jnp.int32, (rows, feat), 0)
    ec = lax.broadcasted_iota(jnp.int32, (rows, feat), 1)
    spread_lanes = (((er // seq) == (ec // MLSTM_DK)) & (er % seq == 0)).astype(F32)
    q_same = (er // seq) == (ec // MLSTM_DK)
    for bi in range(bb):
        qf = q_ref[bi].astype(F32)
        qbd = jnp.where(q_same, jnp.concatenate([qf] * heads, axis=0), 0.0)
        qbd_b = qbd.astype(BF16)
        kt = kt_ref[bi]
        km = jnp.where(feat_same, jnp.concatenate([kt.astype(F32)] * heads, axis=1), 0.0)
        stack = lambda ref: jnp.concatenate(
            [ref[bi, :, h * MLSTM_DV:(h + 1) * MLSTM_DV].astype(F32) for h in range(heads)], axis=0)
        v_st = stack(v_ref).astype(BF16)
        o_st = stack(o_ref)
        hn_st = jnp.concatenate([jnp.broadcast_to(hn_ref[:, h * MLSTM_DV:(h + 1) * MLSTM_DV], (seq, MLSTM_DV))
                                 for h in range(heads)], axis=0)
        gc = gc_ref[bi]
        gr = gr_ref[bi]
        i_col, i_row = gc[:, 0:1], gr[0:1, :]
        cols = _dot_01_by_f32(col_mats, gc[:, 1:2])
        b_col, b_last_col = cols[:rows], cols[rows:]
        rws = _dot_f32_by_01(gr[1:2, :], row_mats)
        b_row, b_last_row = rws[:, :rows], rws[:, rows:]
        c_prev = c0_ref[bi]
        n_prev = n0_ref[bi]
        m_prev_col = mc_ref[bi]
        m_prev_row = mr_ref[bi]
        d = jnp.where(causal, b_col - b_row + i_row, NEG_INF)
        inter = b_col + m_prev_col
        m_t = jnp.maximum(inter, jnp.max(d, axis=1, keepdims=True))
        w_inter = jnp.exp(inter - m_t)
        qk = _dot(qbd_b, kt)
        s = jnp.exp(d - m_t) * jnp.concatenate([qk] * heads, axis=1)
        num = _dot(s.astype(BF16), v_st) + w_inter * _dot(qbd_b, c_prev.astype(BF16))
        qn = jnp.sum(qbd * n_prev, axis=1, keepdims=True)
        den = jnp.sum(s, axis=1, keepdims=True) + w_inter * qn
        hval = num / jnp.maximum(jnp.abs(den), jnp.exp(-m_t))
        hnorm = hval * lax.rsqrt(jnp.mean(hval * hval, axis=1, keepdims=True) + RMS_EPS) * hn_st
        out = (_sigmoid(o_st) * hnorm).astype(h_ref.dtype)
        for h in range(heads):
            h_ref[bi, :, h * MLSTM_DV:(h + 1) * MLSTM_DV] = out[h * seq:(h + 1) * seq, :]
        decay_row = b_last_row - b_row + i_row
        decay_col = b_last_col - b_col + i_col
        m_new_row = jnp.maximum(b_last_row + m_prev_row,
                                jnp.max(jnp.where(same, decay_col, NEG_INF), axis=0, keepdims=True))
        m_new_col = jnp.maximum(b_last_col + m_prev_col,
                                jnp.max(jnp.where(same, decay_row, NEG_INF), axis=1, keepdims=True))
        w_k = jnp.exp(decay_row - m_new_row)
        scale_row = jnp.exp(b_last_row + m_prev_row - m_new_row)
        scale_col = jnp.exp(b_last_col + m_prev_col - m_new_col)
        c_scale = _dot_01_by_f32(spread_rows, scale_col)
        n_scale = _dot_f32_by_01(scale_row, spread_lanes)
        c_out_ref[bi] = c_scale * c_prev + _dot((km * w_k).astype(BF16), v_st)
        n_out_ref[bi] = n_scale * n_prev + _dot_nt(jnp.concatenate(_split3(w_k), axis=1).astype(BF16),
                                                   jnp.concatenate([km] * 3, axis=1).astype(BF16))
        m_out_ref[bi] = m_new_row


def _mlstm_short(q, kt, v, o, gcol, grow, head_norm, c0, n0, m_col, m_row, bb):
    b, seq, _ = q.shape
    rows = MLSTM_HEADS * seq
    blk = lambda *shape: pl.BlockSpec((bb,) + shape, lambda i: (i,) + (0,) * len(shape))
    return pl.pallas_call(
        functools.partial(_mlstm_short_kernel, bb=bb),
        grid=(b // bb,),
        in_specs=[
            blk(seq, MLSTM_QK), blk(MLSTM_QK, seq), blk(seq, D_MODEL), blk(seq, D_MODEL),
            blk(rows, 2), blk(2, rows),
            pl.BlockSpec((1, D_MODEL), lambda i: (0, 0)),
            blk(MLSTM_QK, MLSTM_DV), blk(1, MLSTM_QK), blk(rows, 1), blk(1, rows),
        ],
        out_specs=[blk(seq, D_MODEL), blk(MLSTM_QK, MLSTM_DV), blk(1, MLSTM_QK), blk(1, rows)],
        out_shape=[
            jax.ShapeDtypeStruct((b, seq, D_MODEL), BF16),
            jax.ShapeDtypeStruct((b, MLSTM_QK, MLSTM_DV), F32),
            jax.ShapeDtypeStruct((b, 1, MLSTM_QK), F32),
            jax.ShapeDtypeStruct((b, 1, rows), F32),
        ],
        compiler_params=_params("parallel"),
        name="mlstm_short",
    )(q, kt, v, o, gcol, grow, head_norm, c0, n0, m_col, m_row)


def _pad_lanes(w, width=LANES):
    return jnp.pad(w, ((0, 0), (0, width - w.shape[1])))


def _pad_rows(w, height=LANES):
    return jnp.pad(w, ((0, height - w.shape[0]), (0, 0)))


def _router_weights(w_group, b_group, w_router, b_router):
    w = _pad_lanes(jnp.concatenate([w_router, w_group], axis=1).astype(F32))
    b = _pad_lanes(jnp.concatenate([b_router, b_group])[None, :].astype(F32))
    return w, b


def _mlstm_mixer(x2, y3, b, t, g, w_in, b_i, b_f, head_norm, c0, n0, m0, tm, chunk, pps, bb, transposed):
    pairs = MLSTM_HEADS // 2
    v_end = 2 * MLSTM_QK + 2 * D_MODEL
    w_qvo = jnp.concatenate([w_in[:, :MLSTM_QK], w_in[:, 2 * MLSTM_QK:v_end]], axis=1).astype(BF16)
    w_k = w_in[:, MLSTM_QK:2 * MLSTM_QK]
    w_g = _pad_lanes(w_in[:, v_end:]).astype(BF16)
    b_g = _pad_lanes(jnp.concatenate([b_i, b_f])[None, :].astype(F32))
    if transposed:
        x1, q, kt, v, o, gates = _mlstm_proj(x2, y3, g, w_qvo, w_k.T.astype(BF16), w_g, b_g, tm, seq_len=t)
    else:
        x1, q, k, v, o, gates = _mlstm_proj(x2, y3, g, w_qvo, w_k.astype(BF16), w_g, b_g, tm)
        kt = jnp.swapaxes(k.reshape(b, t, MLSTM_QK), 1, 2)
    q = q.reshape(b, t, MLSTM_QK)
    v = v.reshape(b, t, D_MODEL)
    o = o.reshape(b, t, D_MODEL)
    if not transposed:
        gcol = gates.reshape(b, t, 2, MLSTM_HEADS).transpose(0, 3, 1, 2).reshape(b, MLSTM_HEADS * t, 2)
        m_rep = jnp.repeat(m0, t, axis=1)
        h, c_f, n_f, m_f = _mlstm_short(q, kt, v, o, gcol, jnp.swapaxes(gcol, 1, 2), head_norm.reshape(1, D_MODEL),
                                        c0.reshape(b, MLSTM_QK, MLSTM_DV), n0.reshape(b, 1, MLSTM_QK),
                                        m_rep[:, :, None], m_rep[:, None, :], bb)
        return (x1, h.reshape(b * t, D_MODEL), c_f.reshape(b, MLSTM_HEADS, MLSTM_DK, MLSTM_DV),
                n_f.reshape(b, MLSTM_HEADS, MLSTM_DK), m_f[:, 0, ::t])
    gates = gates.reshape(b, t, 2, pairs, 2)
    gcol = gates.transpose(0, 3, 1, 2, 4).reshape(b, pairs, t, 4)
    grow = jnp.swapaxes(gcol, 2, 3)
    c0 = c0.reshape(b, pairs, 2 * MLSTM_DK, MLSTM_DV)
    n0 = n0.reshape(b, pairs, 1, 2 * MLSTM_DK)
    m0 = jnp.repeat(m0.reshape(b, pairs, 1, 2), MLSTM_DK, axis=3)
    h, c_f, n_f, m_f = _mlstm(q, kt, v, o, gcol, grow, head_norm.reshape(1, D_MODEL), c0, n0, m0, chunk, pps, bb)
    c_f = c_f.reshape(b, MLSTM_HEADS, MLSTM_DK, MLSTM_DV)
    n_f = n_f.reshape(b, MLSTM_HEADS, MLSTM_DK)
    m_f = m_f.reshape(b, MLSTM_HEADS, MLSTM_DK)[:, :, 0]
    return x1, h.reshape(b * t, D_MODEL), c_f, n_f, m_f


def kernel(x_prompt, x_sample, cache_k, cache_v, cache_logf, page_table, state_C, state_n, state_m, norm_mix, norm_ffn, norm_final, fox_w_in, fox_b_f, fox_w_out, mlstm_w_in, mlstm_b_i, mlstm_b_f, mlstm_head_norm, mlstm_w_out, moe_w_group, moe_b_group, moe_w_router, moe_b_router, moe_w_gate, moe_w_up, moe_w_down):
    bp, tp, _ = x_prompt.shape
    bs, ts, _ = x_sample.shape
    n_p, n_s = bp * tp, bs * ts
    tm_p, tm_s = min(512, tp), min(512, n_s)
    xp = x_prompt.reshape(n_p, D_MODEL)
    xs = x_sample.reshape(n_s, D_MODEL)
    g_final = norm_final[None, :]
    chunk_p, chunk_s = min(4096, n_p), min(4096, n_s)

    def ffn(x, o, w_out, layer, tm, chunk):
        w_r, b_r = _router_weights(moe_w_group[layer], moe_b_group[layer], moe_w_router[layer], moe_b_router[layer])
        x_new, h3, route = _outproj_router(x, o, w_out, norm_ffn[layer][None, :], w_r, b_r, tm)
        return x_new, _moe(h3, route, moe_w_gate, moe_w_up, moe_w_down, layer, chunk)

    g0 = norm_mix[0][None, :]
    w_in = fox_w_in[0]
    wq, wk, wv, wf = (w_in[:, :D_MODEL], w_in[:, D_MODEL:2 * D_MODEL], w_in[:, 2 * D_MODEL:3 * D_MODEL],
                      w_in[:, 3 * D_MODEL:])
    b_f = fox_b_f[0].astype(F32)
    wq_b = wq.astype(BF16)
    qp, kpt, vpt, kptb, vptb, lfpt = _fox_proj(
        xp, g0, wq_b, wk.T.astype(BF16), wv.T.astype(BF16), _pad_rows(wf.T).astype(BF16),
        _pad_rows(b_f[:, None]), tm_p, BF16, seq_len=tp)
    qs, ks, vs, _, _, lfs = _fox_proj(
        xs, g0, wq_b, wk.astype(BF16), wv.astype(BF16), _pad_lanes(wf).astype(BF16),
        _pad_lanes(b_f[None, :]), tm_s, F32)

    c_t = _cumsum_lanes(lfpt.reshape(bp * FOX_HEADS, tp)).reshape(bp, FOX_HEADS // 2, 2, tp)
    op = _fox_prompt_attention(qp.reshape(bp, tp, D_MODEL), kptb, vptb, c_t, min(512, tp))
    n_pool, page = cache_k.shape[1], cache_k.shape[2]
    feat_major = lambda c: c.transpose(0, 2, 3, 1).reshape(n_pool, D_MODEL, page)
    seq_last = lambda a: jnp.swapaxes(a.reshape(bs, ts, -1), 1, 2)
    os_ = _fox_sample_attention(page_table, qs.reshape(bs, ts, D_MODEL), feat_major(cache_k[0]), feat_major(cache_v[0]),
                                jnp.swapaxes(cache_logf[0], 1, 2), seq_last(ks), seq_last(vs), seq_last(lfs))
    xp, yp3 = ffn(xp, op.reshape(n_p, D_MODEL), fox_w_out[0], 0, tm_p, chunk_p)
    xs, ys3 = ffn(xs, os_.reshape(n_s, D_MODEL), fox_w_out[0], 0, tm_s, chunk_s)

    g1 = norm_mix[1][None, :]
    zc = jnp.zeros((bp, MLSTM_HEADS, MLSTM_DK, MLSTM_DV), F32)
    zn = jnp.zeros((bp, MLSTM_HEADS, MLSTM_DK), F32)
    zm = jnp.zeros((bp, MLSTM_HEADS), F32)
    xp, hp, cpf, npf, mpf = _mlstm_mixer(xp, yp3, bp, tp, g1, mlstm_w_in[0], mlstm_b_i[0], mlstm_b_f[0],
                                         mlstm_head_norm[0], zc, zn, zm, tm_p, min(256, tp), 2, 1, True)
    xs, hs, csf, nsf, msf = _mlstm_mixer(xs, ys3, bs, ts, g1, mlstm_w_in[0], mlstm_b_i[0], mlstm_b_f[0],
                                         mlstm_head_norm[0], state_C[0], state_n[0], state_m[0], tm_s, ts,
                                         MLSTM_HEADS // 2, 4 if bs % 4 == 0 else 1, False)
    xp, yp3 = ffn(xp, hp, mlstm_w_out[0], 1, tm_p, chunk_p)
    xs, ys3 = ffn(xs, hs, mlstm_w_out[0], 1, tm_s, chunk_s)
    yp = _residual_norm(xp, yp3, g_final, tm_p)
    ys = _residual_norm(xs, ys3, g_final, tm_s)

    hd = (FOX_HEADS, FOX_HEAD_DIM)
    time_major = lambda a: a.reshape(bp, FOX_HEADS, -1, tp).transpose(0, 3, 1, 2)
    return (yp.reshape(bp, tp, D_MODEL), ys.reshape(bs, ts, D_MODEL),
            time_major(kpt)[None], time_major(vpt)[None], jnp.swapaxes(lfpt, 1, 2)[None],
            ks.reshape(1, bs, ts, *hd), vs.reshape(1, bs, ts, *hd), lfs.reshape(1, bs, ts, FOX_HEADS),
            cpf[None], npf[None], mpf[None], csf[None], nsf[None], msf[None])
```
